```python
import jax
import jax.numpy as jnp
from jax import lax

D_MODEL = 1024
BATCH = 2
SEQ = 8192
DEPTH = 1

HEAD_DIM = 64
NSA_HEADS = 8
NSA_KV = 2
SWA_HEADS = 8
SWA_KV = 1
CMP_STRIDE = 16
CMP_LEN = 2 * CMP_STRIDE
CMP_HIDDEN = 256
SEL_LEN = 64
SEL_TOPN = 16
SEL_LOCAL = 2
NSA_WINDOW = 512
SWA_WINDOW = 128
Q_BLOCK = 128
D_FF = 2816
ROPE_THETA = 10000.0
RMS_EPS = 1e-6
FFN_HALF = 0.5
NEG_INF = -1e30
FORCE = 1e9

NSA_Q_W = NSA_HEADS * HEAD_DIM
NSA_KV_W = NSA_KV * HEAD_DIM
NSA_GATE_W = 3 * NSA_HEADS
SWA_Q_W = SWA_HEADS * HEAD_DIM
SWA_KV_W = SWA_KV * HEAD_DIM
IN_WIDTHS = (NSA_Q_W, NSA_KV_W, NSA_KV_W, NSA_KV_W, NSA_KV_W, NSA_KV_W, NSA_KV_W, NSA_GATE_W,
             SWA_Q_W, SWA_KV_W, SWA_KV_W, D_MODEL, D_MODEL)

kernel_name = 'hybrid_nsa_swa_sink_macaron'


def _split_points():
    pts, acc = [], 0
    for w in IN_WIDTHS[:-1]:
        acc += w
        pts.append(acc)
    return pts


def _rmsnorm(x, g):
    xf = x.astype(jnp.float32)
    y = xf * lax.rsqrt(jnp.mean(xf * xf, axis=-1, keepdims=True) + RMS_EPS)
    return (y * g.astype(jnp.float32)).astype(x.dtype)


def _swiglu(x, w_gate, w_up, w_down):
    return (jax.nn.silu(x @ w_gate) * (x @ w_up)) @ w_down


def _rope(x, pos):
    half = HEAD_DIM // 2
    inv_freq = ROPE_THETA ** (-jnp.arange(half, dtype=jnp.float32) / half)
    ang = pos.astype(jnp.float32)[..., None] * inv_freq
    cos = jnp.cos(ang)[:, :, None, :]
    sin = jnp.sin(ang)[:, :, None, :]
    xf = x.astype(jnp.float32)
    x1, x2 = xf[..., :half], xf[..., half:]
    return jnp.concatenate([x1 * cos - x2 * sin, x2 * cos + x1 * sin], axis=-1).astype(x.dtype)


def _banded_attention(q, k, v, window, sink):
    B, S, H, D = q.shape
    G = k.shape[2]
    R = H // G
    nb = S // Q_BLOCK
    pad = -(-window // Q_BLOCK) * Q_BLOCK
    L = pad + Q_BLOCK
    kp = jnp.pad(k, ((0, 0), (pad, 0), (0, 0), (0, 0)))
    vp = jnp.pad(v, ((0, 0), (pad, 0), (0, 0), (0, 0)))
    idx = jnp.arange(nb)[:, None] * Q_BLOCK + jnp.arange(L)[None, :]
    kb = kp[:, idx]
    vb = vp[:, idx]
    qb = q.reshape(B, nb, Q_BLOCK, G, R, D)
    s = jnp.einsum('bnqgrd,bnkgd->bngrqk', qb, kb).astype(jnp.float32) * (D ** -0.5)
    qpos = jnp.arange(nb)[:, None] * Q_BLOCK + jnp.arange(Q_BLOCK)[None, :]
    kpos = idx - pad
    diff = qpos[:, :, None] - kpos[:, None, :]
    mask = (diff >= 0) & (diff < window) & (kpos[:, None, :] >= 0)
    s = jnp.where(mask[None, :, None, None], s, NEG_INF)
    if sink is None:
        p = jax.nn.softmax(s, axis=-1)
    else:
        sk = sink.astype(jnp.float32).reshape(G, R)[None, None, :, :, None, None]
        m = jnp.maximum(jnp.max(s, axis=-1, keepdims=True), sk)
        e = jnp.exp(s - m)
        p = e / (jnp.sum(e, axis=-1, keepdims=True) + jnp.exp(sk - m))
    o = jnp.einsum('bngrqk,bnkgd->bnqgrd', p.astype(vb.dtype), vb)
    return o.reshape(B, S, H, D)


def _compress(kv, pe, w1, w2):
    B, S, G, D = kv.shape
    c = kv.reshape(B, S // CMP_STRIDE, CMP_STRIDE, G, D)
    blocks = jnp.concatenate([c[:, :-1], c[:, 1:]], axis=2)
    blocks = blocks + pe[None, None, :, None, :]
    n_cmp = blocks.shape[1]
    flat = blocks.transpose(0, 1, 3, 2, 4).reshape(B, n_cmp, G, CMP_LEN * D)
    return jax.nn.gelu(flat @ w1) @ w2


def _nsa(q, k_cmp, v_cmp, k_slc, v_slc, k_win, v_win, gates, pos, pe_k, wk1, wk2, pe_v, wv1, wv2):
    B, S, H, D = q.shape
    G = NSA_KV
    R = H // G
    scale = D ** -0.5
    n_sel = S // SEL_LEN
    n_top = min(SEL_TOPN, n_sel)
    nb = S // Q_BLOCK
    qg = q.reshape(B, S, G, R, D)
    t = jnp.arange(S)

    kc = _compress(k_cmp, pe_k, wk1, wk2)
    vc = _compress(v_cmp, pe_v, wv1, wv2)
    n_cmp = kc.shape[1]
    kc = _rope(kc, pos[:, CMP_LEN - 1::CMP_STRIDE])
    end = jnp.arange(n_cmp) * CMP_STRIDE + CMP_LEN - 1
    vis = end[None, :] <= t[:, None]
    s = jnp.einsum('bsgrd,bcgd->bgrsc', qg, kc).astype(jnp.float32) * scale
    s = jnp.where(vis, s, NEG_INF)
    p_cmp = jnp.where(vis, jax.nn.softmax(s, axis=-1), 0.0)
    o_cmp = jnp.einsum('bgrsc,bcgd->bsgrd', p_cmp.astype(vc.dtype), vc)

    cs = jnp.arange(n_cmp) * CMP_STRIDE
    ss = jnp.arange(n_sel) * SEL_LEN
    overlap = jnp.clip(jnp.minimum(cs[:, None] + CMP_LEN, ss[None, :] + SEL_LEN)
                       - jnp.maximum(cs[:, None], ss[None, :]), 0, None).astype(jnp.float32) / CMP_LEN
    imp = jnp.einsum('bgrsc,cj->bgsj', p_cmp, overlap)
    blk = jnp.arange(n_sel)[None, :]
    tb = (t // SEL_LEN)[:, None]
    forced = (blk == 0) | ((tb - blk >= 0) & (tb - blk < SEL_LOCAL))
    imp = jnp.where(forced, FORCE, jnp.where(blk > tb, -FORCE, imp))
    _, sel = lax.top_k(imp, n_top)

    kT = k_slc.transpose(0, 2, 1, 3)
    vT = v_slc.transpose(0, 2, 1, 3)
    offs = jnp.arange(SEL_LEN)
    gather = jax.vmap(jax.vmap(lambda a, i: a[i]))

    def one_block(args):
        qb, selb, tq = args
        idx = (selb[..., None] * SEL_LEN + offs).reshape(B, G, Q_BLOCK, n_top * SEL_LEN)
        kb = gather(kT, idx)
        vb = gather(vT, idx)
        sc = jnp.einsum('bqgrd,bgqtd->bgrqt', qb, kb).astype(jnp.float32) * scale
        m = (idx <= tq[None, None, :, None])[:, :, None]
        pr = jax.nn.softmax(jnp.where(m, sc, NEG_INF), axis=-1)
        return jnp.einsum('bgrqt,bgqtd->bqgrd', pr.astype(vb.dtype), vb)

    qs = qg.reshape(B, nb, Q_BLOCK, G, R, D).transpose(1, 0, 2, 3, 4, 5)
    sels = sel.reshape(B, G, nb, Q_BLOCK, n_top).transpose(2, 0, 1, 3, 4)
    tqs = t.reshape(nb, Q_BLOCK)
    o_slc = lax.map(one_block, (qs, sels, tqs))
    o_slc = o_slc.transpose(1, 0, 2, 3, 4, 5).reshape(B, S, G, R, D)

    o_win = _banded_attention(q, k_win, v_win, NSA_WINDOW, None).reshape(B, S, G, R, D)

    g = jax.nn.sigmoid(gates.astype(jnp.float32)).reshape(B, S, 3, G, R, 1)
    o = (g[:, :, 0] * o_cmp.astype(jnp.float32) + g[:, :, 1] * o_slc.astype(jnp.float32)
         + g[:, :, 2] * o_win.astype(jnp.float32))
    return o.astype(q.dtype).reshape(B, S, H * D)


def setup_inputs(seed: int = 0) -> dict:
    key = jax.random.key(seed)
    ks = jax.random.split(key, 24)
    f32 = jnp.float32

    def w(k, shape, fan_in):
        return jax.random.normal(k, shape, f32) * fan_in ** -0.5

    def gain(k, shape):
        return 1.0 + 0.02 * jax.random.normal(k, shape, f32)

    w_in_width = sum(IN_WIDTHS)
    return {
        'x': jax.random.normal(ks[0], (BATCH, SEQ, D_MODEL), f32),
        'positions': jnp.broadcast_to(jnp.arange(SEQ, dtype=jnp.int32), (BATCH, SEQ)),
        'norm_ffn1': gain(ks[1], (DEPTH, D_MODEL)),
        'ffn1_gate': w(ks[2], (DEPTH, D_MODEL, D_FF), D_MODEL),
        'ffn1_up': w(ks[3], (DEPTH, D_MODEL, D_FF), D_MODEL),
        'ffn1_down': w(ks[4], (DEPTH, D_FF, D_MODEL), D_FF),
        'norm_mix': gain(ks[5], (DEPTH, D_MODEL)),
        'w_in': w(ks[6], (DEPTH, D_MODEL, w_in_width), D_MODEL),
        'cmp_pe_k': 0.1 * jax.random.normal(ks[7], (DEPTH, CMP_LEN, HEAD_DIM), f32),
        'cmp_k_w1': w(ks[8], (DEPTH, CMP_LEN * HEAD_DIM, CMP_HIDDEN), CMP_LEN * HEAD_DIM),
        'cmp_k_w2': w(ks[9], (DEPTH, CMP_HIDDEN, HEAD_DIM), CMP_HIDDEN),
        'cmp_pe_v': 0.1 * jax.random.normal(ks[10], (DEPTH, CMP_LEN, HEAD_DIM), f32),
        'cmp_v_w1': w(ks[11], (DEPTH, CMP_LEN * HEAD_DIM, CMP_HIDDEN), CMP_LEN * HEAD_DIM),
        'cmp_v_w2': w(ks[12], (DEPTH, CMP_HIDDEN, HEAD_DIM), CMP_HIDDEN),
        'swa_sinks': jax.random.normal(ks[13], (DEPTH, SWA_HEADS), f32),
        'w_branch_a': w(ks[14], (DEPTH, NSA_Q_W, D_MODEL), NSA_Q_W),
        'w_branch_b': w(ks[15], (DEPTH, SWA_Q_W, D_MODEL), SWA_Q_W),
        'w_out': w(ks[16], (DEPTH, D_MODEL, D_MODEL), D_MODEL),
        'norm_ffn2': gain(ks[17], (DEPTH, D_MODEL)),
        'ffn2_gate': w(ks[18], (DEPTH, D_MODEL, D_FF), D_MODEL),
        'ffn2_up': w(ks[19], (DEPTH, D_MODEL, D_FF), D_MODEL),
        'ffn2_down': w(ks[20], (DEPTH, D_FF, D_MODEL), D_FF),
        'norm_final': gain(ks[21], (D_MODEL,)),
    }


def reference(x, positions, norm_ffn1, ffn1_gate, ffn1_up, ffn1_down, norm_mix, w_in,
              cmp_pe_k, cmp_k_w1, cmp_k_w2, cmp_pe_v, cmp_v_w1, cmp_v_w2, swa_sinks,
              w_branch_a, w_branch_b, w_out, norm_ffn2, ffn2_gate, ffn2_up, ffn2_down, norm_final):
    B, S, _ = x.shape
    pts = _split_points()
    h = x
    for i in range(DEPTH):
        h = h + FFN_HALF * _swiglu(_rmsnorm(h, norm_ffn1[i]), ffn1_gate[i], ffn1_up[i], ffn1_down[i])

        u = _rmsnorm(h, norm_mix[i])
        (nq, kc, vc, ksl, vsl, kwn, vwn, ng, sq, sk, sv, ga, gb) = jnp.split(u @ w_in[i], pts, axis=-1)
        nq = _rope(nq.reshape(B, S, NSA_HEADS, HEAD_DIM), positions)
        kc = kc.reshape(B, S, NSA_KV, HEAD_DIM)
        vc = vc.reshape(B, S, NSA_KV, HEAD_DIM)
        ksl = _rope(ksl.reshape(B, S, NSA_KV, HEAD_DIM), positions)
        vsl = vsl.reshape(B, S, NSA_KV, HEAD_DIM)
        kwn = _rope(kwn.reshape(B, S, NSA_KV, HEAD_DIM), positions)
        vwn = vwn.reshape(B, S, NSA_KV, HEAD_DIM)
        o_a = _nsa(nq, kc, vc, ksl, vsl, kwn, vwn, ng, positions,
                   cmp_pe_k[i], cmp_k_w1[i], cmp_k_w2[i], cmp_pe_v[i], cmp_v_w1[i], cmp_v_w2[i])

        sq = _rope(sq.reshape(B, S, SWA_HEADS, HEAD_DIM), positions)
        sk = _rope(sk.reshape(B, S, SWA_KV, HEAD_DIM), positions)
        sv = sv.reshape(B, S, SWA_KV, HEAD_DIM)
        o_b = _banded_attention(sq, sk, sv, SWA_WINDOW, swa_sinks[i]).reshape(B, S, SWA_Q_W)

        merged = jax.nn.sigmoid(ga) * (o_a @ w_branch_a[i]) + jax.nn.sigmoid(gb) * (o_b @ w_branch_b[i])
        h = h + merged @ w_out[i]

        h = h + FFN_HALF * _swiglu(_rmsnorm(h, norm_ffn2[i]), ffn2_gate[i], ffn2_up[i], ffn2_down[i])
    return _rmsnorm(h, norm_final)
```

```python
import functools

import numpy as np
import jax
import jax.numpy as jnp
from jax import lax
from jax.experimental import pallas as pl
from jax.experimental.pallas import tpu as pltpu

HEAD_DIM = 64
HALF_DIM = HEAD_DIM // 2
NSA_HEADS = 8
NSA_KV = 2
NSA_REP = NSA_HEADS // NSA_KV
SWA_HEADS = 8
CMP_STRIDE = 16
CMP_LEN = 2 * CMP_STRIDE
SEL_LEN = 64
SEL_SHIFT = 6
SEL_TOPN = 16
SEL_LOCAL = 2
NSA_WINDOW = 512
SWA_WINDOW = 128
ROPE_THETA = 10000.0
RMS_EPS = 1e-6
FFN_HALF = 0.5
NEG_INF = -1e30
FORCE = 1e9
SCALE = HEAD_DIM ** -0.5

NSA_Q_W = NSA_HEADS * HEAD_DIM
NSA_KV_W = NSA_KV * HEAD_DIM
NSA_GATE_W = 3 * NSA_HEADS
SWA_Q_W = SWA_HEADS * HEAD_DIM
SWA_KV_W = HEAD_DIM

VMEM_LIMIT_BYTES = 56 * 1024 * 1024

BF16 = jnp.bfloat16
F32 = jnp.float32

ROW_TILE = 512
NSA_TILE = 256
SWA_TILE = 128


def _params(n_axes):
    return pltpu.CompilerParams(dimension_semantics=("arbitrary",) * n_axes,
                                vmem_limit_bytes=VMEM_LIMIT_BYTES)


def _resident(shape):
    zeros = (0,) * len(shape)
    return pl.BlockSpec(shape, lambda *_: zeros, pipeline_mode=pl.Buffered(1))


def _rms(x, g):
    y = x * lax.rsqrt(jnp.mean(x * x, axis=-1, keepdims=True) + RMS_EPS)
    return y * g


def _dot(a, b):
    return jnp.dot(a, b, preferred_element_type=F32)


def _dot_nt(a, b):
    return lax.dot_general(a, b, (((1,), (1,)), ((), ())), preferred_element_type=F32)


def _ffn_kernel(x_ref, g_ref, wg_ref, wu_ref, wd_ref, gf_ref, o_ref, *, final_norm):
    x = x_ref[...]
    xb = _rms(x, g_ref[...]).astype(BF16)
    a = _dot(xb, wg_ref[...])
    b = _dot(xb, wu_ref[...])
    t = (a * jax.nn.sigmoid(a)) * b
    h = x + FFN_HALF * _dot(t.astype(BF16), wd_ref[...])
    if final_norm:
        h = _rms(h, gf_ref[...])
    o_ref[...] = h


def _ffn(x2, g, wg, wu, wd, gf, final_norm):
    n, d = x2.shape
    f = wg.shape[1]
    tm = ROW_TILE
    row = pl.BlockSpec((tm, d), lambda i: (i, 0))
    return pl.pallas_call(
        functools.partial(_ffn_kernel, final_norm=final_norm),
        grid=(n // tm,),
        in_specs=[row, _resident((1, d)), _resident((d, f)), _resident((d, f)), _resident((f, d)),
                  _resident((1, d))],
        out_specs=row,
        out_shape=jax.ShapeDtypeStruct((n, d), F32),
        compiler_params=_params(1),
        name="ffn_final" if final_norm else "ffn",
    )(x2, g, wg, wu, wd, gf)


def _proj_kernel(h_ref, g_ref, wt_ref, wn_ref, cos_t_ref, sin_t_ref, cos_n_ref, sin_n_ref,
                 qn_ref, sq_ref, vsl_ref, vwn_ref, sv_ref, ng_ref, kcv_ref, kr_ref, gab_ref):
    ub = _rms(h_ref[...], g_ref[...]).astype(BF16)
    yt = _dot_nt(wt_ref[...], ub)
    cos_t = cos_t_ref[...]
    sin_t = sin_t_ref[...]

    def rope_t(block, out_ref):
        for hd in range(block.shape[0] // HEAD_DIM):
            x1 = block[hd * HEAD_DIM:hd * HEAD_DIM + HALF_DIM]
            x2 = block[hd * HEAD_DIM + HALF_DIM:(hd + 1) * HEAD_DIM]
            out_ref[hd * HEAD_DIM:hd * HEAD_DIM + HALF_DIM, :] = (
                (x1 * cos_t - x2 * sin_t) * SCALE).astype(out_ref.dtype)
            out_ref[hd * HEAD_DIM + HALF_DIM:(hd + 1) * HEAD_DIM, :] = (
                (x2 * cos_t + x1 * sin_t) * SCALE).astype(out_ref.dtype)

    o = 0
    rope_t(yt[o:o + NSA_Q_W], qn_ref); o += NSA_Q_W
    rope_t(yt[o:o + SWA_Q_W], sq_ref); o += SWA_Q_W
    vsl_ref[...] = yt[o:o + NSA_KV_W].astype(BF16); o += NSA_KV_W
    vwn_ref[...] = yt[o:o + NSA_KV_W].astype(BF16); o += NSA_KV_W
    sv_ref[...] = yt[o:o + SWA_KV_W].astype(BF16); o += SWA_KV_W
    ng_ref[...] = jax.nn.sigmoid(yt[o:o + 32])

    yn = _dot(ub, wn_ref[...])
    kcv_ref[...] = yn[:, 0:256]
    cos_n = jnp.concatenate([cos_n_ref[...]] * 3, axis=1)
    sin_n = jnp.concatenate([sin_n_ref[...]] * 3, axis=1)
    kr_ref[...] = (yn[:, 256:640] * cos_n + yn[:, 640:1024] * sin_n).astype(BF16)
    gab_ref[...] = jax.nn.sigmoid(yn[:, 1024:])


def _proj(h2, g, wt, wn, cos_t, sin_t, cos_n, sin_n):
    n, d = h2.shape
    tm = ROW_TILE
    rows = lambda w: pl.BlockSpec((tm, w), lambda i: (i, 0))
    cols = lambda w: pl.BlockSpec((w, tm), lambda i: (0, i))
    out_shape = [
        jax.ShapeDtypeStruct((NSA_Q_W, n), BF16), jax.ShapeDtypeStruct((SWA_Q_W, n), BF16),
        jax.ShapeDtypeStruct((NSA_KV_W, n), BF16), jax.ShapeDtypeStruct((NSA_KV_W, n), BF16),
        jax.ShapeDtypeStruct((SWA_KV_W, n), BF16), jax.ShapeDtypeStruct((32, n), F32),
        jax.ShapeDtypeStruct((n, 256), F32), jax.ShapeDtypeStruct((n, 384), BF16),
        jax.ShapeDtypeStruct((n, 2 * d), F32),
    ]
    return pl.pallas_call(
        _proj_kernel,
        grid=(n // tm,),
        in_specs=[rows(d), _resident((1, d)), _resident(wt.shape), _resident(wn.shape),
                  cols(HALF_DIM), cols(HALF_DIM), rows(128), rows(128)],
        out_specs=[cols(NSA_Q_W), cols(SWA_Q_W), cols(NSA_KV_W), cols(NSA_KV_W), cols(SWA_KV_W),
                   cols(32), rows(256), rows(384), rows(2 * d)],
        out_shape=out_shape,
        compiler_params=_params(1),
        name="proj",
    )(h2, g, wt, wn, cos_t, sin_t, cos_n, sin_n)


def _gelu(x):
    return jax.nn.gelu(x, approximate=True)


def _compress_kernel(ak_ref, av_ref, pek_ref, pev_ref, w1k_ref, w1v_ref, w2k_ref, w2ks_ref, w2vt_ref,
                     cos_ref, sin_ref, kc_ref, vct_ref):
    nc = ak_ref.shape[1]

    def hidden(a_ref, pe_ref, w1_ref):
        a = a_ref[0]
        top = _dot((a + pe_ref[0:1, :]).astype(BF16), w1_ref[0])
        bot = _dot((a + pe_ref[1:2, :]).astype(BF16), w1_ref[1])
        return _gelu(top + pltpu.roll(bot, shift=nc - 1, axis=0)).astype(BF16)

    hk = hidden(ak_ref, pek_ref, w1k_ref)
    kc = _dot(hk, w2k_ref[...]) * cos_ref[0] + _dot(hk, w2ks_ref[...]) * sin_ref[0]
    kc_ref[0] = kc.astype(BF16)
    hv = hidden(av_ref, pev_ref, w1v_ref)
    vct_ref[0] = _dot_nt(w2vt_ref[...], hv).astype(BF16)


def _compress(ak, av, pek, pev, w1k, w1v, w2k, w2ks, w2vt, cos_c, sin_c):
    bg, nc, cw = ak.shape
    hid = w1k.shape[-1]
    per = lambda *s: pl.BlockSpec((1,) + s, lambda i: (i,) + (0,) * len(s))
    return pl.pallas_call(
        _compress_kernel,
        grid=(bg,),
        in_specs=[per(nc, cw), per(nc, cw), _resident((2, cw)), _resident((2, cw)),
                  _resident((2, cw, hid)), _resident((2, cw, hid)), _resident((hid, HEAD_DIM)),
                  _resident((hid, HEAD_DIM)), _resident((HEAD_DIM, hid)),
                  per(nc, HEAD_DIM), per(nc, HEAD_DIM)],
        out_specs=[per(nc, HEAD_DIM), per(HEAD_DIM, nc)],
        out_shape=[jax.ShapeDtypeStruct((bg, nc, HEAD_DIM), BF16),
                   jax.ShapeDtypeStruct((bg, HEAD_DIM, nc), BF16)],
        compiler_params=_params(1),
        name="compress",
    )(ak, av, pek, pev, w1k, w1v, w2k, w2ks, w2vt, cos_c, sin_c)


def _stack_heads(q_block, n_heads):
    return jnp.concatenate([q_block[r * HEAD_DIM:(r + 1) * HEAD_DIM] for r in range(n_heads)], axis=1)


def _cmp_topk_kernel(q_ref, kc_ref, vct_ref, ov_ref, gate_ref, mb_ref, oc_ref, *, n_top):
    tq = q_ref.shape[1]
    nc = kc_ref.shape[1]
    n_sel = ov_ref.shape[0]
    rep = NSA_REP
    q0 = pl.program_id(2) * tq

    qs = _stack_heads(q_ref[...], rep)
    s = _dot(kc_ref[0], qs)
    cblk = lax.broadcasted_iota(jnp.int32, (nc, rep * tq), 0)
    lane = lax.broadcasted_iota(jnp.int32, (nc, rep * tq), 1)
    t_row = q0 + (lane & (tq - 1))
    vis = (cblk * CMP_STRIDE + (CMP_LEN - 1) <= t_row) & (cblk < nc - 1)
    s = jnp.where(vis, s, NEG_INF)
    m = jnp.max(s, axis=0, keepdims=True)
    e = jnp.where(vis, jnp.exp(s - m), 0.0)
    l = jnp.sum(e, axis=0, keepdims=True)
    p = e / jnp.where(l > 0.0, l, 1.0)

    oc = _dot(vct_ref[0], p.astype(BF16))
    for r in range(rep):
        oc_ref[r * HEAD_DIM:(r + 1) * HEAD_DIM, :] = oc[:, r * tq:(r + 1) * tq] * gate_ref[0, 0, r:r + 1, :]

    psum = p[:, 0:tq]
    for r in range(1, rep):
        psum = psum + p[:, r * tq:(r + 1) * tq]
    p_hi = psum.astype(BF16)
    p_lo = (psum - p_hi.astype(F32)).astype(BF16)
    imp = _dot(ov_ref[...], p_hi) + _dot(ov_ref[...], p_lo)

    blk = lax.broadcasted_iota(jnp.int32, (n_sel, tq), 0)
    tb = (q0 + lax.broadcasted_iota(jnp.int32, (n_sel, tq), 1)) >> SEL_SHIFT
    forced = (blk == 0) | ((tb - blk >= 0) & (tb - blk < SEL_LOCAL))
    imp = jnp.where(forced, FORCE, jnp.where(blk > tb, -FORCE, imp))

    def pick(_, carry):
        x, chosen = carry
        top = jnp.max(x, axis=0, keepdims=True)
        first = jnp.min(jnp.where(x == top, blk, n_sel), axis=0, keepdims=True)
        hit = blk == first
        return jnp.where(hit, -jnp.inf, x), jnp.where(hit, 0.0, chosen)

    _, bias = lax.fori_loop(0, n_top, pick, (imp, jnp.full((n_sel, tq), NEG_INF, F32)))
    mb_ref[0] = bias.astype(BF16)


def _cmp_topk(qn_t, kc, vct, ov, gates, b, g_kv, s_len, n_top):
    tq = NSA_TILE
    nq = s_len // tq
    nc = kc.shape[1]
    n_sel = ov.shape[0]
    rep = NSA_REP
    return pl.pallas_call(
        functools.partial(_cmp_topk_kernel, n_top=n_top),
        grid=(b, g_kv, nq),
        in_specs=[
            pl.BlockSpec((rep * HEAD_DIM, tq), lambda bi, gi, qi: (gi, bi * nq + qi)),
            pl.BlockSpec((1, nc, HEAD_DIM), lambda bi, gi, qi: (bi * g_kv + gi, 0, 0)),
            pl.BlockSpec((1, HEAD_DIM, nc), lambda bi, gi, qi: (bi * g_kv + gi, 0, 0)),
            _resident(ov.shape),
            pl.BlockSpec((1, 1, rep, tq), lambda bi, gi, qi: (0, gi, 0, bi * nq + qi)),
        ],
        out_specs=[
            pl.BlockSpec((1, n_sel, tq), lambda bi, gi, qi: (bi * g_kv + gi, 0, qi)),
            pl.BlockSpec((rep * HEAD_DIM, tq), lambda bi, gi, qi: (gi, bi * nq + qi)),
        ],
        out_shape=[jax.ShapeDtypeStruct((b * g_kv, n_sel, s_len), BF16),
                   jax.ShapeDtypeStruct((NSA_Q_W, b * s_len), F32)],
        compiler_params=_params(3),
        name="cmp_topk",
    )(qn_t, kc, vct, ov, gates)


def _flash_step(carry, s, v_t):
    m, l, acc = carry
    m_new = jnp.maximum(m, jnp.max(s, axis=0, keepdims=True))
    alpha = jnp.exp(m - m_new)
    p = jnp.exp(s - m_new)
    l = alpha * l + jnp.sum(p, axis=0, keepdims=True)
    acc = alpha * acc + _dot(v_t, p.astype(BF16))
    return m_new, l, acc


def _flash_init(width):
    return (jnp.full((1, width), NEG_INF, F32), jnp.zeros((1, width), F32),
            jnp.zeros((HEAD_DIM, width), F32))


def _rel_pos(tk, tq, n_heads):
    key = lax.broadcasted_iota(jnp.int32, (tk, n_heads * tq), 0)
    qry = lax.broadcasted_iota(jnp.int32, (tk, n_heads * tq), 1) & (tq - 1)
    return qry - key


def _nsa_kernel(q_ref, mb_ref, oc_ref, gate_ref, ka_ref, vs_ref, kw_ref, vw_ref, o_ref):
    tq = q_ref.shape[1]
    tk = ka_ref.shape[2]
    rep = NSA_REP
    width = rep * tq
    qi = pl.program_id(2)

    qs = _stack_heads(q_ref[...], rep)
    mb = mb_ref[0]
    q_aug = jnp.concatenate([qs, jnp.concatenate([mb] * rep, axis=1)], axis=0)
    rel = _rel_pos(tk, tq, rep)
    causal = jnp.where(rel >= 0, 0.0, NEG_INF)

    def sel_body(kj, carry):
        return _flash_step(carry, _dot(ka_ref[0, kj], q_aug), vs_ref[0, kj])

    carry = lax.fori_loop(0, qi, sel_body, _flash_init(width))
    _, l_s, acc_s = _flash_step(carry, _dot(ka_ref[0, qi], q_aug) + causal, vs_ref[0, qi])
    o_sel = acc_s / l_s

    n_back = NSA_WINDOW // tk
    carry = _flash_init(width)
    for back in range(n_back, -1, -1):
        kj = qi - back
        kjc = jnp.maximum(kj, 0)
        dist = rel + (back * tk + jnp.where(kj >= 0, 0, NSA_WINDOW))
        bias = jnp.where((dist >= 0) & (dist < NSA_WINDOW), 0.0, NEG_INF)
        carry = _flash_step(carry, _dot(kw_ref[0, kjc], qs) + bias, vw_ref[0, kjc])
    _, l_w, acc_w = carry
    o_win = acc_w / l_w

    heads = []
    for r in range(rep):
        sl = slice(r * tq, (r + 1) * tq)
        heads.append(oc_ref[r * HEAD_DIM:(r + 1) * HEAD_DIM, :]
                     + gate_ref[1, 0, r:r + 1, :] * o_sel[:, sl]
                     + gate_ref[2, 0, r:r + 1, :] * o_win[:, sl])
    o_ref[...] = jnp.concatenate(heads, axis=0).T.astype(o_ref.dtype)


def _nsa(qn_t, mb, oc_t, gates, ka, vs, kw, vw, b, g_kv, s_len):
    tq = NSA_TILE
    nq = s_len // tq
    rep = NSA_REP
    n_sel = mb.shape[1]
    kv = lambda arr: pl.BlockSpec((1,) + arr.shape[1:], lambda bi, gi, qi: (bi * g_kv + gi, 0, 0, 0))
    return pl.pallas_call(
        _nsa_kernel,
        grid=(b, g_kv, nq),
        in_specs=[
            pl.BlockSpec((rep * HEAD_DIM, tq), lambda bi, gi, qi: (gi, bi * nq + qi)),
            pl.BlockSpec((1, n_sel, tq), lambda bi, gi, qi: (bi * g_kv + gi, 0, qi)),
            pl.BlockSpec((rep * HEAD_DIM, tq), lambda bi, gi, qi: (gi, bi * nq + qi)),
            pl.BlockSpec((3, 1, rep, tq), lambda bi, gi, qi: (0, gi, 0, bi * nq + qi)),
            kv(ka), kv(vs), kv(kw), kv(vw),
        ],
        out_specs=pl.BlockSpec((tq, rep * HEAD_DIM), lambda bi, gi, qi: (bi * nq + qi, gi)),
        out_shape=jax.ShapeDtypeStruct((b * s_len, NSA_Q_W), BF16),
        compiler_params=_params(3),
        name="nsa",
    )(qn_t, mb, oc_t, gates, ka, vs, kw, vw)


def _swa_kernel(q_ref, sink_ref, k_ref, v_ref, o_ref):
    tq = q_ref.shape[1]
    tk = k_ref.shape[2]
    heads = SWA_HEADS
    width = heads * tq
    qi = pl.program_id(1)

    qs = _stack_heads(q_ref[...], heads)
    rel = _rel_pos(tk, tq, heads)
    n_back = SWA_WINDOW // tk
    carry = _flash_init(width)
    for back in range(n_back, -1, -1):
        kj = qi - back
        kjc = jnp.maximum(kj, 0)
        dist = rel + (back * tk + jnp.where(kj >= 0, 0, SWA_WINDOW))
        bias = jnp.where((dist >= 0) & (dist < SWA_WINDOW), 0.0, NEG_INF)
        carry = _flash_step(carry, _dot(k_ref[0, kjc], qs) + bias, v_ref[0, kjc])
    m, l, acc = carry
    sink = sink_ref[...]
    m_all = jnp.maximum(m, sink)
    scale = jnp.exp(m - m_all)
    o = acc * scale / (l * scale + jnp.exp(sink - m_all))
    o_ref[...] = jnp.concatenate(
        [o[:, r * tq:(r + 1) * tq] for r in range(heads)], axis=0).T.astype(o_ref.dtype)


def _swa(sq_t, sink_row, k, v, b, s_len):
    tq = SWA_TILE
    nq = s_len // tq
    kv = lambda arr: pl.BlockSpec((1,) + arr.shape[1:], lambda bi, qi: (bi, 0, 0, 0))
    return pl.pallas_call(
        _swa_kernel,
        grid=(b, nq),
        in_specs=[pl.BlockSpec((SWA_Q_W, tq), lambda bi, qi: (0, bi * nq + qi)),
                  _resident(sink_row.shape), kv(k), kv(v)],
        out_specs=pl.BlockSpec((tq, SWA_Q_W), lambda bi, qi: (bi * nq + qi, 0)),
        out_shape=jax.ShapeDtypeStruct((b * s_len, SWA_Q_W), BF16),
        compiler_params=_params(2),
        name="swa",
    )(sq_t, sink_row, k, v)


def _merge_kernel(h_ref, oa_ref, ob_ref, gab_ref, wa_ref, wb_ref, wo_ref, o_ref):
    d = h_ref.shape[1]
    gab = gab_ref[...]
    merged = gab[:, :d] * _dot(oa_ref[...], wa_ref[...]) + gab[:, d:] * _dot(ob_ref[...], wb_ref[...])
    o_ref[...] = h_ref[...] + _dot(merged.astype(BF16), wo_ref[...])


def _merge(h2, oa, ob, gab, wa, wb, wo):
    n, d = h2.shape
    tm = ROW_TILE
    rows = lambda w: pl.BlockSpec((tm, w), lambda i: (i, 0))
    return pl.pallas_call(
        _merge_kernel,
        grid=(n // tm,),
        in_specs=[rows(d), rows(NSA_Q_W), rows(SWA_Q_W), rows(2 * d),
                  _resident(wa.shape), _resident(wb.shape), _resident(wo.shape)],
        out_specs=rows(d),
        out_shape=jax.ShapeDtypeStruct((n, d), F32),
        compiler_params=_params(1),
        name="merge",
    )(h2, oa, ob, gab, wa, wb, wo)


def _swap_halves_cols(w):
    k, c = w.shape
    return w.reshape(k, c // HEAD_DIM, 2, HALF_DIM)[:, :, ::-1, :].reshape(k, c)


def _overlap_matrix(n_sel, n_cmp, n_cmp_pad):
    cs = np.arange(n_cmp) * CMP_STRIDE
    ss = np.arange(n_sel) * SEL_LEN
    ov = np.clip(np.minimum(cs[None, :] + CMP_LEN, ss[:, None] + SEL_LEN)
                 - np.maximum(cs[None, :], ss[:, None]), 0, None).astype(np.float32) / CMP_LEN
    return np.pad(ov, ((0, 0), (0, n_cmp_pad - n_cmp)))


def _tile_major(x_t, b, groups, s_len, tk):
    x = x_t.reshape(groups, HEAD_DIM, b, s_len // tk, tk)
    return x.transpose(2, 0, 3, 1, 4).reshape(b * groups, s_len // tk, HEAD_DIM, tk)


def _rows_by_group(x, b, groups, s_len):
    return x.reshape(b, s_len, groups, HEAD_DIM).transpose(0, 2, 1, 3)


def _layer(h, positions, w, b, s_len):
    n, d = h.shape
    g_kv = NSA_KV
    n_sel = s_len // SEL_LEN
    n_top = min(SEL_TOPN, n_sel)
    nc = s_len // CMP_STRIDE
    bf = lambda a: a.astype(BF16)

    h = _ffn(h, w['norm_ffn1'][None], bf(w['ffn1_gate']), bf(w['ffn1_up']), bf(w['ffn1_down']),
             w['norm_ffn1'][None], False)

    pts = np.cumsum((NSA_Q_W,) + (NSA_KV_W,) * 6 + (NSA_GATE_W, SWA_Q_W, SWA_KV_W, SWA_KV_W, d, d))[:-1]
    (w_nq, w_kc, w_vc, w_ksl, w_vsl, w_kwn, w_vwn, w_ng, w_sq, w_sk, w_sv, w_ga, w_gb) = jnp.split(
        w['w_in'], pts, axis=1)
    w_t = jnp.concatenate([w_nq, w_sq, w_vsl, w_vwn, w_sv, w_ng,
                           jnp.zeros((d, 32 - NSA_GATE_W), F32)], axis=1).T
    w_k = jnp.concatenate([w_ksl, w_kwn, w_sk, jnp.zeros((d, 64), F32)], axis=1)
    w_n = jnp.concatenate([w_kc, w_vc, w_k, _swap_halves_cols(w_k), w_ga, w_gb], axis=1)

    inv_freq = ROPE_THETA ** (-jnp.arange(HALF_DIM, dtype=F32) / HALF_DIM)
    ang = positions.astype(F32)[..., None] * inv_freq
    cos, sin = jnp.cos(ang), jnp.sin(ang)
    cos_n = jnp.concatenate([cos, cos], axis=-1)
    sin_n = jnp.concatenate([-sin, sin], axis=-1)
    tile2 = lambda a: jnp.concatenate([a, a], axis=-1).reshape(n, 2 * HEAD_DIM)

    (qn_t, sq_t, vsl_t, vwn_t, sv_t, ng_t, kcv, kr, gab) = _proj(
        h, w['norm_mix'][None], bf(w_t), bf(w_n),
        cos.reshape(n, HALF_DIM).T, sin.reshape(n, HALF_DIM).T, tile2(cos_n), tile2(sin_n))

    def chunk_rows(x):
        return (x.reshape(b, nc, CMP_STRIDE, g_kv, HEAD_DIM).transpose(0, 3, 1, 2, 4)
                .reshape(b * g_kv, nc, CMP_STRIDE * HEAD_DIM))
    cpos = slice(CMP_LEN - 1, None, CMP_STRIDE)
    pad_c = lambda a: jnp.repeat(jnp.pad(a[:, cpos], ((0, 0), (0, 1), (0, 0))), g_kv, axis=0)
    w1 = lambda a: bf(a.reshape(2, CMP_STRIDE * HEAD_DIM, a.shape[-1]))
    kc, vc_t = _compress(
        chunk_rows(kcv[:, :NSA_KV_W]), chunk_rows(kcv[:, NSA_KV_W:]),
        w['cmp_pe_k'].reshape(2, CMP_STRIDE * HEAD_DIM), w['cmp_pe_v'].reshape(2, CMP_STRIDE * HEAD_DIM),
        w1(w['cmp_k_w1']), w1(w['cmp_v_w1']), bf(w['cmp_k_w2']), bf(_swap_halves_cols(w['cmp_k_w2'])),
        bf(w['cmp_v_w2'].T), pad_c(cos_n), pad_c(sin_n))

    gates = ng_t[:NSA_GATE_W].reshape(3, g_kv, NSA_REP, n)
    ov = jnp.asarray(_overlap_matrix(n_sel, nc - 1, nc), BF16)
    mb, oc_t = _cmp_topk(qn_t, kc, vc_t, ov, gates, b, g_kv, s_len, n_top)

    tk = NSA_TILE
    expand = jnp.asarray(np.arange(s_len)[:, None] // SEL_LEN == np.arange(n_sel)[None, :], BF16)
    k_sel = _rows_by_group(kr[:, 0:128], b, g_kv, s_len)
    k_aug = jnp.concatenate([k_sel, jnp.broadcast_to(expand, (b, g_kv, s_len, n_sel))], axis=-1)
    k_aug = k_aug.reshape(b * g_kv, s_len // tk, tk, HEAD_DIM + n_sel)
    k_win = _rows_by_group(kr[:, 128:256], b, g_kv, s_len).reshape(b * g_kv, s_len // tk, tk, HEAD_DIM)
    o_a = _nsa(qn_t, mb, oc_t, gates, k_aug, _tile_major(vsl_t, b, g_kv, s_len, tk), k_win,
               _tile_major(vwn_t, b, g_kv, s_len, tk), b, g_kv, s_len)

    ts = SWA_TILE
    k_swa = kr[:, 256:320].reshape(b, s_len // ts, ts, HEAD_DIM)
    sink_row = jnp.repeat(w['swa_sinks'].astype(F32), ts)[None]
    o_b = _swa(sq_t, sink_row, k_swa, _tile_major(sv_t, b, 1, s_len, ts), b, s_len)

    return _merge(h, o_a, o_b, gab, bf(w['w_branch_a']), bf(w['w_branch_b']), bf(w['w_out']))


def kernel(x, positions, norm_ffn1, ffn1_gate, ffn1_up, ffn1_down, norm_mix, w_in, cmp_pe_k, cmp_k_w1, cmp_k_w2, cmp_pe_v, cmp_v_w1, cmp_v_w2, swa_sinks, w_branch_a, w_branch_b, w_out, norm_ffn2, ffn2_gate, ffn2_up, ffn2_down, norm_final):
    b, s_len, d = x.shape
    stacked = dict(norm_ffn1=norm_ffn1, ffn1_gate=ffn1_gate, ffn1_up=ffn1_up, ffn1_down=ffn1_down,
                   norm_mix=norm_mix, w_in=w_in, cmp_pe_k=cmp_pe_k, cmp_k_w1=cmp_k_w1, cmp_k_w2=cmp_k_w2,
                   cmp_pe_v=cmp_pe_v, cmp_v_w1=cmp_v_w1, cmp_v_w2=cmp_v_w2, swa_sinks=swa_sinks,
                   w_branch_a=w_branch_a, w_branch_b=w_branch_b, w_out=w_out)
    depth = norm_ffn1.shape[0]
    h = x.reshape(b * s_len, d)
    for i in range(depth):
        w = {k: v[i] for k, v in stacked.items()}
        h = _layer(h, positions, w, b, s_len)
        last = i == depth - 1
        h = _ffn(h, norm_ffn2[i][None], ffn2_gate[i].astype(BF16), ffn2_up[i].astype(BF16),
                 ffn2_down[i].astype(BF16), norm_final[None], last)
    return h.reshape(b, s_len, d)
```

```python
import functools

import numpy as np
import jax
import jax.numpy as jnp
from jax import lax
from jax.experimental import pallas as pl
from jax.experimental.pallas import tpu as pltpu

HEAD_DIM = 64
HALF_DIM = HEAD_DIM // 2
NSA_HEADS = 8
NSA_KV = 2
NSA_REP = NSA_HEADS // NSA_KV
SWA_HEADS = 8
CMP_STRIDE = 16
CMP_LEN = 2 * CMP_STRIDE
SEL_LEN = 64
SEL_SHIFT = 6
SEL_TOPN = 16
SEL_LOCAL = 2
NSA_WINDOW = 512
SWA_WINDOW = 128
ROPE_THETA = 10000.0
RMS_EPS = 1e-6
FFN_HALF = 0.5
NEG_INF = -1e30
FORCE = 1e9
LOG2E = 1.4426950408889634
Q_SCALE = HEAD_DIM ** -0.5 * LOG2E
BF16_ROWS = 16
VAL_ROWS = HEAD_DIM + BF16_ROWS

NSA_Q_W = NSA_HEADS * HEAD_DIM
NSA_KV_W = NSA_KV * HEAD_DIM
NSA_GATE_W = 3 * NSA_HEADS
SWA_Q_W = SWA_HEADS * HEAD_DIM
SWA_KV_W = HEAD_DIM

VMEM_LIMIT_BYTES = 56 * 1024 * 1024

BF16 = jnp.bfloat16
F32 = jnp.float32

ROW_TILE = 512
CMP_TILE = 256
NSA_TILE = 256
SWA_TILE = 128


def _params(n_axes):
    return pltpu.CompilerParams(dimension_semantics=("arbitrary",) * n_axes,
                                vmem_limit_bytes=VMEM_LIMIT_BYTES)


def _resident(shape):
    zeros = (0,) * len(shape)
    return pl.BlockSpec(shape, lambda *_: zeros, pipeline_mode=pl.Buffered(1))


def _rms(x, g):
    y = x * lax.rsqrt(jnp.mean(x * x, axis=-1, keepdims=True) + RMS_EPS)
    return y * g


def _dot(a, b):
    return jnp.dot(a, b, preferred_element_type=F32)


def _dot_nt(a, b):
    return lax.dot_general(a, b, (((1,), (1,)), ((), ())), preferred_element_type=F32)


def _ffn_kernel(x_ref, g_ref, wg_ref, wu_ref, wd_ref, gf_ref, o_ref, *, final_norm):
    x = x_ref[...]
    xb = _rms(x, g_ref[...]).astype(BF16)
    a = _dot(xb, wg_ref[...])
    b = _dot(xb, wu_ref[...])
    t = (a * jax.nn.sigmoid(a)) * b
    h = x + FFN_HALF * _dot(t.astype(BF16), wd_ref[...])
    if final_norm:
        h = _rms(h, gf_ref[...])
    o_ref[...] = h


def _ffn(x2, g, wg, wu, wd, gf, final_norm):
    n, d = x2.shape
    f = wg.shape[1]
    tm = ROW_TILE
    row = pl.BlockSpec((tm, d), lambda i: (i, 0))
    return pl.pallas_call(
        functools.partial(_ffn_kernel, final_norm=final_norm),
        grid=(n // tm,),
        in_specs=[row, _resident((1, d)), _resident((d, f)), _resident((d, f)), _resident((f, d)),
                  _resident((1, d))],
        out_specs=row,
        out_shape=jax.ShapeDtypeStruct((n, d), F32),
        compiler_params=_params(1),
        name="ffn_final" if final_norm else "ffn",
    )(x2, g, wg, wu, wd, gf)


def _proj_kernel(h_ref, g_ref, wt_ref, wn_ref, cos_t_ref, sin_t_ref, cos_n_ref, sin_n_ref,
                 qn_ref, sq_ref, vsl_ref, vwn_ref, sv_ref, ng_ref, kcv_ref, kr_ref, gab_ref):
    ub = _rms(h_ref[...], g_ref[...]).astype(BF16)
    yt = _dot_nt(wt_ref[...], ub)
    cos_t = cos_t_ref[...]
    sin_t = sin_t_ref[...]

    def rope_t(block, out_ref):
        for hd in range(block.shape[0] // HEAD_DIM):
            x1 = block[hd * HEAD_DIM:hd * HEAD_DIM + HALF_DIM]
            x2 = block[hd * HEAD_DIM + HALF_DIM:(hd + 1) * HEAD_DIM]
            out_ref[hd * HEAD_DIM:hd * HEAD_DIM + HALF_DIM, :] = (
                (x1 * cos_t - x2 * sin_t) * Q_SCALE).astype(out_ref.dtype)
            out_ref[hd * HEAD_DIM + HALF_DIM:(hd + 1) * HEAD_DIM, :] = (
                (x2 * cos_t + x1 * sin_t) * Q_SCALE).astype(out_ref.dtype)

    o = 0
    rope_t(yt[o:o + NSA_Q_W], qn_ref); o += NSA_Q_W
    rope_t(yt[o:o + SWA_Q_W], sq_ref); o += SWA_Q_W
    vsl_ref[...] = yt[o:o + NSA_KV_W].astype(BF16); o += NSA_KV_W
    vwn_ref[...] = yt[o:o + NSA_KV_W].astype(BF16); o += NSA_KV_W
    sv_ref[...] = yt[o:o + SWA_KV_W].astype(BF16); o += SWA_KV_W
    ng_ref[...] = jax.nn.sigmoid(yt[o:o + 32])

    yn = _dot(ub, wn_ref[...])
    kcv_ref[...] = yn[:, 0:256]
    cos_n = jnp.concatenate([cos_n_ref[...]] * 3, axis=1)
    sin_n = jnp.concatenate([sin_n_ref[...]] * 3, axis=1)
    kr_ref[...] = (yn[:, 256:640] * cos_n + yn[:, 640:1024] * sin_n).astype(BF16)
    gab_ref[...] = jax.nn.sigmoid(yn[:, 1024:])


def _proj(h2, g, wt, wn, cos_t, sin_t, cos_n, sin_n):
    n, d = h2.shape
    tm = ROW_TILE
    rows = lambda w: pl.BlockSpec((tm, w), lambda i: (i, 0))
    cols = lambda w: pl.BlockSpec((w, tm), lambda i: (0, i))
    out_shape = [
        jax.ShapeDtypeStruct((NSA_Q_W, n), BF16), jax.ShapeDtypeStruct((SWA_Q_W, n), BF16),
        jax.ShapeDtypeStruct((NSA_KV_W, n), BF16), jax.ShapeDtypeStruct((NSA_KV_W, n), BF16),
        jax.ShapeDtypeStruct((SWA_KV_W, n), BF16), jax.ShapeDtypeStruct((32, n), F32),
        jax.ShapeDtypeStruct((n, 256), F32), jax.ShapeDtypeStruct((n, 384), BF16),
        jax.ShapeDtypeStruct((n, 2 * d), F32),
    ]
    return pl.pallas_call(
        _proj_kernel,
        grid=(n // tm,),
        in_specs=[rows(d), _resident((1, d)), _resident(wt.shape), _resident(wn.shape),
                  cols(HALF_DIM), cols(HALF_DIM), rows(128), rows(128)],
        out_specs=[cols(NSA_Q_W), cols(SWA_Q_W), cols(NSA_KV_W), cols(NSA_KV_W), cols(SWA_KV_W),
                   cols(32), rows(256), rows(384), rows(2 * d)],
        out_shape=out_shape,
        compiler_params=_params(1),
        name="proj",
    )(h2, g, wt, wn, cos_t, sin_t, cos_n, sin_n)


def _gelu(x):
    return jax.nn.gelu(x, approximate=True)


def _compress_kernel(ak_ref, av_ref, pek_ref, pev_ref, w1k_ref, w1v_ref, w2k_ref, w2ks_ref, w2vt_ref,
                     cos_ref, sin_ref, kc_ref, vct_ref):
    nc = ak_ref.shape[1]

    def hidden(a_ref, pe_ref, w1_ref):
        a = a_ref[0]
        top = _dot((a + pe_ref[0:1, :]).astype(BF16), w1_ref[0])
        bot = _dot((a + pe_ref[1:2, :]).astype(BF16), w1_ref[1])
        return _gelu(top + pltpu.roll(bot, shift=nc - 1, axis=0)).astype(BF16)

    hk = hidden(ak_ref, pek_ref, w1k_ref)
    kc = _dot(hk, w2k_ref[...]) * cos_ref[0] + _dot(hk, w2ks_ref[...]) * sin_ref[0]
    kc_ref[0] = kc.astype(BF16)
    hv = hidden(av_ref, pev_ref, w1v_ref)
    vct_ref[0] = _dot_nt(w2vt_ref[...], hv).astype(BF16)


def _compress(ak, av, pek, pev, w1k, w1v, w2k, w2ks, w2vt, cos_c, sin_c):
    bg, nc, cw = ak.shape
    hid = w1k.shape[-1]
    per = lambda *s: pl.BlockSpec((1,) + s, lambda i: (i,) + (0,) * len(s))
    return pl.pallas_call(
        _compress_kernel,
        grid=(bg,),
        in_specs=[per(nc, cw), per(nc, cw), _resident((2, cw)), _resident((2, cw)),
                  _resident((2, cw, hid)), _resident((2, cw, hid)), _resident((hid, HEAD_DIM)),
                  _resident((hid, HEAD_DIM)), _resident((HEAD_DIM, hid)),
                  per(nc, HEAD_DIM), per(nc, HEAD_DIM)],
        out_specs=[per(nc, HEAD_DIM), per(HEAD_DIM, nc)],
        out_shape=[jax.ShapeDtypeStruct((bg, nc, HEAD_DIM), BF16),
                   jax.ShapeDtypeStruct((bg, HEAD_DIM, nc), BF16)],
        compiler_params=_params(1),
        name="compress",
    )(ak, av, pek, pev, w1k, w1v, w2k, w2ks, w2vt, cos_c, sin_c)


def _stack_heads(q_block, n_heads):
    return jnp.concatenate([q_block[r * HEAD_DIM:(r + 1) * HEAD_DIM] for r in range(n_heads)], axis=1)


def _cmp_topk_kernel(q_ref, kc_ref, vct_ref, ov_ref, gate_ref, mb_ref, oc_ref, *, n_top):
    tq = q_ref.shape[1]
    nc = kc_ref.shape[1]
    n_sel = ov_ref.shape[0]
    rep = NSA_REP
    q0 = pl.program_id(2) * tq

    qs = _stack_heads(q_ref[...], rep)
    s = _dot(kc_ref[0], qs)
    cblk = lax.broadcasted_iota(jnp.int32, (nc, rep * tq), 0)
    lane = lax.broadcasted_iota(jnp.int32, (nc, rep * tq), 1)
    t_row = q0 + (lane & (tq - 1))
    vis = (cblk * CMP_STRIDE + (CMP_LEN - 1) <= t_row) & (cblk < nc - 1)
    s = jnp.where(vis, s, NEG_INF)
    m = jnp.max(s, axis=0, keepdims=True)
    e = jnp.where(vis, jnp.exp2(s - m), 0.0)
    l = jnp.sum(e, axis=0, keepdims=True)
    p = e / jnp.where(l > 0.0, l, 1.0)

    oc = _dot(vct_ref[0], p.astype(BF16))
    for r in range(rep):
        oc_ref[r * HEAD_DIM:(r + 1) * HEAD_DIM, :] = oc[:, r * tq:(r + 1) * tq] * gate_ref[0, 0, r:r + 1, :]

    psum = p[:, 0:tq]
    for r in range(1, rep):
        psum = psum + p[:, r * tq:(r + 1) * tq]
    p_hi = psum.astype(BF16)
    p_lo = (psum - p_hi.astype(F32)).astype(BF16)
    imp = _dot(ov_ref[...], p_hi) + _dot(ov_ref[...], p_lo)

    blk = lax.broadcasted_iota(jnp.int32, (n_sel, tq), 0)
    tb = (q0 + lax.broadcasted_iota(jnp.int32, (n_sel, tq), 1)) >> SEL_SHIFT
    forced = (blk == 0) | ((tb - blk >= 0) & (tb - blk < SEL_LOCAL))
    imp = jnp.where(forced, FORCE, jnp.where(blk > tb, -FORCE, imp))

    def pick(_, carry):
        x, chosen = carry
        top = jnp.max(x, axis=0, keepdims=True)
        first = jnp.min(jnp.where(x == top, blk, n_sel), axis=0, keepdims=True)
        hit = blk == first
        return jnp.where(hit, -jnp.inf, x), jnp.where(hit, 0.0, chosen)

    _, bias = lax.fori_loop(0, n_top, pick, (imp, jnp.full((n_sel, tq), NEG_INF, F32)))
    mb_ref[0] = bias.astype(BF16)


def _cmp_topk(qn_t, kc, vct, ov, gates, b, g_kv, s_len, n_top):
    tq = CMP_TILE
    nq = s_len // tq
    nc = kc.shape[1]
    n_sel = ov.shape[0]
    rep = NSA_REP
    return pl.pallas_call(
        functools.partial(_cmp_topk_kernel, n_top=n_top),
        grid=(b, g_kv, nq),
        in_specs=[
            pl.BlockSpec((rep * HEAD_DIM, tq), lambda bi, gi, qi: (gi, bi * nq + qi)),
            pl.BlockSpec((1, nc, HEAD_DIM), lambda bi, gi, qi: (bi * g_kv + gi, 0, 0)),
            pl.BlockSpec((1, HEAD_DIM, nc), lambda bi, gi, qi: (bi * g_kv + gi, 0, 0)),
            _resident(ov.shape),
            pl.BlockSpec((1, 1, rep, tq), lambda bi, gi, qi: (0, gi, 0, bi * nq + qi)),
        ],
        out_specs=[
            pl.BlockSpec((1, n_sel, tq), lambda bi, gi, qi: (bi * g_kv + gi, 0, qi)),
            pl.BlockSpec((rep * HEAD_DIM, tq), lambda bi, gi, qi: (gi, bi * nq + qi)),
        ],
        out_shape=[jax.ShapeDtypeStruct((b * g_kv, n_sel, s_len), BF16),
                   jax.ShapeDtypeStruct((NSA_Q_W, b * s_len), F32)],
        compiler_params=_params(3),
        name="cmp_topk",
    )(qn_t, kc, vct, ov, gates)


def _softmax_tile(m, s):
    m_new = jnp.maximum(m, jnp.max(s, axis=0, keepdims=True))
    return m_new, jnp.exp2(m - m_new), jnp.exp2(s - m_new).astype(BF16)


def _flash_step(carry, s, v_aug):
    m, acc = carry
    m_new, alpha, p = _softmax_tile(m, s)
    return m_new, alpha * acc + _dot(v_aug, p)


def _flash_init(n_heads, tq):
    return tuple((jnp.full((1, tq), NEG_INF, F32), jnp.zeros((VAL_ROWS, tq), F32)) for _ in range(n_heads))


def _normalize(acc):
    return acc[:HEAD_DIM] / acc[HEAD_DIM:HEAD_DIM + 1]


def _flash_heads(state, k_tile, q_heads, v_t, bias=None):
    scores = [_dot(k_tile, q_h) for q_h in q_heads]
    if bias is not None:
        scores = [s + bias for s in scores]
    return tuple(_flash_step(carry, s, v_t) for carry, s in zip(state, scores))


def _rel_pos(tk, tq):
    return lax.broadcasted_iota(jnp.int32, (tk, tq), 1) - lax.broadcasted_iota(jnp.int32, (tk, tq), 0)


def _window_bias(rel, back, tk, kj, window):
    dist = rel + (back * tk + jnp.where(kj >= 0, 0, window))
    return jnp.where((dist >= 0) & (dist < window), 0.0, NEG_INF)


def _nsa_kernel(q_ref, mb_ref, oc_ref, gate_ref, ka_ref, vs_ref, kw_ref, vw_ref, o_ref,
                qa_buf, s_buf, p_buf, acc_buf, o_buf):
    tq = q_ref.shape[1]
    tk = ka_ref.shape[2]
    rep = NSA_REP
    heads = range(rep)
    qi = pl.program_id(2)

    mb = mb_ref[0]
    q_heads = [q_ref[r * HEAD_DIM:(r + 1) * HEAD_DIM, :] for r in heads]
    for r in heads:
        qa_buf[r] = jnp.concatenate([q_heads[r], mb], axis=0)
        s_buf[r] = _dot(ka_ref[0, 0], qa_buf[r])
        p_buf[r] = jnp.zeros((tk, tq), BF16)
        acc_buf[r] = jnp.zeros((VAL_ROWS, tq), F32)
    rel = _rel_pos(tk, tq)

    backs = list(range(NSA_WINDOW // tk, -1, -1))
    tiles = [jnp.maximum(qi - back, 0) for back in backs]
    win_scores = lambda n: [_dot(kw_ref[0, tiles[n]], q_h) for q_h in q_heads]
    win = _flash_init(rep, tq)
    ahead = win_scores(0)
    for n, back in enumerate(backs):
        scores, ahead = ahead, (win_scores(n + 1) if n + 1 < len(backs) else None)
        bias = _window_bias(rel, back, tk, qi - back, NSA_WINDOW)
        win = tuple(_flash_step(win[r], scores[r] + bias, vw_ref[0, tiles[n]]) for r in heads)
    for r in heads:
        o_buf[r] = (oc_ref[r * HEAD_DIM:(r + 1) * HEAD_DIM, :]
                    + gate_ref[2, 0, r:r + 1, :] * _normalize(win[r][1]))

    def stage(i, carry):
        m, alpha = carry
        k_next = ka_ref[0, i + 1]
        s_next = [_dot(k_next, qa_buf[r]) for r in heads]
        v_prev = vs_ref[0, jnp.maximum(i - 1, 0)]
        pv = [_dot(v_prev, p_buf[r]) for r in heads]
        out = []
        for r in heads:
            m_r, a_r, p_buf[r] = _softmax_tile(m[r], s_buf[r])
            out.append((m_r, a_r))
        for r in heads:
            acc_buf[r] = alpha[r] * acc_buf[r] + pv[r]
            s_buf[r] = s_next[r]
        return tuple(zip(*out))

    row = lambda v: tuple(jnp.full((1, tq), v, F32) for _ in heads)
    carry = lax.fori_loop(0, qi >> 1, lambda j, c: stage(2 * j + 1, stage(2 * j, c)),
                          (row(NEG_INF), row(1.0)))
    m, alpha = lax.fori_loop(0, qi & 1, lambda _, c: stage(qi - 1, c), carry)

    v_prev = vs_ref[0, jnp.maximum(qi - 1, 0)]
    pv = [_dot(v_prev, p_buf[r]) for r in heads]
    causal = jnp.where(rel >= 0, 0.0, NEG_INF)
    outs = []
    for r in heads:
        _, a_r, p = _softmax_tile(m[r], s_buf[r] + causal)
        acc = a_r * (alpha[r] * acc_buf[r] + pv[r]) + _dot(vs_ref[0, qi], p)
        outs.append(o_buf[r] + gate_ref[1, 0, r:r + 1, :] * _normalize(acc))
    o_ref[...] = jnp.concatenate(outs, axis=0).T.astype(o_ref.dtype)


def _nsa(qn_t, mb, oc_t, gates, ka, vs, kw, vw, b, g_kv, s_len):
    tq = NSA_TILE
    nq = s_len // tq
    rep = NSA_REP
    n_sel = mb.shape[1]
    kv = lambda arr: pl.BlockSpec((1,) + arr.shape[1:], lambda bi, gi, qi: (bi * g_kv + gi, 0, 0, 0))
    return pl.pallas_call(
        _nsa_kernel,
        grid=(b, g_kv, nq),
        in_specs=[
            pl.BlockSpec((rep * HEAD_DIM, tq), lambda bi, gi, qi: (gi, bi * nq + qi)),
            pl.BlockSpec((1, n_sel, tq), lambda bi, gi, qi: (bi * g_kv + gi, 0, qi)),
            pl.BlockSpec((rep * HEAD_DIM, tq), lambda bi, gi, qi: (gi, bi * nq + qi)),
            pl.BlockSpec((3, 1, rep, tq), lambda bi, gi, qi: (0, gi, 0, bi * nq + qi)),
            kv(ka), kv(vs), kv(kw), kv(vw),
        ],
        out_specs=pl.BlockSpec((tq, rep * HEAD_DIM), lambda bi, gi, qi: (bi * nq + qi, gi)),
        out_shape=jax.ShapeDtypeStruct((b * s_len, NSA_Q_W), BF16),
        scratch_shapes=[pltpu.VMEM((rep, HEAD_DIM + n_sel, tq), BF16), pltpu.VMEM((rep, tq, tq), F32),
                        pltpu.VMEM((rep, tq, tq), BF16), pltpu.VMEM((rep, VAL_ROWS, tq), F32),
                        pltpu.VMEM((rep, HEAD_DIM, tq), F32)],
        compiler_params=_params(3),
        name="nsa",
    )(qn_t, mb, oc_t, gates, ka, vs, kw, vw)


def _swa_kernel(q_ref, sink_ref, k_ref, v_ref, o_ref):
    tq = q_ref.shape[1]
    tk = k_ref.shape[2]
    heads = SWA_HEADS
    qi = pl.program_id(1)

    q_heads = [q_ref[r * HEAD_DIM:(r + 1) * HEAD_DIM, :] for r in range(heads)]
    rel = _rel_pos(tk, tq)
    n_back = SWA_WINDOW // tk
    state = _flash_init(heads, tq)
    for back in range(n_back, -1, -1):
        kj = qi - back
        kjc = jnp.maximum(kj, 0)
        state = _flash_heads(state, k_ref[0, kjc], q_heads, v_ref[0, kjc],
                             _window_bias(rel, back, tk, kj, SWA_WINDOW))
    outs = []
    for r in range(heads):
        m, acc = state[r]
        sink = sink_ref[:, r * tq:(r + 1) * tq]
        m_all = jnp.maximum(m, sink)
        scale = jnp.exp2(m - m_all)
        outs.append(acc[:HEAD_DIM] * scale / (acc[HEAD_DIM:HEAD_DIM + 1] * scale + jnp.exp2(sink - m_all)))
    o_ref[...] = jnp.concatenate(outs, axis=0).T.astype(o_ref.dtype)


def _swa(sq_t, sink_row, k, v, b, s_len):
    tq = SWA_TILE
    nq = s_len // tq
    kv = lambda arr: pl.BlockSpec((1,) + arr.shape[1:], lambda bi, qi: (bi, 0, 0, 0))
    return pl.pallas_call(
        _swa_kernel,
        grid=(b, nq),
        in_specs=[pl.BlockSpec((SWA_Q_W, tq), lambda bi, qi: (0, bi * nq + qi)),
                  _resident(sink_row.shape), kv(k), kv(v)],
        out_specs=pl.BlockSpec((tq, SWA_Q_W), lambda bi, qi: (bi * nq + qi, 0)),
        out_shape=jax.ShapeDtypeStruct((b * s_len, SWA_Q_W), BF16),
        compiler_params=_params(2),
        name="swa",
    )(sq_t, sink_row, k, v)


def _merge_kernel(h_ref, oa_ref, ob_ref, gab_ref, wa_ref, wb_ref, wo_ref, o_ref):
    d = h_ref.shape[1]
    gab = gab_ref[...]
    merged = gab[:, :d] * _dot(oa_ref[...], wa_ref[...]) + gab[:, d:] * _dot(ob_ref[...], wb_ref[...])
    o_ref[...] = h_ref[...] + _dot(merged.astype(BF16), wo_ref[...])


def _merge(h2, oa, ob, gab, wa, wb, wo):
    n, d = h2.shape
    tm = ROW_TILE
    rows = lambda w: pl.BlockSpec((tm, w), lambda i: (i, 0))
    return pl.pallas_call(
        _merge_kernel,
        grid=(n // tm,),
        in_specs=[rows(d), rows(NSA_Q_W), rows(SWA_Q_W), rows(2 * d),
                  _resident(wa.shape), _resident(wb.shape), _resident(wo.shape)],
        out_specs=rows(d),
        out_shape=jax.ShapeDtypeStruct((n, d), F32),
        compiler_params=_params(1),
        name="merge",
    )(h2, oa, ob, gab, wa, wb, wo)


def _swap_halves_cols(w):
    k, c = w.shape
    return w.reshape(k, c // HEAD_DIM, 2, HALF_DIM)[:, :, ::-1, :].reshape(k, c)


def _overlap_matrix(n_sel, n_cmp, n_cmp_pad):
    cs = np.arange(n_cmp) * CMP_STRIDE
    ss = np.arange(n_sel) * SEL_LEN
    ov = np.clip(np.minimum(cs[None, :] + CMP_LEN, ss[:, None] + SEL_LEN)
                 - np.maximum(cs[None, :], ss[:, None]), 0, None).astype(np.float32) / CMP_LEN
    return np.pad(ov, ((0, 0), (0, n_cmp_pad - n_cmp)))


def _value_tiles(x_t, b, groups, s_len, tk):
    x = x_t.reshape(groups, HEAD_DIM, b, s_len // tk, tk).transpose(2, 0, 3, 1, 4)
    ones = jnp.ones(x.shape[:3] + (1, tk), x.dtype)
    zeros = jnp.zeros(x.shape[:3] + (VAL_ROWS - HEAD_DIM - 1, tk), x.dtype)
    return jnp.concatenate([x, ones, zeros], axis=3).reshape(b * groups, s_len // tk, VAL_ROWS, tk)


def _rows_by_group(x, b, groups, s_len):
    return x.reshape(b, s_len, groups, HEAD_DIM).transpose(0, 2, 1, 3)


def _layer(h, positions, w, b, s_len):
    n, d = h.shape
    g_kv = NSA_KV
    n_sel = s_len // SEL_LEN
    n_top = min(SEL_TOPN, n_sel)
    nc = s_len // CMP_STRIDE
    bf = lambda a: a.astype(BF16)

    h = _ffn(h, w['norm_ffn1'][None], bf(w['ffn1_gate']), bf(w['ffn1_up']), bf(w['ffn1_down']),
             w['norm_ffn1'][None], False)

    pts = np.cumsum((NSA_Q_W,) + (NSA_KV_W,) * 6 + (NSA_GATE_W, SWA_Q_W, SWA_KV_W, SWA_KV_W, d, d))[:-1]
    (w_nq, w_kc, w_vc, w_ksl, w_vsl, w_kwn, w_vwn, w_ng, w_sq, w_sk, w_sv, w_ga, w_gb) = jnp.split(
        w['w_in'], pts, axis=1)
    w_t = jnp.concatenate([w_nq, w_sq, w_vsl, w_vwn, w_sv, w_ng,
                           jnp.zeros((d, 32 - NSA_GATE_W), F32)], axis=1).T
    w_k = jnp.concatenate([w_ksl, w_kwn, w_sk, jnp.zeros((d, 64), F32)], axis=1)
    w_n = jnp.concatenate([w_kc, w_vc, w_k, _swap_halves_cols(w_k), w_ga, w_gb], axis=1)

    inv_freq = ROPE_THETA ** (-jnp.arange(HALF_DIM, dtype=F32) / HALF_DIM)
    ang = positions.astype(F32)[..., None] * inv_freq
    cos, sin = jnp.cos(ang), jnp.sin(ang)
    cos_n = jnp.concatenate([cos, cos], axis=-1)
    sin_n = jnp.concatenate([-sin, sin], axis=-1)
    tile2 = lambda a: jnp.concatenate([a, a], axis=-1).reshape(n, 2 * HEAD_DIM)

    (qn_t, sq_t, vsl_t, vwn_t, sv_t, ng_t, kcv, kr, gab) = _proj(
        h, w['norm_mix'][None], bf(w_t), bf(w_n),
        cos.reshape(n, HALF_DIM).T, sin.reshape(n, HALF_DIM).T, tile2(cos_n), tile2(sin_n))

    def chunk_rows(x):
        return (x.reshape(b, nc, CMP_STRIDE, g_kv, HEAD_DIM).transpose(0, 3, 1, 2, 4)
                .reshape(b * g_kv, nc, CMP_STRIDE * HEAD_DIM))
    cpos = slice(CMP_LEN - 1, None, CMP_STRIDE)
    pad_c = lambda a: jnp.repeat(jnp.pad(a[:, cpos], ((0, 0), (0, 1), (0, 0))), g_kv, axis=0)
    w1 = lambda a: bf(a.reshape(2, CMP_STRIDE * HEAD_DIM, a.shape[-1]))
    kc, vc_t = _compress(
        chunk_rows(kcv[:, :NSA_KV_W]), chunk_rows(kcv[:, NSA_KV_W:]),
        w['cmp_pe_k'].reshape(2, CMP_STRIDE * HEAD_DIM), w['cmp_pe_v'].reshape(2, CMP_STRIDE * HEAD_DIM),
        w1(w['cmp_k_w1']), w1(w['cmp_v_w1']), bf(w['cmp_k_w2']), bf(_swap_halves_cols(w['cmp_k_w2'])),
        bf(w['cmp_v_w2'].T), pad_c(cos_n), pad_c(sin_n))

    gates = ng_t[:NSA_GATE_W].reshape(3, g_kv, NSA_REP, n)
    ov = jnp.asarray(_overlap_matrix(n_sel, nc - 1, nc), BF16)
    mb, oc_t = _cmp_topk(qn_t, kc, vc_t, ov, gates, b, g_kv, s_len, n_top)

    tk = NSA_TILE
    expand = jnp.asarray(np.arange(s_len)[:, None] // SEL_LEN == np.arange(n_sel)[None, :], BF16)
    k_sel = _rows_by_group(kr[:, 0:128], b, g_kv, s_len)
    k_aug = jnp.concatenate([k_sel, jnp.broadcast_to(expand, (b, g_kv, s_len, n_sel))], axis=-1)
    k_aug = k_aug.reshape(b * g_kv, s_len // tk, tk, HEAD_DIM + n_sel)
    k_win = _rows_by_group(kr[:, 128:256], b, g_kv, s_len).reshape(b * g_kv, s_len // tk, tk, HEAD_DIM)
    o_a = _nsa(qn_t, mb, oc_t, gates, k_aug, _value_tiles(vsl_t, b, g_kv, s_len, tk), k_win,
               _value_tiles(vwn_t, b, g_kv, s_len, tk), b, g_kv, s_len)

    ts = SWA_TILE
    k_swa = kr[:, 256:320].reshape(b, s_len // ts, ts, HEAD_DIM)
    sink_row = jnp.repeat(w['swa_sinks'].astype(F32) * LOG2E, ts)[None]
    o_b = _swa(sq_t, sink_row, k_swa, _value_tiles(sv_t, b, 1, s_len, ts), b, s_len)

    return _merge(h, o_a, o_b, gab, bf(w['w_branch_a']), bf(w['w_branch_b']), bf(w['w_out']))


def kernel(x, positions, norm_ffn1, ffn1_gate, ffn1_up, ffn1_down, norm_mix, w_in, cmp_pe_k, cmp_k_w1, cmp_k_w2, cmp_pe_v, cmp_v_w1, cmp_v_w2, swa_sinks, w_branch_a, w_branch_b, w_out, norm_ffn2, ffn2_gate, ffn2_up, ffn2_down, norm_final):
    b, s_len, d = x.shape
    stacked = dict(norm_ffn1=norm_ffn1, ffn1_gate=ffn1_gate, ffn1_up=ffn1_up, ffn1_down=ffn1_down,
                   norm_mix=norm_mix, w_in=w_in, cmp_pe_k=cmp_pe_k, cmp_k_w1=cmp_k_w1, cmp_k_w2=cmp_k_w2,
                   cmp_pe_v=cmp_pe_v, cmp_v_w1=cmp_v_w1, cmp_v_w2=cmp_v_w2, swa_sinks=swa_sinks,
                   w_branch_a=w_branch_a, w_branch_b=w_branch_b, w_out=w_out)
    depth = norm_ffn1.shape[0]
    h = x.reshape(b * s_len, d)
    for i in range(depth):
        w = {k: v[i] for k, v in stacked.items()}
        h = _layer(h, positions, w, b, s_len)
        last = i == depth - 1
        h = _ffn(h, norm_ffn2[i][None], ffn2_gate[i].astype(BF16), ffn2_up[i].astype(BF16),
                 ffn2_down[i].astype(BF16), norm_final[None], last)
    return h.reshape(b, s_len, d)
```

```python
import functools

import numpy as np
import jax
import jax.numpy as jnp
from jax import lax
from jax.experimental import pallas as pl
from jax.experimental.pallas import tpu as pltpu

HEAD_DIM = 64
HALF_DIM = HEAD_DIM // 2
NSA_HEADS = 8
NSA_KV = 2
NSA_REP = NSA_HEADS // NSA_KV
SWA_HEADS = 8
CMP_STRIDE = 16
CMP_LEN = 2 * CMP_STRIDE
SEL_LEN = 64
SEL_SHIFT = 6
SEL_TOPN = 16
SEL_LOCAL = 2
NSA_WINDOW = 512
SWA_WINDOW = 128
ROPE_THETA = 10000.0
RMS_EPS = 1e-6
FFN_HALF = 0.5
NEG_INF = -1e30
FORCE = 1e9
LOG2E = 1.4426950408889634
Q_SCALE = HEAD_DIM ** -0.5 * LOG2E
BF16_ROWS = 16
VAL_ROWS = HEAD_DIM + BF16_ROWS

NSA_Q_W = NSA_HEADS * HEAD_DIM
NSA_KV_W = NSA_KV * HEAD_DIM
NSA_GATE_W = 3 * NSA_HEADS
SWA_Q_W = SWA_HEADS * HEAD_DIM
SWA_KV_W = HEAD_DIM

VMEM_LIMIT_BYTES = 56 * 1024 * 1024

BF16 = jnp.bfloat16
F32 = jnp.float32

ROW_TILE = 512
CMP_TILE = 256
NSA_TILE = 256
SWA_TILE = 128


def _params(n_axes):
    return pltpu.CompilerParams(dimension_semantics=("arbitrary",) * n_axes,
                                vmem_limit_bytes=VMEM_LIMIT_BYTES)


def _resident(shape):
    zeros = (0,) * len(shape)
    return pl.BlockSpec(shape, lambda *_: zeros, pipeline_mode=pl.Buffered(1))


def _rms(x, g):
    y = x * lax.rsqrt(jnp.mean(x * x, axis=-1, keepdims=True) + RMS_EPS)
    return y * g


def _dot(a, b):
    return jnp.dot(a, b, preferred_element_type=F32)


def _dot_nt(a, b):
    return lax.dot_general(a, b, (((1,), (1,)), ((), ())), preferred_element_type=F32)


def _ffn_kernel(x_ref, g_ref, wg_ref, wu_ref, wd_ref, gf_ref, o_ref, *, final_norm):
    x = x_ref[...]
    xb = _rms(x, g_ref[...]).astype(BF16)
    a = _dot(xb, wg_ref[...])
    b = _dot(xb, wu_ref[...])
    t = (a * jax.nn.sigmoid(a)) * b
    h = x + FFN_HALF * _dot(t.astype(BF16), wd_ref[...])
    if final_norm:
        h = _rms(h, gf_ref[...])
    o_ref[...] = h


def _ffn(x2, g, wg, wu, wd, gf, final_norm):
    n, d = x2.shape
    f = wg.shape[1]
    tm = ROW_TILE
    row = pl.BlockSpec((tm, d), lambda i: (i, 0))
    return pl.pallas_call(
        functools.partial(_ffn_kernel, final_norm=final_norm),
        grid=(n // tm,),
        in_specs=[row, _resident((1, d)), _resident((d, f)), _resident((d, f)), _resident((f, d)),
                  _resident((1, d))],
        out_specs=row,
        out_shape=jax.ShapeDtypeStruct((n, d), F32),
        compiler_params=_params(1),
        name="ffn_final" if final_norm else "ffn",
    )(x2, g, wg, wu, wd, gf)


def _proj_kernel(h_ref, g_ref, wt_ref, wn_ref, cos_t_ref, sin_t_ref, cos_n_ref, sin_n_ref,
                 qn_ref, sq_ref, vsl_ref, vwn_ref, sv_ref, ng_ref, kcv_ref, kr_ref, gab_ref):
    ub = _rms(h_ref[...], g_ref[...]).astype(BF16)
    yt = _dot_nt(wt_ref[...], ub)
    cos_t = cos_t_ref[...]
    sin_t = sin_t_ref[...]

    def rope_t(block, out_ref):
        for hd in range(block.shape[0] // HEAD_DIM):
            x1 = block[hd * HEAD_DIM:hd * HEAD_DIM + HALF_DIM]
            x2 = block[hd * HEAD_DIM + HALF_DIM:(hd + 1) * HEAD_DIM]
            out_ref[hd * HEAD_DIM:hd * HEAD_DIM + HALF_DIM, :] = (
                (x1 * cos_t - x2 * sin_t) * Q_SCALE).astype(out_ref.dtype)
            out_ref[hd * HEAD_DIM + HALF_DIM:(hd + 1) * HEAD_DIM, :] = (
                (x2 * cos_t + x1 * sin_t) * Q_SCALE).astype(out_ref.dtype)

    o = 0
    rope_t(yt[o:o + NSA_Q_W], qn_ref); o += NSA_Q_W
    rope_t(yt[o:o + SWA_Q_W], sq_ref); o += SWA_Q_W
    vsl_ref[...] = yt[o:o + NSA_KV_W].astype(BF16); o += NSA_KV_W
    vwn_ref[...] = yt[o:o + NSA_KV_W].astype(BF16); o += NSA_KV_W
    sv_ref[...] = yt[o:o + SWA_KV_W].astype(BF16); o += SWA_KV_W
    ng_ref[...] = jax.nn.sigmoid(yt[o:o + 32])

    yn = _dot(ub, wn_ref[...])
    kcv_ref[...] = yn[:, 0:256]
    cos_n = jnp.concatenate([cos_n_ref[...]] * 3, axis=1)
    sin_n = jnp.concatenate([sin_n_ref[...]] * 3, axis=1)
    kr_ref[...] = (yn[:, 256:640] * cos_n + yn[:, 640:1024] * sin_n).astype(BF16)
    gab_ref[...] = jax.nn.sigmoid(yn[:, 1024:])


def _proj(h2, g, wt, wn, cos_t, sin_t, cos_n, sin_n):
    n, d = h2.shape
    tm = ROW_TILE
    rows = lambda w: pl.BlockSpec((tm, w), lambda i: (i, 0))
    cols = lambda w: pl.BlockSpec((w, tm), lambda i: (0, i))
    out_shape = [
        jax.ShapeDtypeStruct((NSA_Q_W, n), BF16), jax.ShapeDtypeStruct((SWA_Q_W, n), BF16),
        jax.ShapeDtypeStruct((NSA_KV_W, n), BF16), jax.ShapeDtypeStruct((NSA_KV_W, n), BF16),
        jax.ShapeDtypeStruct((SWA_KV_W, n), BF16), jax.ShapeDtypeStruct((32, n), F32),
        jax.ShapeDtypeStruct((n, 256), F32), jax.ShapeDtypeStruct((n, 384), BF16),
        jax.ShapeDtypeStruct((n, 2 * d), F32),
    ]
    return pl.pallas_call(
        _proj_kernel,
        grid=(n // tm,),
        in_specs=[rows(d), _resident((1, d)), _resident(wt.shape), _resident(wn.shape),
                  cols(HALF_DIM), cols(HALF_DIM), rows(128), rows(128)],
        out_specs=[cols(NSA_Q_W), cols(SWA_Q_W), cols(NSA_KV_W), cols(NSA_KV_W), cols(SWA_KV_W),
                   cols(32), rows(256), rows(384), rows(2 * d)],
        out_shape=out_shape,
        compiler_params=_params(1),
        name="proj",
    )(h2, g, wt, wn, cos_t, sin_t, cos_n, sin_n)


def _gelu(x):
    return jax.nn.gelu(x, approximate=True)


def _compress_kernel(ak_ref, av_ref, pek_ref, pev_ref, w1k_ref, w1v_ref, w2k_ref, w2ks_ref, w2vt_ref,
                     cos_ref, sin_ref, kc_ref, vct_ref):
    nc = ak_ref.shape[1]

    def hidden(a_ref, pe_ref, w1_ref):
        a = a_ref[0]
        top = _dot((a + pe_ref[0:1, :]).astype(BF16), w1_ref[0])
        bot = _dot((a + pe_ref[1:2, :]).astype(BF16), w1_ref[1])
        return _gelu(top + pltpu.roll(bot, shift=nc - 1, axis=0)).astype(BF16)

    hk = hidden(ak_ref, pek_ref, w1k_ref)
    kc = _dot(hk, w2k_ref[...]) * cos_ref[0] + _dot(hk, w2ks_ref[...]) * sin_ref[0]
    kc_ref[0] = kc.astype(BF16)
    hv = hidden(av_ref, pev_ref, w1v_ref)
    vct_ref[0] = _dot_nt(w2vt_ref[...], hv).astype(BF16)


def _compress(ak, av, pek, pev, w1k, w1v, w2k, w2ks, w2vt, cos_c, sin_c):
    bg, nc, cw = ak.shape
    hid = w1k.shape[-1]
    per = lambda *s: pl.BlockSpec((1,) + s, lambda i: (i,) + (0,) * len(s))
    return pl.pallas_call(
        _compress_kernel,
        grid=(bg,),
        in_specs=[per(nc, cw), per(nc, cw), _resident((2, cw)), _resident((2, cw)),
                  _resident((2, cw, hid)), _resident((2, cw, hid)), _resident((hid, HEAD_DIM)),
                  _resident((hid, HEAD_DIM)), _resident((HEAD_DIM, hid)),
                  per(nc, HEAD_DIM), per(nc, HEAD_DIM)],
        out_specs=[per(nc, HEAD_DIM), per(HEAD_DIM, nc)],
        out_shape=[jax.ShapeDtypeStruct((bg, nc, HEAD_DIM), BF16),
                   jax.ShapeDtypeStruct((bg, HEAD_DIM, nc), BF16)],
        compiler_params=_params(1),
        name="compress",
    )(ak, av, pek, pev, w1k, w1v, w2k, w2ks, w2vt, cos_c, sin_c)


def _stack_heads(q_block, n_heads):
    return jnp.concatenate([q_block[r * HEAD_DIM:(r + 1) * HEAD_DIM] for r in range(n_heads)], axis=1)


def _cmp_topk_kernel(q_ref, kc_ref, vct_ref, ov_ref, gate_ref, mb_ref, oc_ref, *, n_top):
    tq = q_ref.shape[1]
    nc = kc_ref.shape[1]
    n_sel = ov_ref.shape[0]
    rep = NSA_REP
    q0 = pl.program_id(2) * tq

    qs = _stack_heads(q_ref[...], rep)
    s = _dot(kc_ref[0], qs)
    cblk = lax.broadcasted_iota(jnp.int32, (nc, rep * tq), 0)
    lane = lax.broadcasted_iota(jnp.int32, (nc, rep * tq), 1)
    t_row = q0 + (lane & (tq - 1))
    vis = (cblk * CMP_STRIDE + (CMP_LEN - 1) <= t_row) & (cblk < nc - 1)
    s = jnp.where(vis, s, NEG_INF)
    m = jnp.max(s, axis=0, keepdims=True)
    e = jnp.where(vis, jnp.exp2(s - m), 0.0)
    l = jnp.sum(e, axis=0, keepdims=True)
    p = e / jnp.where(l > 0.0, l, 1.0)

    oc = _dot(vct_ref[0], p.astype(BF16))
    for r in range(rep):
        oc_ref[r * HEAD_DIM:(r + 1) * HEAD_DIM, :] = oc[:, r * tq:(r + 1) * tq] * gate_ref[0, 0, r:r + 1, :]

    psum = p[:, 0:tq]
    for r in range(1, rep):
        psum = psum + p[:, r * tq:(r + 1) * tq]
    p_hi = psum.astype(BF16)
    p_lo = (psum - p_hi.astype(F32)).astype(BF16)
    imp = _dot(ov_ref[...], p_hi) + _dot(ov_ref[...], p_lo)

    blk = lax.broadcasted_iota(jnp.int32, (n_sel, tq), 0)
    tb = (q0 + lax.broadcasted_iota(jnp.int32, (n_sel, tq), 1)) >> SEL_SHIFT
    forced = (blk == 0) | ((tb - blk >= 0) & (tb - blk < SEL_LOCAL))
    imp = jnp.where(forced, FORCE, jnp.where(blk > tb, -FORCE, imp))

    def pick(_, carry):
        x, chosen = carry
        top = jnp.max(x, axis=0, keepdims=True)
        first = jnp.min(jnp.where(x == top, blk, n_sel), axis=0, keepdims=True)
        hit = blk == first
        return jnp.where(hit, -jnp.inf, x), jnp.where(hit, 0.0, chosen)

    _, bias = lax.fori_loop(0, n_top, pick, (imp, jnp.full((n_sel, tq), NEG_INF, F32)))
    mb_ref[0] = bias.astype(BF16)


def _cmp_topk(qn_t, kc, vct, ov, gates, b, g_kv, s_len, n_top):
    tq = CMP_TILE
    nq = s_len // tq
    nc = kc.shape[1]
    n_sel = ov.shape[0]
    rep = NSA_REP
    return pl.pallas_call(
        functools.partial(_cmp_topk_kernel, n_top=n_top),
        grid=(b, g_kv, nq),
        in_specs=[
            pl.BlockSpec((rep * HEAD_DIM, tq), lambda bi, gi, qi: (gi, bi * nq + qi)),
            pl.BlockSpec((1, nc, HEAD_DIM), lambda bi, gi, qi: (bi * g_kv + gi, 0, 0)),
            pl.BlockSpec((1, HEAD_DIM, nc), lambda bi, gi, qi: (bi * g_kv + gi, 0, 0)),
            _resident(ov.shape),
            pl.BlockSpec((1, 1, rep, tq), lambda bi, gi, qi: (0, gi, 0, bi * nq + qi)),
        ],
        out_specs=[
            pl.BlockSpec((1, n_sel, tq), lambda bi, gi, qi: (bi * g_kv + gi, 0, qi)),
            pl.BlockSpec((rep * HEAD_DIM, tq), lambda bi, gi, qi: (gi, bi * nq + qi)),
        ],
        out_shape=[jax.ShapeDtypeStruct((b * g_kv, n_sel, s_len), BF16),
                   jax.ShapeDtypeStruct((NSA_Q_W, b * s_len), F32)],
        compiler_params=_params(3),
        name="cmp_topk",
    )(qn_t, kc, vct, ov, gates)


def _softmax_tile(m, s):
    m_new = jnp.maximum(m, jnp.max(s, axis=0, keepdims=True))
    return m_new, jnp.exp2(m - m_new), jnp.exp2(s - m_new).astype(BF16)


def _flash_step(carry, s, v_aug):
    m, acc = carry
    m_new, alpha, p = _softmax_tile(m, s)
    return m_new, alpha * acc + _dot(v_aug, p)


def _flash_init(n_heads, tq):
    return tuple((jnp.full((1, tq), NEG_INF, F32), jnp.zeros((VAL_ROWS, tq), F32)) for _ in range(n_heads))


def _normalize(acc):
    return acc[:HEAD_DIM] / acc[HEAD_DIM:HEAD_DIM + 1]


def _flash_heads(state, k_tile, q_heads, v_t, bias=None):
    scores = [_dot(k_tile, q_h) for q_h in q_heads]
    if bias is not None:
        scores = [s + bias for s in scores]
    return tuple(_flash_step(carry, s, v_t) for carry, s in zip(state, scores))


def _rel_pos(tk, tq):
    return lax.broadcasted_iota(jnp.int32, (tk, tq), 1) - lax.broadcasted_iota(jnp.int32, (tk, tq), 0)


def _window_bias(rel, back, tk, kj, window):
    dist = rel + (back * tk + jnp.where(kj >= 0, 0, window))
    return jnp.where((dist >= 0) & (dist < window), 0.0, NEG_INF)


def _pipe_stage(s_buf, p_buf, heads, cur=None, nxt=None, prv=None):
    if nxt is not None:
        s_next = [_dot(nxt[0], nxt[1][r][...]) for r in heads]
    if prv is not None:
        pv = [_dot(prv[0], p_buf[r]) for r in heads]
    out = None
    if cur is not None:
        m, bias = cur
        out = []
        for r in heads:
            s = s_buf[r] if bias is None else s_buf[r] + bias
            m_r, a_r, p_buf[r] = _softmax_tile(m[r], s)
            out.append((m_r, a_r))
        out = tuple(zip(*out))
    for r in heads:
        if prv is not None:
            prv[1][r] = prv[2][r] * prv[1][r] + pv[r]
        if nxt is not None:
            s_buf[r] = s_next[r]
    return out


def _nsa_kernel(q_ref, mb_ref, oc_ref, gate_ref, ka_ref, vs_ref, kw_ref, vw_ref, o_ref,
                qa_buf, s_buf, p_buf, sel_acc, win_acc):
    tq = q_ref.shape[1]
    tk = ka_ref.shape[2]
    rep = NSA_REP
    heads = range(rep)
    qi = pl.program_id(2)
    stage = functools.partial(_pipe_stage, s_buf, p_buf, heads)

    mb = mb_ref[0]
    q_heads = [q_ref.at[pl.ds(r * HEAD_DIM, HEAD_DIM), :] for r in heads]
    q_aug = [qa_buf.at[r] for r in heads]
    for r in heads:
        qa_buf[r] = jnp.concatenate([q_heads[r][...], mb], axis=0)
        s_buf[r] = _dot(ka_ref[0, 0], qa_buf[r])
        p_buf[r] = jnp.zeros((tk, tq), BF16)
        sel_acc[r] = jnp.zeros((VAL_ROWS, tq), F32)
        win_acc[r] = jnp.zeros((VAL_ROWS, tq), F32)
    row = lambda v: tuple(jnp.full((1, tq), v, F32) for _ in heads)

    def sel_stage(i, carry):
        m, alpha = carry
        return stage(cur=(m, None), nxt=(ka_ref[0, i + 1], q_aug),
                     prv=(vs_ref[0, jnp.maximum(i - 1, 0)], sel_acc, alpha))

    carry = lax.fori_loop(0, qi >> 1, lambda j, c: sel_stage(2 * j + 1, sel_stage(2 * j, c)),
                          (row(NEG_INF), row(1.0)))
    m, alpha = lax.fori_loop(0, qi & 1, lambda _, c: sel_stage(qi - 1, c), carry)

    rel = _rel_pos(tk, tq)
    backs = list(range(NSA_WINDOW // tk, -1, -1))
    win_tile = [jnp.maximum(qi - back, 0) for back in backs]
    win_bias = [_window_bias(rel, back, tk, qi - back, NSA_WINDOW) for back in backs]
    _, alpha_d = stage(cur=(m, jnp.where(rel >= 0, 0.0, NEG_INF)), nxt=(kw_ref[0, win_tile[0]], q_heads),
                       prv=(vs_ref[0, jnp.maximum(qi - 1, 0)], sel_acc, alpha))
    prv = (vs_ref[0, qi], sel_acc, alpha_d)
    m_w = row(NEG_INF)
    for n in range(len(backs)):
        nxt = (kw_ref[0, win_tile[n + 1]], q_heads) if n + 1 < len(backs) else None
        m_w, alpha_w = stage(cur=(m_w, win_bias[n]), nxt=nxt, prv=prv)
        prv = (vw_ref[0, win_tile[n]], win_acc, alpha_w)
    stage(prv=prv)

    outs = [oc_ref[r * HEAD_DIM:(r + 1) * HEAD_DIM, :]
            + gate_ref[1, 0, r:r + 1, :] * _normalize(sel_acc[r])
            + gate_ref[2, 0, r:r + 1, :] * _normalize(win_acc[r]) for r in heads]
    o_ref[...] = jnp.concatenate(outs, axis=0).T.astype(o_ref.dtype)


def _nsa(qn_t, mb, oc_t, gates, ka, vs, kw, vw, b, g_kv, s_len):
    tq = NSA_TILE
    nq = s_len // tq
    rep = NSA_REP
    n_sel = mb.shape[1]
    kv = lambda arr: pl.BlockSpec((1,) + arr.shape[1:], lambda bi, gi, qi: (bi * g_kv + gi, 0, 0, 0))
    return pl.pallas_call(
        _nsa_kernel,
        grid=(b, g_kv, nq),
        in_specs=[
            pl.BlockSpec((rep * HEAD_DIM, tq), lambda bi, gi, qi: (gi, bi * nq + qi)),
            pl.BlockSpec((1, n_sel, tq), lambda bi, gi, qi: (bi * g_kv + gi, 0, qi)),
            pl.BlockSpec((rep * HEAD_DIM, tq), lambda bi, gi, qi: (gi, bi * nq + qi)),
            pl.BlockSpec((3, 1, rep, tq), lambda bi, gi, qi: (0, gi, 0, bi * nq + qi)),
            kv(ka), kv(vs), kv(kw), kv(vw),
        ],
        out_specs=pl.BlockSpec((tq, rep * HEAD_DIM), lambda bi, gi, qi: (bi * nq + qi, gi)),
        out_shape=jax.ShapeDtypeStruct((b * s_len, NSA_Q_W), BF16),
        scratch_shapes=[pltpu.VMEM((rep, HEAD_DIM + n_sel, tq), BF16), pltpu.VMEM((rep, tq, tq), F32),
                        pltpu.VMEM((rep, tq, tq), BF16), pltpu.VMEM((rep, VAL_ROWS, tq), F32),
                        pltpu.VMEM((rep, VAL_ROWS, tq), F32)],
        compiler_params=_params(3),
        name="nsa",
    )(qn_t, mb, oc_t, gates, ka, vs, kw, vw)


def _swa_kernel(q_ref, sink_ref, k_ref, v_ref, o_ref):
    tq = q_ref.shape[1]
    tk = k_ref.shape[2]
    heads = SWA_HEADS
    qi = pl.program_id(1)

    q_heads = [q_ref[r * HEAD_DIM:(r + 1) * HEAD_DIM, :] for r in range(heads)]
    rel = _rel_pos(tk, tq)
    n_back = SWA_WINDOW // tk
    state = _flash_init(heads, tq)
    for back in range(n_back, -1, -1):
        kj = qi - back
        kjc = jnp.maximum(kj, 0)
        state = _flash_heads(state, k_ref[0, kjc], q_heads, v_ref[0, kjc],
                             _window_bias(rel, back, tk, kj, SWA_WINDOW))
    outs = []
    for r in range(heads):
        m, acc = state[r]
        sink = sink_ref[:, r * tq:(r + 1) * tq]
        m_all = jnp.maximum(m, sink)
        scale = jnp.exp2(m - m_all)
        outs.append(acc[:HEAD_DIM] * scale / (acc[HEAD_DIM:HEAD_DIM + 1] * scale + jnp.exp2(sink - m_all)))
    o_ref[...] = jnp.concatenate(outs, axis=0).T.astype(o_ref.dtype)


def _swa(sq_t, sink_row, k, v, b, s_len):
    tq = SWA_TILE
    nq = s_len // tq
    kv = lambda arr: pl.BlockSpec((1,) + arr.shape[1:], lambda bi, qi: (bi, 0, 0, 0))
    return pl.pallas_call(
        _swa_kernel,
        grid=(b, nq),
        in_specs=[pl.BlockSpec((SWA_Q_W, tq), lambda bi, qi: (0, bi * nq + qi)),
                  _resident(sink_row.shape), kv(k), kv(v)],
        out_specs=pl.BlockSpec((tq, SWA_Q_W), lambda bi, qi: (bi * nq + qi, 0)),
        out_shape=jax.ShapeDtypeStruct((b * s_len, SWA_Q_W), BF16),
        compiler_params=_params(2),
        name="swa",
    )(sq_t, sink_row, k, v)


def _merge_kernel(h_ref, oa_ref, ob_ref, gab_ref, wa_ref, wb_ref, wo_ref, o_ref):
    d = h_ref.shape[1]
    gab = gab_ref[...]
    merged = gab[:, :d] * _dot(oa_ref[...], wa_ref[...]) + gab[:, d:] * _dot(ob_ref[...], wb_ref[...])
    o_ref[...] = h_ref[...] + _dot(merged.astype(BF16), wo_ref[...])


def _merge(h2, oa, ob, gab, wa, wb, wo):
    n, d = h2.shape
    tm = ROW_TILE
    rows = lambda w: pl.BlockSpec((tm, w), lambda i: (i, 0))
    return pl.pallas_call(
        _merge_kernel,
        grid=(n // tm,),
        in_specs=[rows(d), rows(NSA_Q_W), rows(SWA_Q_W), rows(2 * d),
                  _resident(wa.shape), _resident(wb.shape), _resident(wo.shape)],
        out_specs=rows(d),
        out_shape=jax.ShapeDtypeStruct((n, d), F32),
        compiler_params=_params(1),
        name="merge",
    )(h2, oa, ob, gab, wa, wb, wo)


def _swap_halves_cols(w):
    k, c = w.shape
    return w.reshape(k, c // HEAD_DIM, 2, HALF_DIM)[:, :, ::-1, :].reshape(k, c)


def _overlap_matrix(n_sel, n_cmp, n_cmp_pad):
    cs = np.arange(n_cmp) * CMP_STRIDE
    ss = np.arange(n_sel) * SEL_LEN
    ov = np.clip(np.minimum(cs[None, :] + CMP_LEN, ss[:, None] + SEL_LEN)
                 - np.maximum(cs[None, :], ss[:, None]), 0, None).astype(np.float32) / CMP_LEN
    return np.pad(ov, ((0, 0), (0, n_cmp_pad - n_cmp)))


def _value_tiles(x_t, b, groups, s_len, tk):
    x = x_t.reshape(groups, HEAD_DIM, b, s_len // tk, tk).transpose(2, 0, 3, 1, 4)
    ones = jnp.ones(x.shape[:3] + (1, tk), x.dtype)
    zeros = jnp.zeros(x.shape[:3] + (VAL_ROWS - HEAD_DIM - 1, tk), x.dtype)
    return jnp.concatenate([x, ones, zeros], axis=3).reshape(b * groups, s_len // tk, VAL_ROWS, tk)


def _rows_by_group(x, b, groups, s_len):
    return x.reshape(b, s_len, groups, HEAD_DIM).transpose(0, 2, 1, 3)


def _layer(h, positions, w, b, s_len):
    n, d = h.shape
    g_kv = NSA_KV
    n_sel = s_len // SEL_LEN
    n_top = min(SEL_TOPN, n_sel)
    nc = s_len // CMP_STRIDE
    bf = lambda a: a.astype(BF16)

    h = _ffn(h, w['norm_ffn1'][None], bf(w['ffn1_gate']), bf(w['ffn1_up']), bf(w['ffn1_down']),
             w['norm_ffn1'][None], False)

    pts = np.cumsum((NSA_Q_W,) + (NSA_KV_W,) * 6 + (NSA_GATE_W, SWA_Q_W, SWA_KV_W, SWA_KV_W, d, d))[:-1]
    (w_nq, w_kc, w_vc, w_ksl, w_vsl, w_kwn, w_vwn, w_ng, w_sq, w_sk, w_sv, w_ga, w_gb) = jnp.split(
        w['w_in'], pts, axis=1)
    w_t = jnp.concatenate([w_nq, w_sq, w_vsl, w_vwn, w_sv, w_ng,
                           jnp.zeros((d, 32 - NSA_GATE_W), F32)], axis=1).T
    w_k = jnp.concatenate([w_ksl, w_kwn, w_sk, jnp.zeros((d, 64), F32)], axis=1)
    w_n = jnp.concatenate([w_kc, w_vc, w_k, _swap_halves_cols(w_k), w_ga, w_gb], axis=1)

    inv_freq = ROPE_THETA ** (-jnp.arange(HALF_DIM, dtype=F32) / HALF_DIM)
    ang = positions.astype(F32)[..., None] * inv_freq
    cos, sin = jnp.cos(ang), jnp.sin(ang)
    cos_n = jnp.concatenate([cos, cos], axis=-1)
    sin_n = jnp.concatenate([-sin, sin], axis=-1)
    tile2 = lambda a: jnp.concatenate([a, a], axis=-1).reshape(n, 2 * HEAD_DIM)

    (qn_t, sq_t, vsl_t, vwn_t, sv_t, ng_t, kcv, kr, gab) = _proj(
        h, w['norm_mix'][None], bf(w_t), bf(w_n),
        cos.reshape(n, HALF_DIM).T, sin.reshape(n, HALF_DIM).T, tile2(cos_n), tile2(sin_n))

    def chunk_rows(x):
        return (x.reshape(b, nc, CMP_STRIDE, g_kv, HEAD_DIM).transpose(0, 3, 1, 2, 4)
                .reshape(b * g_kv, nc, CMP_STRIDE * HEAD_DIM))
    cpos = slice(CMP_LEN - 1, None, CMP_STRIDE)
    pad_c = lambda a: jnp.repeat(jnp.pad(a[:, cpos], ((0, 0), (0, 1), (0, 0))), g_kv, axis=0)
    w1 = lambda a: bf(a.reshape(2, CMP_STRIDE * HEAD_DIM, a.shape[-1]))
    kc, vc_t = _compress(
        chunk_rows(kcv[:, :NSA_KV_W]), chunk_rows(kcv[:, NSA_KV_W:]),
        w['cmp_pe_k'].reshape(2, CMP_STRIDE * HEAD_DIM), w['cmp_pe_v'].reshape(2, CMP_STRIDE * HEAD_DIM),
        w1(w['cmp_k_w1']), w1(w['cmp_v_w1']), bf(w['cmp_k_w2']), bf(_swap_halves_cols(w['cmp_k_w2'])),
        bf(w['cmp_v_w2'].T), pad_c(cos_n), pad_c(sin_n))

    gates = ng_t[:NSA_GATE_W].reshape(3, g_kv, NSA_REP, n)
    ov = jnp.asarray(_overlap_matrix(n_sel, nc - 1, nc), BF16)
    mb, oc_t = _cmp_topk(qn_t, kc, vc_t, ov, gates, b, g_kv, s_len, n_top)

    tk = NSA_TILE
    expand = jnp.asarray(np.arange(s_len)[:, None] // SEL_LEN == np.arange(n_sel)[None, :], BF16)
    k_sel = _rows_by_group(kr[:, 0:128], b, g_kv, s_len)
    k_aug = jnp.concatenate([k_sel, jnp.broadcast_to(expand, (b, g_kv, s_len, n_sel))], axis=-1)
    k_aug = k_aug.reshape(b * g_kv, s_len // tk, tk, HEAD_DIM + n_sel)
    k_win = _rows_by_group(kr[:, 128:256], b, g_kv, s_len).reshape(b * g_kv, s_len // tk, tk, HEAD_DIM)
    o_a = _nsa(qn_t, mb, oc_t, gates, k_aug, _value_tiles(vsl_t, b, g_kv, s_len, tk), k_win,
               _value_tiles(vwn_t, b, g_kv, s_len, tk), b, g_kv, s_len)

    ts = SWA_TILE
    k_swa = kr[:, 256:320].reshape(b, s_len // ts, ts, HEAD_DIM)
    sink_row = jnp.repeat(w['swa_sinks'].astype(F32) * LOG2E, ts)[None]
    o_b = _swa(sq_t, sink_row, k_swa, _value_tiles(sv_t, b, 1, s_len, ts), b, s_len)

    return _merge(h, o_a, o_b, gab, bf(w['w_branch_a']), bf(w['w_branch_b']), bf(w['w_out']))


def kernel(x, positions, norm_ffn1, ffn1_gate, ffn1_up, ffn1_down, norm_mix, w_in, cmp_pe_k, cmp_k_w1, cmp_k_w2, cmp_pe_v, cmp_v_w1, cmp_v_w2, swa_sinks, w_branch_a, w_branch_b, w_out, norm_ffn2, ffn2_gate, ffn2_up, ffn2_down, norm_final):
    b, s_len, d = x.shape
    stacked = dict(norm_ffn1=norm_ffn1, ffn1_gate=ffn1_gate, ffn1_up=ffn1_up, ffn1_down=ffn1_down,
                   norm_mix=norm_mix, w_in=w_in, cmp_pe_k=cmp_pe_k, cmp_k_w1=cmp_k_w1, cmp_k_w2=cmp_k_w2,
                   cmp_pe_v=cmp_pe_v, cmp_v_w1=cmp_v_w1, cmp_v_w2=cmp_v_w2, swa_sinks=swa_sinks,
                   w_branch_a=w_branch_a, w_branch_b=w_branch_b, w_out=w_out)
    depth = norm_ffn1.shape[0]
    h = x.reshape(b * s_len, d)
    for i in range(depth):
        w = {k: v[i] for k, v in stacked.items()}
        h = _layer(h, positions, w, b, s_len)
        last = i == depth - 1
        h = _ffn(h, norm_ffn2[i][None], ffn2_gate[i].astype(BF16), ffn2_up[i].astype(BF16),
                 ffn2_down[i].astype(BF16), norm_final[None], last)
    return h.reshape(b, s_len, d)
```

```python
import functools

import numpy as np
import jax
import jax.numpy as jnp
from jax import lax
from jax.experimental import pallas as pl
from jax.experimental.pallas import tpu as pltpu

HEAD_DIM = 64
HALF_DIM = HEAD_DIM // 2
NSA_HEADS = 8
NSA_KV = 2
NSA_REP = NSA_HEADS // NSA_KV
SWA_HEADS = 8
CMP_STRIDE = 16
CMP_LEN = 2 * CMP_STRIDE
SEL_LEN = 64
SEL_SHIFT = 6
SEL_TOPN = 16
SEL_LOCAL = 2
NSA_WINDOW = 512
SWA_WINDOW = 128
ROPE_THETA = 10000.0
RMS_EPS = 1e-6
FFN_HALF = 0.5
NEG_INF = -1e30
FORCE = 1e9
LOG2E = 1.4426950408889634
Q_SCALE = HEAD_DIM ** -0.5 * LOG2E
BF16_ROWS = 16
VAL_ROWS = HEAD_DIM + BF16_ROWS
LANES = 128

NSA_Q_W = NSA_HEADS * HEAD_DIM
NSA_KV_W = NSA_KV * HEAD_DIM
NSA_GATE_W = 3 * NSA_HEADS
SWA_Q_W = SWA_HEADS * HEAD_DIM
SWA_KV_W = HEAD_DIM

VMEM_LIMIT_BYTES = 56 * 1024 * 1024

BF16 = jnp.bfloat16
F32 = jnp.float32

ROW_TILE = 512
CMP_TILE = 256
NSA_TILE = 256
SWA_TILE = 128


def _params(n_axes):
    return pltpu.CompilerParams(dimension_semantics=("arbitrary",) * n_axes,
                                vmem_limit_bytes=VMEM_LIMIT_BYTES)


def _resident(shape):
    zeros = (0,) * len(shape)
    return pl.BlockSpec(shape, lambda *_: zeros, pipeline_mode=pl.Buffered(1))


def _rms(x, g):
    y = x * lax.rsqrt(jnp.mean(x * x, axis=-1, keepdims=True) + RMS_EPS)
    return y * g


def _dot(a, b):
    return jnp.dot(a, b, preferred_element_type=F32)


def _dot_nt(a, b):
    return lax.dot_general(a, b, (((1,), (1,)), ((), ())), preferred_element_type=F32)


def _rope_angles(pos_row, freq_col):
    ang = pos_row.astype(F32) * freq_col
    return jnp.cos(ang), jnp.sin(ang)


def _rope_rows(block, cos_t, sin_t):
    out = []
    for hd in range(block.shape[0] // HEAD_DIM):
        x1 = block[hd * HEAD_DIM:hd * HEAD_DIM + HALF_DIM]
        x2 = block[hd * HEAD_DIM + HALF_DIM:(hd + 1) * HEAD_DIM]
        out += [x1 * cos_t - x2 * sin_t, x2 * cos_t + x1 * sin_t]
    return out


def _ones_rows(tk):
    return jnp.where(lax.broadcasted_iota(jnp.int32, (BF16_ROWS, tk), 0) == 0, 1.0, 0.0).astype(BF16)


def _ffn_kernel(x_ref, g_ref, wg_ref, wu_ref, wd_ref, gf_ref, o_ref, *, final_norm):
    x = x_ref[...]
    xb = _rms(x, g_ref[...]).astype(BF16)
    a = _dot(xb, wg_ref[...])
    b = _dot(xb, wu_ref[...])
    t = (a * jax.nn.sigmoid(a)) * b
    h = x + FFN_HALF * _dot(t.astype(BF16), wd_ref[...])
    if final_norm:
        h = _rms(h, gf_ref[...])
    o_ref[...] = h


def _ffn(x2, g, wg, wu, wd, gf, final_norm):
    n, d = x2.shape
    f = wg.shape[1]
    tm = ROW_TILE
    row = pl.BlockSpec((tm, d), lambda i: (i, 0))
    return pl.pallas_call(
        functools.partial(_ffn_kernel, final_norm=final_norm),
        grid=(n // tm,),
        in_specs=[row, _resident((1, d)), _resident((d, f)), _resident((d, f)), _resident((f, d)),
                  _resident((1, d))],
        out_specs=row,
        out_shape=jax.ShapeDtypeStruct((n, d), F32),
        compiler_params=_params(1),
        name="ffn_final" if final_norm else "ffn",
    )(x2, g, wg, wu, wd, gf)


PROJ_ROWS = (('qn', NSA_Q_W), ('sq', SWA_Q_W), ('ksel', NSA_KV_W), ('kwin', NSA_KV_W), ('kswa', LANES),
             ('vsel', NSA_KV_W), ('vwin', NSA_KV_W), ('vswa', SWA_KV_W), ('ng', 32),
             ('kc', NSA_KV_W), ('vc', NSA_KV_W))


def _proj_kernel(h_ref, g_ref, wt_ref, wn_ref, pos_ref, freq_ref,
                 qn_ref, sq_ref, ksel_ref, kwin_ref, kswa_ref, vsel_ref, vwin_ref, vswa_ref,
                 ng_ref, kcv_ref, gab_ref):
    ub = _rms(h_ref[...], g_ref[...]).astype(BF16)
    yt = _dot_nt(wt_ref[...], ub)
    rows, o = {}, 0
    for name, width in PROJ_ROWS:
        rows[name] = yt[o:o + width]
        o += width
    cos_t, sin_t = _rope_angles(pos_ref[...], freq_ref[...])

    for name, ref in (('qn', qn_ref), ('sq', sq_ref)):
        for i, piece in enumerate(_rope_rows(rows[name], cos_t, sin_t)):
            ref[i * HALF_DIM:(i + 1) * HALF_DIM, :] = (piece * Q_SCALE).astype(BF16)

    for name, ref in (('ksel', ksel_ref), ('kwin', kwin_ref)):
        k_nat = jnp.concatenate(_rope_rows(rows[name], cos_t, sin_t), axis=0).T
        for g in range(NSA_KV):
            ref[g] = k_nat[:, g * HEAD_DIM:(g + 1) * HEAD_DIM].astype(BF16)
    kswa = _rope_rows(rows['kswa'][:HEAD_DIM], cos_t, sin_t) + [rows['kswa'][HEAD_DIM:]]
    kswa_ref[...] = jnp.concatenate(kswa, axis=0).T[:, :HEAD_DIM].astype(BF16)

    for name, ref, groups in (('vsel', vsel_ref, NSA_KV), ('vwin', vwin_ref, NSA_KV), ('vswa', vswa_ref, 1)):
        tk = ref.shape[2]
        for j in range(ref.shape[0]):
            for g in range(groups):
                ref[j, g * VAL_ROWS:g * VAL_ROWS + HEAD_DIM, :] = (
                    rows[name][g * HEAD_DIM:(g + 1) * HEAD_DIM, j * tk:(j + 1) * tk].astype(BF16))
                ref[j, g * VAL_ROWS + HEAD_DIM:(g + 1) * VAL_ROWS, :] = _ones_rows(tk)

    ng_ref[...] = jax.nn.sigmoid(rows['ng'][:NSA_GATE_W])

    for i, name in enumerate(('kc', 'vc')):
        nat = rows[name].T
        for g in range(NSA_KV):
            kcv_ref[i * NSA_KV + g] = nat[:, g * HEAD_DIM:(g + 1) * HEAD_DIM]

    gab_ref[...] = jax.nn.sigmoid(_dot(ub, wn_ref[...]))


def _proj(h2, g, wt, wn, pos_row, freq_col):
    n, d = h2.shape
    tm = ROW_TILE
    rows = lambda w: pl.BlockSpec((tm, w), lambda i: (i, 0))
    cols = lambda w: pl.BlockSpec((w, tm), lambda i: (0, i))
    grouped = lambda k: pl.BlockSpec((k, tm, HEAD_DIM), lambda i: (0, i, 0))
    tiles = lambda groups, tk: pl.BlockSpec((tm // tk, groups * VAL_ROWS, tk), lambda i: (i, 0, 0))
    val_shape = lambda groups, tk: jax.ShapeDtypeStruct((n // tk, groups * VAL_ROWS, tk), BF16)
    out_shape = [
        jax.ShapeDtypeStruct((NSA_Q_W, n), BF16), jax.ShapeDtypeStruct((SWA_Q_W, n), BF16),
        jax.ShapeDtypeStruct((NSA_KV, n, HEAD_DIM), BF16), jax.ShapeDtypeStruct((NSA_KV, n, HEAD_DIM), BF16),
        jax.ShapeDtypeStruct((n, HEAD_DIM), BF16),
        val_shape(NSA_KV, NSA_TILE), val_shape(NSA_KV, NSA_TILE), val_shape(1, SWA_TILE),
        jax.ShapeDtypeStruct((NSA_GATE_W, n), F32), jax.ShapeDtypeStruct((2 * NSA_KV, n, HEAD_DIM), F32),
        jax.ShapeDtypeStruct((n, 2 * d), F32),
    ]
    return pl.pallas_call(
        _proj_kernel,
        grid=(n // tm,),
        in_specs=[rows(d), _resident((1, d)), _resident(wt.shape), _resident(wn.shape),
                  cols(1), _resident(freq_col.shape)],
        out_specs=[cols(NSA_Q_W), cols(SWA_Q_W), grouped(NSA_KV), grouped(NSA_KV), rows(HEAD_DIM),
                   tiles(NSA_KV, NSA_TILE), tiles(NSA_KV, NSA_TILE), tiles(1, SWA_TILE),
                   cols(NSA_GATE_W), grouped(2 * NSA_KV), rows(2 * d)],
        out_shape=out_shape,
        compiler_params=_params(1),
        name="proj",
    )(h2, g, wt, wn, pos_row, freq_col)


def _gelu(x):
    return jax.nn.gelu(x, approximate=True)


def _compress_kernel(kc_ref, vc_ref, pek_ref, pev_ref, w1k_ref, w1v_ref, w2kt_ref, w2vt_ref,
                     pos_ref, freq_ref, kc_out, vct_out):
    nc = kc_out.shape[1]

    def hidden(x_ref, pe_ref, w1_ref):
        top = bot = None
        for j in range(CMP_STRIDE):
            x = x_ref[0, pl.ds(j, nc, stride=CMP_STRIDE), :]
            t = _dot((x + pe_ref[j:j + 1, :]).astype(BF16), w1_ref[j])
            b = _dot((x + pe_ref[CMP_STRIDE + j:CMP_STRIDE + j + 1, :]).astype(BF16), w1_ref[CMP_STRIDE + j])
            top, bot = (t, b) if top is None else (top + t, bot + b)
        return _gelu(top + pltpu.roll(bot, shift=nc - 1, axis=0)).astype(BF16)

    kt = _dot_nt(w2kt_ref[...], hidden(kc_ref, pek_ref, w1k_ref))
    cos_t, sin_t = _rope_angles(pos_ref[0], freq_ref[...])
    kt = jnp.concatenate(_rope_rows(kt, cos_t, sin_t) + [jnp.zeros((LANES - HEAD_DIM, nc), F32)], axis=0)
    kc_out[0] = kt.T[:, :HEAD_DIM].astype(BF16)
    vct_out[0] = _dot_nt(w2vt_ref[...], hidden(vc_ref, pev_ref, w1v_ref)).astype(BF16)


def _compress(kcv, pek, pev, w1k, w1v, w2kt, w2vt, pos_c, freq_col, b, g_kv, s_len):
    nc = s_len // CMP_STRIDE
    hid = w1k.shape[-1]
    kcv = kcv.reshape(2 * g_kv * b, s_len, HEAD_DIM)
    src = lambda kind: pl.BlockSpec((1, s_len, HEAD_DIM),
                                    lambda i: ((kind * g_kv + i % g_kv) * b + i // g_kv, 0, 0))
    per = lambda *s: pl.BlockSpec((1,) + s, lambda i: (i,) + (0,) * len(s))
    return pl.pallas_call(
        _compress_kernel,
        grid=(b * g_kv,),
        in_specs=[src(0), src(1), _resident(pek.shape), _resident(pev.shape),
                  _resident(w1k.shape), _resident(w1v.shape), _resident((HEAD_DIM, hid)),
                  _resident((HEAD_DIM, hid)),
                  pl.BlockSpec((1, 1, nc), lambda i: (i // g_kv, 0, 0)), _resident(freq_col.shape)],
        out_specs=[per(nc, HEAD_DIM), per(HEAD_DIM, nc)],
        out_shape=[jax.ShapeDtypeStruct((b * g_kv, nc, HEAD_DIM), BF16),
                   jax.ShapeDtypeStruct((b * g_kv, HEAD_DIM, nc), BF16)],
        compiler_params=_params(1),
        name="compress",
    )(kcv, kcv, pek, pev, w1k, w1v, w2kt, w2vt, pos_c, freq_col)


def _stack_heads(q_block, n_heads):
    return jnp.concatenate([q_block[r * HEAD_DIM:(r + 1) * HEAD_DIM] for r in range(n_heads)], axis=1)


def _cmp_topk_kernel(q_ref, kc_ref, vct_ref, ov_ref, gate_ref, mb_ref, oc_ref, *, n_top):
    tq = q_ref.shape[1]
    nc = kc_ref.shape[1]
    n_sel = ov_ref.shape[0]
    rep = NSA_REP
    q0 = pl.program_id(2) * tq

    qs = _stack_heads(q_ref[...], rep)
    s = _dot(kc_ref[0], qs)
    cblk = lax.broadcasted_iota(jnp.int32, (nc, rep * tq), 0)
    lane = lax.broadcasted_iota(jnp.int32, (nc, rep * tq), 1)
    t_row = q0 + (lane & (tq - 1))
    vis = (cblk * CMP_STRIDE + (CMP_LEN - 1) <= t_row) & (cblk < nc - 1)
    s = jnp.where(vis, s, NEG_INF)
    m = jnp.max(s, axis=0, keepdims=True)
    e = jnp.where(vis, jnp.exp2(s - m), 0.0)
    l = jnp.sum(e, axis=0, keepdims=True)
    p = e / jnp.where(l > 0.0, l, 1.0)

    oc = _dot(vct_ref[0], p.astype(BF16))
    for r in range(rep):
        oc_ref[r * HEAD_DIM:(r + 1) * HEAD_DIM, :] = oc[:, r * tq:(r + 1) * tq] * gate_ref[0, 0, r:r + 1, :]

    psum = p[:, 0:tq]
    for r in range(1, rep):
        psum = psum + p[:, r * tq:(r + 1) * tq]
    p_hi = psum.astype(BF16)
    p_lo = (psum - p_hi.astype(F32)).astype(BF16)
    imp = _dot(ov_ref[...], p_hi) + _dot(ov_ref[...], p_lo)

    blk = lax.broadcasted_iota(jnp.int32, (n_sel, tq), 0)
    tb = (q0 + lax.broadcasted_iota(jnp.int32, (n_sel, tq), 1)) >> SEL_SHIFT
    forced = (blk == 0) | ((tb - blk >= 0) & (tb - blk < SEL_LOCAL))
    imp = jnp.where(forced, FORCE, jnp.where(blk > tb, -FORCE, imp))

    def pick(_, carry):
        x, chosen = carry
        top = jnp.max(x, axis=0, keepdims=True)
        first = jnp.min(jnp.where(x == top, blk, n_sel), axis=0, keepdims=True)
        hit = blk == first
        return jnp.where(hit, -jnp.inf, x), jnp.where(hit, 0.0, chosen)

    _, bias = lax.fori_loop(0, n_top, pick, (imp, jnp.full((n_sel, tq), NEG_INF, F32)))
    mb_ref[0] = bias.astype(BF16)


def _cmp_topk(qn_t, kc, vct, ov, gates, b, g_kv, s_len, n_top):
    tq = CMP_TILE
    nq = s_len // tq
    nc = kc.shape[1]
    n_sel = ov.shape[0]
    rep = NSA_REP
    return pl.pallas_call(
        functools.partial(_cmp_topk_kernel, n_top=n_top),
        grid=(b, g_kv, nq),
        in_specs=[
            pl.BlockSpec((rep * HEAD_DIM, tq), lambda bi, gi, qi: (gi, bi * nq + qi)),
            pl.BlockSpec((1, nc, HEAD_DIM), lambda bi, gi, qi: (bi * g_kv + gi, 0, 0)),
            pl.BlockSpec((1, HEAD_DIM, nc), lambda bi, gi, qi: (bi * g_kv + gi, 0, 0)),
            _resident(ov.shape),
            pl.BlockSpec((1, 1, rep, tq), lambda bi, gi, qi: (0, gi, 0, bi * nq + qi)),
        ],
        out_specs=[
            pl.BlockSpec((1, n_sel, tq), lambda bi, gi, qi: (bi * g_kv + gi, 0, qi)),
            pl.BlockSpec((rep * HEAD_DIM, tq), lambda bi, gi, qi: (gi, bi * nq + qi)),
        ],
        out_shape=[jax.ShapeDtypeStruct((b * g_kv, n_sel, s_len), BF16),
                   jax.ShapeDtypeStruct((NSA_Q_W, b * s_len), F32)],
        compiler_params=_params(3),
        name="cmp_topk",
    )(qn_t, kc, vct, ov, gates)


def _softmax_tile(m, s):
    m_new = jnp.maximum(m, jnp.max(s, axis=0, keepdims=True))
    return m_new, jnp.exp2(m - m_new), jnp.exp2(s - m_new).astype(BF16)


def _flash_step(carry, s, v_aug):
    m, acc = carry
    m_new, alpha, p = _softmax_tile(m, s)
    return m_new, alpha * acc + _dot(v_aug, p)


def _flash_init(n_heads, tq):
    return tuple((jnp.full((1, tq), NEG_INF, F32), jnp.zeros((VAL_ROWS, tq), F32)) for _ in range(n_heads))


def _normalize(acc):
    return acc[:HEAD_DIM] / acc[HEAD_DIM:HEAD_DIM + 1]


def _flash_heads(state, k_tile, q_heads, v_t, bias=None):
    scores = [_dot(k_tile, q_h) for q_h in q_heads]
    if bias is not None:
        scores = [s + bias for s in scores]
    return tuple(_flash_step(carry, s, v_t) for carry, s in zip(state, scores))


def _rel_pos(tk, tq):
    return lax.broadcasted_iota(jnp.int32, (tk, tq), 1) - lax.broadcasted_iota(jnp.int32, (tk, tq), 0)


def _window_bias(rel, back, tk, kj, window):
    dist = rel + (back * tk + jnp.where(kj >= 0, 0, window))
    return jnp.where((dist >= 0) & (dist < window), 0.0, NEG_INF)


def _key_tile(k_ref, j, tk):
    return k_ref[0, pl.ds(pl.multiple_of(j * tk, tk), tk), :]


def _pipe_stage(s_buf, p_buf, heads, cur=None, nxt=None, prv=None):
    if nxt is not None:
        s_next = [_dot(nxt[0], nxt[1][r][...]) for r in heads]
    if prv is not None:
        pv = [_dot(prv[0], p_buf[r]) for r in heads]
    out = None
    if cur is not None:
        m, bias = cur
        out = []
        for r in heads:
            s = s_buf[r] if bias is None else s_buf[r] + bias
            m_r, a_r, p_buf[r] = _softmax_tile(m[r], s)
            out.append((m_r, a_r))
        out = tuple(zip(*out))
    for r in heads:
        if prv is not None:
            prv[1][r] = prv[2][r] * prv[1][r] + pv[r]
        if nxt is not None:
            s_buf[r] = s_next[r]
    return out


def _nsa_kernel(q_ref, mb_ref, oc_ref, gate_ref, ks_ref, ex_ref, vs_ref, kw_ref, vw_ref, o_ref,
                qa_buf, s_buf, p_buf, sel_acc, win_acc):
    tq = q_ref.shape[1]
    tk = tq
    n_sel = mb_ref.shape[1]
    rep = NSA_REP
    heads = range(rep)
    qi = pl.program_id(2)
    stage = functools.partial(_pipe_stage, s_buf, p_buf, heads)

    def sel_keys(j):
        at = pl.ds(pl.multiple_of(j * tk, tk), tk)
        return jnp.concatenate([ex_ref[at, :], ks_ref[0, at, :]], axis=1)

    q_heads = [q_ref.at[pl.ds(r * HEAD_DIM, HEAD_DIM), :] for r in heads]
    q_aug = [qa_buf.at[r] for r in heads]
    for r in heads:
        qa_buf[r, 0:n_sel, :] = mb_ref[0]
        qa_buf[r, n_sel:n_sel + HEAD_DIM, :] = q_heads[r][...]
        s_buf[r] = _dot(sel_keys(0), qa_buf[r])
        p_buf[r] = jnp.zeros((tk, tq), BF16)
        sel_acc[r] = jnp.zeros((VAL_ROWS, tq), F32)
        win_acc[r] = jnp.zeros((VAL_ROWS, tq), F32)
    row = lambda v: tuple(jnp.full((1, tq), v, F32) for _ in heads)

    def sel_stage(i, carry):
        m, alpha = carry
        return stage(cur=(m, None), nxt=(sel_keys(i + 1), q_aug),
                     prv=(vs_ref[jnp.maximum(i - 1, 0)], sel_acc, alpha))

    carry = lax.fori_loop(0, qi >> 1, lambda j, c: sel_stage(2 * j + 1, sel_stage(2 * j, c)),
                          (row(NEG_INF), row(1.0)))
    m, alpha = lax.fori_loop(0, qi & 1, lambda _, c: sel_stage(qi - 1, c), carry)

    rel = _rel_pos(tk, tq)
    backs = list(range(NSA_WINDOW // tk, -1, -1))
    win_tile = [jnp.maximum(qi - back, 0) for back in backs]
    win_bias = [_window_bias(rel, back, tk, qi - back, NSA_WINDOW) for back in backs]
    _, alpha_d = stage(cur=(m, jnp.where(rel >= 0, 0.0, NEG_INF)),
                       nxt=(_key_tile(kw_ref, win_tile[0], tk), q_heads),
                       prv=(vs_ref[jnp.maximum(qi - 1, 0)], sel_acc, alpha))
    prv = (vs_ref[qi], sel_acc, alpha_d)
    m_w = row(NEG_INF)
    for n in range(len(backs)):
        nxt = (_key_tile(kw_ref, win_tile[n + 1], tk), q_heads) if n + 1 < len(backs) else None
        m_w, alpha_w = stage(cur=(m_w, win_bias[n]), nxt=nxt, prv=prv)
        prv = (vw_ref[win_tile[n]], win_acc, alpha_w)
    stage(prv=prv)

    outs = [oc_ref[r * HEAD_DIM:(r + 1) * HEAD_DIM, :]
            + gate_ref[1, 0, r:r + 1, :] * _normalize(sel_acc[r])
            + gate_ref[2, 0, r:r + 1, :] * _normalize(win_acc[r]) for r in heads]
    o_ref[...] = jnp.concatenate(outs, axis=0).T.astype(o_ref.dtype)


def _nsa(qn_t, mb, oc_t, gates, ksel, expand, vsel, kwin, vwin, b, g_kv, s_len):
    tq = NSA_TILE
    nq = s_len // tq
    rep = NSA_REP
    n_sel = mb.shape[1]
    keys = pl.BlockSpec((1, s_len, HEAD_DIM), lambda bi, gi, qi: (gi * b + bi, 0, 0))
    vals = pl.BlockSpec((nq, VAL_ROWS, tq), lambda bi, gi, qi: (bi, gi, 0))
    return pl.pallas_call(
        _nsa_kernel,
        grid=(b, g_kv, nq),
        in_specs=[
            pl.BlockSpec((rep * HEAD_DIM, tq), lambda bi, gi, qi: (gi, bi * nq + qi)),
            pl.BlockSpec((1, n_sel, tq), lambda bi, gi, qi: (bi * g_kv + gi, 0, qi)),
            pl.BlockSpec((rep * HEAD_DIM, tq), lambda bi, gi, qi: (gi, bi * nq + qi)),
            pl.BlockSpec((3, 1, rep, tq), lambda bi, gi, qi: (0, gi, 0, bi * nq + qi)),
            keys, _resident(expand.shape), vals, keys, vals,
        ],
        out_specs=pl.BlockSpec((tq, rep * HEAD_DIM), lambda bi, gi, qi: (bi * nq + qi, gi)),
        out_shape=jax.ShapeDtypeStruct((b * s_len, NSA_Q_W), BF16),
        scratch_shapes=[pltpu.VMEM((rep, n_sel + HEAD_DIM, tq), BF16), pltpu.VMEM((rep, tq, tq), F32),
                        pltpu.VMEM((rep, tq, tq), BF16), pltpu.VMEM((rep, VAL_ROWS, tq), F32),
                        pltpu.VMEM((rep, VAL_ROWS, tq), F32)],
        compiler_params=_params(3),
        name="nsa",
    )(qn_t, mb, oc_t, gates, ksel.reshape(g_kv * b, s_len, HEAD_DIM), expand, vsel,
      kwin.reshape(g_kv * b, s_len, HEAD_DIM), vwin)


def _swa_kernel(q_ref, sink_ref, k_ref, v_ref, o_ref):
    tq = q_ref.shape[1]
    tk = tq
    heads = SWA_HEADS
    qi = pl.program_id(1)

    q_heads = [q_ref[r * HEAD_DIM:(r + 1) * HEAD_DIM, :] for r in range(heads)]
    rel = _rel_pos(tk, tq)
    n_back = SWA_WINDOW // tk
    state = _flash_init(heads, tq)
    for back in range(n_back, -1, -1):
        kj = qi - back
        kjc = jnp.maximum(kj, 0)
        state = _flash_heads(state, _key_tile(k_ref, kjc, tk), q_heads, v_ref[kjc],
                             _window_bias(rel, back, tk, kj, SWA_WINDOW))
    outs = []
    for r in range(heads):
        m, acc = state[r]
        sink = sink_ref[:, r * tq:(r + 1) * tq]
        m_all = jnp.maximum(m, sink)
        scale = jnp.exp2(m - m_all)
        outs.append(acc[:HEAD_DIM] * scale / (acc[HEAD_DIM:HEAD_DIM + 1] * scale + jnp.exp2(sink - m_all)))
    o_ref[...] = jnp.concatenate(outs, axis=0).T.astype(o_ref.dtype)


def _swa(sq_t, sink_row, k, v, b, s_len):
    tq = SWA_TILE
    nq = s_len // tq
    return pl.pallas_call(
        _swa_kernel,
        grid=(b, nq),
        in_specs=[pl.BlockSpec((SWA_Q_W, tq), lambda bi, qi: (0, bi * nq + qi)),
                  _resident(sink_row.shape),
                  pl.BlockSpec((1, s_len, HEAD_DIM), lambda bi, qi: (bi, 0, 0)),
                  pl.BlockSpec((nq, VAL_ROWS, tq), lambda bi, qi: (bi, 0, 0))],
        out_specs=pl.BlockSpec((tq, SWA_Q_W), lambda bi, qi: (bi * nq + qi, 0)),
        out_shape=jax.ShapeDtypeStruct((b * s_len, SWA_Q_W), BF16),
        compiler_params=_params(2),
        name="swa",
    )(sq_t, sink_row, k.reshape(b, s_len, HEAD_DIM), v)


def _merge_kernel(h_ref, oa_ref, ob_ref, gab_ref, wa_ref, wb_ref, wo_ref, o_ref):
    d = h_ref.shape[1]
    gab = gab_ref[...]
    merged = gab[:, :d] * _dot(oa_ref[...], wa_ref[...]) + gab[:, d:] * _dot(ob_ref[...], wb_ref[...])
    o_ref[...] = h_ref[...] + _dot(merged.astype(BF16), wo_ref[...])


def _merge(h2, oa, ob, gab, wa, wb, wo):
    n, d = h2.shape
    tm = ROW_TILE
    rows = lambda w: pl.BlockSpec((tm, w), lambda i: (i, 0))
    return pl.pallas_call(
        _merge_kernel,
        grid=(n // tm,),
        in_specs=[rows(d), rows(NSA_Q_W), rows(SWA_Q_W), rows(2 * d),
                  _resident(wa.shape), _resident(wb.shape), _resident(wo.shape)],
        out_specs=rows(d),
        out_shape=jax.ShapeDtypeStruct((n, d), F32),
        compiler_params=_params(1),
        name="merge",
    )(h2, oa, ob, gab, wa, wb, wo)


def _overlap_matrix(n_sel, n_cmp, n_cmp_pad):
    cs = np.arange(n_cmp) * CMP_STRIDE
    ss = np.arange(n_sel) * SEL_LEN
    ov = np.clip(np.minimum(cs[None, :] + CMP_LEN, ss[:, None] + SEL_LEN)
                 - np.maximum(cs[None, :], ss[:, None]), 0, None).astype(np.float32) / CMP_LEN
    return np.pad(ov, ((0, 0), (0, n_cmp_pad - n_cmp)))


def _layer(h, positions, w, b, s_len):
    n, d = h.shape
    g_kv = NSA_KV
    n_sel = s_len // SEL_LEN
    n_top = min(SEL_TOPN, n_sel)
    nc = s_len // CMP_STRIDE
    bf = lambda a: a.astype(BF16)

    h = _ffn(h, w['norm_ffn1'][None], bf(w['ffn1_gate']), bf(w['ffn1_up']), bf(w['ffn1_down']),
             w['norm_ffn1'][None], False)

    pts = np.cumsum((NSA_Q_W,) + (NSA_KV_W,) * 6 + (NSA_GATE_W, SWA_Q_W, SWA_KV_W, SWA_KV_W, d, d))[:-1]
    (w_nq, w_kc, w_vc, w_ksl, w_vsl, w_kwn, w_vwn, w_ng, w_sq, w_sk, w_sv, w_ga, w_gb) = jnp.split(
        w['w_in'], pts, axis=1)
    parts = dict(qn=w_nq, sq=w_sq, ksel=w_ksl, kwin=w_kwn, kswa=w_sk, vsel=w_vsl, vwin=w_vwn, vswa=w_sv,
                 ng=w_ng, kc=w_kc, vc=w_vc)
    w_t = jnp.concatenate([jnp.pad(parts[name], ((0, 0), (0, width - parts[name].shape[1])))
                           for name, width in PROJ_ROWS], axis=1).T
    w_n = jnp.concatenate([w_ga, w_gb], axis=1)

    freq_col = (ROPE_THETA ** (-jnp.arange(HALF_DIM, dtype=F32) / HALF_DIM))[:, None]
    (qn_t, sq_t, ksel, kwin, kswa, vsel, vwin, vswa, ng_t, kcv, gab) = _proj(
        h, w['norm_mix'][None], bf(w_t), bf(w_n), positions.reshape(1, n), freq_col)

    pos_c = jnp.pad(positions[:, CMP_LEN - 1::CMP_STRIDE], ((0, 0), (0, 1)))[:, None, :]
    w1 = lambda a: bf(a.reshape(CMP_LEN, HEAD_DIM, a.shape[-1]))
    kc, vc_t = _compress(kcv, w['cmp_pe_k'], w['cmp_pe_v'], w1(w['cmp_k_w1']), w1(w['cmp_v_w1']),
                         bf(w['cmp_k_w2'].T), bf(w['cmp_v_w2'].T), pos_c, freq_col, b, g_kv, s_len)

    gates = ng_t.reshape(3, g_kv, NSA_REP, n)
    ov = jnp.asarray(_overlap_matrix(n_sel, nc - 1, nc), BF16)
    mb, oc_t = _cmp_topk(qn_t, kc, vc_t, ov, gates, b, g_kv, s_len, n_top)

    expand = jnp.asarray(np.arange(s_len)[:, None] // SEL_LEN == np.arange(n_sel)[None, :], BF16)
    o_a = _nsa(qn_t, mb, oc_t, gates, ksel, expand, vsel, kwin, vwin, b, g_kv, s_len)

    sink_row = jnp.repeat(w['swa_sinks'].astype(F32) * LOG2E, SWA_TILE)[None]
    o_b = _swa(sq_t, sink_row, kswa, vswa, b, s_len)

    return _merge(h, o_a, o_b, gab, bf(w['w_branch_a']), bf(w['w_branch_b']), bf(w['w_out']))


def kernel(x, positions, norm_ffn1, ffn1_gate, ffn1_up, ffn1_down, norm_mix, w_in, cmp_pe_k, cmp_k_w1, cmp_k_w2, cmp_pe_v, cmp_v_w1, cmp_v_w2, swa_sinks, w_branch_a, w_branch_b, w_out, norm_ffn2, ffn2_gate, ffn2_up, ffn2_down, norm_final):
    b, s_len, d = x.shape
    stacked = dict(norm_ffn1=norm_ffn1, ffn1_gate=ffn1_gate, ffn1_up=ffn1_up, ffn1_down=ffn1_down,
                   norm_mix=norm_mix, w_in=w_in, cmp_pe_k=cmp_pe_k, cmp_k_w1=cmp_k_w1, cmp_k_w2=cmp_k_w2,
                   cmp_pe_v=cmp_pe_v, cmp_v_w1=cmp_v_w1, cmp_v_w2=cmp_v_w2, swa_sinks=swa_sinks,
                   w_branch_a=w_branch_a, w_branch_b=w_branch_b, w_out=w_out)
    depth = norm_ffn1.shape[0]
    h = x.reshape(b * s_len, d)
    for i in range(depth):
        w = {k: v[i] for k, v in stacked.items()}
        h = _layer(h, positions, w, b, s_len)
        last = i == depth - 1
        h = _ffn(h, norm_ffn2[i][None], ffn2_gate[i].astype(BF16), ffn2_up[i].astype(BF16),
                 ffn2_down[i].astype(BF16), norm_final[None], last)
    return h.reshape(b, s_len, d)
```

```python
import functools

import numpy as np
import jax
import jax.numpy as jnp
from jax import lax
from jax.experimental import pallas as pl
from jax.experimental.pallas import tpu as pltpu

HEAD_DIM = 64
HALF_DIM = HEAD_DIM // 2
NSA_HEADS = 8
NSA_KV = 2
NSA_REP = NSA_HEADS // NSA_KV
SWA_HEADS = 8
CMP_STRIDE = 16
CMP_LEN = 2 * CMP_STRIDE
SEL_LEN = 64
SEL_SHIFT = 6
SEL_TOPN = 16
SEL_LOCAL = 2
NSA_WINDOW = 512
SWA_WINDOW = 128
ROPE_THETA = 10000.0
RMS_EPS = 1e-6
FFN_HALF = 0.5
NEG_INF = -1e30
FORCE = 1e9
LOG2E = 1.4426950408889634
Q_SCALE = HEAD_DIM ** -0.5 * LOG2E
BF16_ROWS = 16
VAL_ROWS = HEAD_DIM + BF16_ROWS
LANES = 128

NSA_Q_W = NSA_HEADS * HEAD_DIM
NSA_KV_W = NSA_KV * HEAD_DIM
NSA_GATE_W = 3 * NSA_HEADS
SWA_Q_W = SWA_HEADS * HEAD_DIM
SWA_KV_W = HEAD_DIM

VMEM_LIMIT_BYTES = 56 * 1024 * 1024

BF16 = jnp.bfloat16
F32 = jnp.float32

ROW_TILE = 512
CMP_TILE = 1024
CMP_SUBTILE = 256
NSA_TILE = 256
SWA_TILE = 128


def _params(n_axes):
    return pltpu.CompilerParams(dimension_semantics=("arbitrary",) * n_axes,
                                vmem_limit_bytes=VMEM_LIMIT_BYTES)


def _resident(shape):
    zeros = (0,) * len(shape)
    return pl.BlockSpec(shape, lambda *_: zeros, pipeline_mode=pl.Buffered(1))


def _rms(x, g):
    y = x * lax.rsqrt(jnp.mean(x * x, axis=-1, keepdims=True) + RMS_EPS)
    return y * g


def _dot(a, b):
    return jnp.dot(a, b, preferred_element_type=F32)


def _dot_nt(a, b):
    return lax.dot_general(a, b, (((1,), (1,)), ((), ())), preferred_element_type=F32)


def _rope_angles(pos_row, freq_col):
    ang = pos_row.astype(F32) * freq_col
    return jnp.cos(ang), jnp.sin(ang)


def _rope_rows(block, cos_t, sin_t):
    out = []
    for hd in range(block.shape[0] // HEAD_DIM):
        x1 = block[hd * HEAD_DIM:hd * HEAD_DIM + HALF_DIM]
        x2 = block[hd * HEAD_DIM + HALF_DIM:(hd + 1) * HEAD_DIM]
        out += [x1 * cos_t - x2 * sin_t, x2 * cos_t + x1 * sin_t]
    return out


def _ones_rows(tk):
    return jnp.where(lax.broadcasted_iota(jnp.int32, (BF16_ROWS, tk), 0) == 0, 1.0, 0.0).astype(BF16)


def _ffn_kernel(x_ref, g_ref, wg_ref, wu_ref, wd_ref, gf_ref, o_ref, *, final_norm):
    x = x_ref[...]
    xb = _rms(x, g_ref[...]).astype(BF16)
    a = _dot(xb, wg_ref[...])
    b = _dot(xb, wu_ref[...])
    t = (a * jax.nn.sigmoid(a)) * b
    h = x + FFN_HALF * _dot(t.astype(BF16), wd_ref[...])
    if final_norm:
        h = _rms(h, gf_ref[...])
    o_ref[...] = h


def _ffn(x2, g, wg, wu, wd, gf, final_norm):
    n, d = x2.shape
    f = wg.shape[1]
    tm = ROW_TILE
    row = pl.BlockSpec((tm, d), lambda i: (i, 0))
    return pl.pallas_call(
        functools.partial(_ffn_kernel, final_norm=final_norm),
        grid=(n // tm,),
        in_specs=[row, _resident((1, d)), _resident((d, f)), _resident((d, f)), _resident((f, d)),
                  _resident((1, d))],
        out_specs=row,
        out_shape=jax.ShapeDtypeStruct((n, d), F32),
        compiler_params=_params(1),
        name="ffn_final" if final_norm else "ffn",
    )(x2, g, wg, wu, wd, gf)


PROJ_ROWS = (('qn', NSA_Q_W), ('sq', SWA_Q_W), ('ksel', NSA_KV_W), ('kwin', NSA_KV_W), ('kswa', LANES),
             ('vsel', NSA_KV_W), ('vwin', NSA_KV_W), ('vswa', SWA_KV_W), ('ng', 32),
             ('kc', NSA_KV_W), ('vc', NSA_KV_W))


def _proj_kernel(h_ref, g_ref, wt_ref, wn_ref, pos_ref, freq_ref,
                 qn_ref, sq_ref, ksel_ref, kwin_ref, kswa_ref, vsel_ref, vwin_ref, vswa_ref,
                 ng_ref, kcv_ref, gab_ref):
    ub = _rms(h_ref[...], g_ref[...]).astype(BF16)
    yt = _dot_nt(wt_ref[...], ub)
    rows, o = {}, 0
    for name, width in PROJ_ROWS:
        rows[name] = yt[o:o + width]
        o += width
    cos_t, sin_t = _rope_angles(pos_ref[...], freq_ref[...])

    for name, ref in (('qn', qn_ref), ('sq', sq_ref)):
        for i, piece in enumerate(_rope_rows(rows[name], cos_t, sin_t)):
            ref[i * HALF_DIM:(i + 1) * HALF_DIM, :] = (piece * Q_SCALE).astype(BF16)

    for name, ref in (('ksel', ksel_ref), ('kwin', kwin_ref)):
        k_nat = jnp.concatenate(_rope_rows(rows[name], cos_t, sin_t), axis=0).T
        for g in range(NSA_KV):
            ref[g] = k_nat[:, g * HEAD_DIM:(g + 1) * HEAD_DIM].astype(BF16)
    kswa = _rope_rows(rows['kswa'][:HEAD_DIM], cos_t, sin_t) + [rows['kswa'][HEAD_DIM:]]
    kswa_ref[...] = jnp.concatenate(kswa, axis=0).T[:, :HEAD_DIM].astype(BF16)

    for name, ref, groups in (('vsel', vsel_ref, NSA_KV), ('vwin', vwin_ref, NSA_KV), ('vswa', vswa_ref, 1)):
        tk = ref.shape[2]
        for j in range(ref.shape[0]):
            for g in range(groups):
                ref[j, g * VAL_ROWS:g * VAL_ROWS + HEAD_DIM, :] = (
                    rows[name][g * HEAD_DIM:(g + 1) * HEAD_DIM, j * tk:(j + 1) * tk].astype(BF16))
                ref[j, g * VAL_ROWS + HEAD_DIM:(g + 1) * VAL_ROWS, :] = _ones_rows(tk)

    ng_ref[...] = jax.nn.sigmoid(rows['ng'][:NSA_GATE_W])

    for i, name in enumerate(('kc', 'vc')):
        nat = rows[name].T
        for g in range(NSA_KV):
            kcv_ref[i * NSA_KV + g] = nat[:, g * HEAD_DIM:(g + 1) * HEAD_DIM]

    gab_ref[...] = jax.nn.sigmoid(_dot(ub, wn_ref[...]))


def _proj(h2, g, wt, wn, pos_row, freq_col):
    n, d = h2.shape
    tm = ROW_TILE
    rows = lambda w: pl.BlockSpec((tm, w), lambda i: (i, 0))
    cols = lambda w: pl.BlockSpec((w, tm), lambda i: (0, i))
    grouped = lambda k: pl.BlockSpec((k, tm, HEAD_DIM), lambda i: (0, i, 0))
    tiles = lambda groups, tk: pl.BlockSpec((tm // tk, groups * VAL_ROWS, tk), lambda i: (i, 0, 0))
    val_shape = lambda groups, tk: jax.ShapeDtypeStruct((n // tk, groups * VAL_ROWS, tk), BF16)
    out_shape = [
        jax.ShapeDtypeStruct((NSA_Q_W, n), BF16), jax.ShapeDtypeStruct((SWA_Q_W, n), BF16),
        jax.ShapeDtypeStruct((NSA_KV, n, HEAD_DIM), BF16), jax.ShapeDtypeStruct((NSA_KV, n, HEAD_DIM), BF16),
        jax.ShapeDtypeStruct((n, HEAD_DIM), BF16),
        val_shape(NSA_KV, NSA_TILE), val_shape(NSA_KV, NSA_TILE), val_shape(1, SWA_TILE),
        jax.ShapeDtypeStruct((NSA_GATE_W, n), F32), jax.ShapeDtypeStruct((2 * NSA_KV, n, HEAD_DIM), F32),
        jax.ShapeDtypeStruct((n, 2 * d), F32),
    ]
    return pl.pallas_call(
        _proj_kernel,
        grid=(n // tm,),
        in_specs=[rows(d), _resident((1, d)), _resident(wt.shape), _resident(wn.shape),
                  cols(1), _resident(freq_col.shape)],
        out_specs=[cols(NSA_Q_W), cols(SWA_Q_W), grouped(NSA_KV), grouped(NSA_KV), rows(HEAD_DIM),
                   tiles(NSA_KV, NSA_TILE), tiles(NSA_KV, NSA_TILE), tiles(1, SWA_TILE),
                   cols(NSA_GATE_W), grouped(2 * NSA_KV), rows(2 * d)],
        out_shape=out_shape,
        compiler_params=_params(1),
        name="proj",
    )(h2, g, wt, wn, pos_row, freq_col)


def _gelu(x):
    return jax.nn.gelu(x, approximate=True)


def _compress_kernel(kc_ref, vc_ref, pek_ref, pev_ref, w1k_ref, w1v_ref, w2kt_ref, w2vt_ref,
                     pos_ref, freq_ref, kc_out, vct_out):
    nc = kc_out.shape[1]

    def hidden(x_ref, pe_ref, w1_ref):
        top = bot = None
        for j in range(CMP_STRIDE):
            x = x_ref[0, pl.ds(j, nc, stride=CMP_STRIDE), :]
            t = _dot((x + pe_ref[j:j + 1, :]).astype(BF16), w1_ref[j])
            b = _dot((x + pe_ref[CMP_STRIDE + j:CMP_STRIDE + j + 1, :]).astype(BF16), w1_ref[CMP_STRIDE + j])
            top, bot = (t, b) if top is None else (top + t, bot + b)
        return _gelu(top + pltpu.roll(bot, shift=nc - 1, axis=0)).astype(BF16)

    kt = _dot_nt(w2kt_ref[...], hidden(kc_ref, pek_ref, w1k_ref))
    cos_t, sin_t = _rope_angles(pos_ref[0], freq_ref[...])
    kt = jnp.concatenate(_rope_rows(kt, cos_t, sin_t) + [jnp.zeros((LANES - HEAD_DIM, nc), F32)], axis=0)
    kc_out[0] = kt.T[:, :HEAD_DIM].astype(BF16)
    vct_out[0] = _dot_nt(w2vt_ref[...], hidden(vc_ref, pev_ref, w1v_ref)).astype(BF16)


def _compress(kcv, pek, pev, w1k, w1v, w2kt, w2vt, pos_c, freq_col, b, g_kv, s_len):
    nc = s_len // CMP_STRIDE
    hid = w1k.shape[-1]
    kcv = kcv.reshape(2 * g_kv * b, s_len, HEAD_DIM)
    src = lambda kind: pl.BlockSpec((1, s_len, HEAD_DIM),
                                    lambda i: ((kind * g_kv + i % g_kv) * b + i // g_kv, 0, 0))
    per = lambda *s: pl.BlockSpec((1,) + s, lambda i: (i,) + (0,) * len(s))
    return pl.pallas_call(
        _compress_kernel,
        grid=(b * g_kv,),
        in_specs=[src(0), src(1), _resident(pek.shape), _resident(pev.shape),
                  _resident(w1k.shape), _resident(w1v.shape), _resident((HEAD_DIM, hid)),
                  _resident((HEAD_DIM, hid)),
                  pl.BlockSpec((1, 1, nc), lambda i: (i // g_kv, 0, 0)), _resident(freq_col.shape)],
        out_specs=[per(nc, HEAD_DIM), per(HEAD_DIM, nc)],
        out_shape=[jax.ShapeDtypeStruct((b * g_kv, nc, HEAD_DIM), BF16),
                   jax.ShapeDtypeStruct((b * g_kv, HEAD_DIM, nc), BF16)],
        compiler_params=_params(1),
        name="compress",
    )(kcv, kcv, pek, pev, w1k, w1v, w2kt, w2vt, pos_c, freq_col)


def _cmp_topk_kernel(q_ref, kc_ref, vct_ref, ov_ref, gate_ref, mb_ref, oc_ref,
                     s_buf, p_buf, bias_buf, psum_buf, x_buf, *, n_top):
    tq = q_ref.shape[1]
    sub = s_buf.shape[1]
    nc = kc_ref.shape[1]
    n_sel = ov_ref.shape[0]
    rep = NSA_REP
    qi = pl.program_id(2)
    q0 = qi * tq
    stages = [(u, r) for u in range(tq // sub) for r in range(rep)]
    q_cols = lambda u, r: q_ref[r * HEAD_DIM:(r + 1) * HEAD_DIM, u * sub:(u + 1) * sub]

    s_buf[...] = _dot(kc_ref[0], q_cols(*stages[0]))
    inv_prev = None
    for n, (u, r) in enumerate(stages):
        cols = slice(u * sub, (u + 1) * sub)
        if n + 1 < len(stages):
            s_next = _dot(kc_ref[0], q_cols(*stages[n + 1]))
        if n > 0:
            pu, pr = stages[n - 1]
            oc_ref[pr * HEAD_DIM:(pr + 1) * HEAD_DIM, pu * sub:(pu + 1) * sub] = (
                _dot(vct_ref[0], p_buf[...]) * (inv_prev * gate_ref[0, 0, pr:pr + 1, pu * sub:(pu + 1) * sub]))
        last = ((q0 + u * sub + lax.broadcasted_iota(jnp.int32, (1, sub), 1)) - (CMP_LEN - 1)) >> 4
        if r == 0:
            bias_buf[...] = jnp.where(lax.broadcasted_iota(jnp.int32, (nc, sub), 0) <= last, 0.0, NEG_INF)
        s = s_buf[...] + bias_buf[...]
        m = jnp.max(s, axis=0, keepdims=True)
        e = jnp.exp2(s - m)
        inv_prev = jnp.where(last >= 0, 1.0 / jnp.sum(e, axis=0, keepdims=True), 0.0)
        p_buf[...] = e.astype(BF16)
        if r == 0:
            psum_buf[:, cols] = e * inv_prev
        else:
            psum_buf[:, cols] += e * inv_prev
        if n + 1 < len(stages):
            s_buf[...] = s_next
    pu, pr = stages[-1]
    oc_ref[pr * HEAD_DIM:(pr + 1) * HEAD_DIM, pu * sub:(pu + 1) * sub] = (
        _dot(vct_ref[0], p_buf[...]) * (inv_prev * gate_ref[0, 0, pr:pr + 1, pu * sub:(pu + 1) * sub]))

    psum = psum_buf[...]
    p_hi = psum.astype(BF16)
    p_lo = (psum - p_hi.astype(F32)).astype(BF16)
    imp = _dot(ov_ref[...], p_hi) + _dot(ov_ref[...], p_lo)

    blk = lax.broadcasted_iota(jnp.int32, (n_sel, tq), 0)
    tb = (q0 + lax.broadcasted_iota(jnp.int32, (n_sel, tq), 1)) >> SEL_SHIFT
    forced = (blk == 0) | ((tb - blk >= 0) & (tb - blk < SEL_LOCAL))
    n_forced = 1 + SEL_LOCAL
    premark = forced & (jnp.full((n_sel, tq), qi, jnp.int32) > 0)
    x_buf[...] = jnp.where(premark, -jnp.inf, jnp.where(forced, FORCE, jnp.where(blk > tb, -FORCE, imp)))

    def pick(_, carry):
        x = x_buf[...]
        top = jnp.max(x, axis=0, keepdims=True)
        first = jnp.min(jnp.where(x == top, blk, n_sel), axis=0, keepdims=True)
        x_buf[...] = jnp.where(blk == first, -jnp.inf, x)
        return carry

    lax.fori_loop(0, jnp.where(qi > 0, n_top - n_forced, n_top), pick, 0)
    mb_ref[0] = jnp.where(x_buf[...] == -jnp.inf, 0.0, NEG_INF).astype(BF16)


def _cmp_topk(qn_t, kc, vct, ov, gates, b, g_kv, s_len, n_top):
    tq = min(CMP_TILE, s_len)
    sub = min(CMP_SUBTILE, tq)
    nq = s_len // tq
    nc = kc.shape[1]
    n_sel = ov.shape[0]
    rep = NSA_REP
    return pl.pallas_call(
        functools.partial(_cmp_topk_kernel, n_top=n_top),
        grid=(b, g_kv, nq),
        in_specs=[
            pl.BlockSpec((rep * HEAD_DIM, tq), lambda bi, gi, qi: (gi, bi * nq + qi)),
            pl.BlockSpec((1, nc, HEAD_DIM), lambda bi, gi, qi: (bi * g_kv + gi, 0, 0)),
            pl.BlockSpec((1, HEAD_DIM, nc), lambda bi, gi, qi: (bi * g_kv + gi, 0, 0)),
            _resident(ov.shape),
            pl.BlockSpec((1, 1, rep, tq), lambda bi, gi, qi: (0, gi, 0, bi * nq + qi)),
        ],
        out_specs=[
            pl.BlockSpec((1, n_sel, tq), lambda bi, gi, qi: (bi * g_kv + gi, 0, qi)),
            pl.BlockSpec((rep * HEAD_DIM, tq), lambda bi, gi, qi: (gi, bi * nq + qi)),
        ],
        out_shape=[jax.ShapeDtypeStruct((b * g_kv, n_sel, s_len), BF16),
                   jax.ShapeDtypeStruct((NSA_Q_W, b * s_len), F32)],
        scratch_shapes=[pltpu.VMEM((nc, sub), F32), pltpu.VMEM((nc, sub), BF16), pltpu.VMEM((nc, sub), F32),
                        pltpu.VMEM((nc, tq), F32), pltpu.VMEM((n_sel, tq), F32)],
        compiler_params=_params(3),
        name="cmp_topk",
    )(qn_t, kc, vct, ov, gates)


def _softmax_tile(m, s):
    m_new = jnp.maximum(m, jnp.max(s, axis=0, keepdims=True))
    return m_new, jnp.exp2(m - m_new), jnp.exp2(s - m_new).astype(BF16)


def _flash_step(carry, s, v_aug):
    m, acc = carry
    m_new, alpha, p = _softmax_tile(m, s)
    return m_new, alpha * acc + _dot(v_aug, p)


def _flash_init(n_heads, tq):
    return tuple((jnp.full((1, tq), NEG_INF, F32), jnp.zeros((VAL_ROWS, tq), F32)) for _ in range(n_heads))


def _normalize(acc):
    return acc[:HEAD_DIM] / acc[HEAD_DIM:HEAD_DIM + 1]


def _flash_heads(state, k_tile, q_heads, v_t, bias=None):
    scores = [_dot(k_tile, q_h) for q_h in q_heads]
    if bias is not None:
        scores = [s + bias for s in scores]
    return tuple(_flash_step(carry, s, v_t) for carry, s in zip(state, scores))


def _rel_pos(tk, tq):
    return lax.broadcasted_iota(jnp.int32, (tk, tq), 1) - lax.broadcasted_iota(jnp.int32, (tk, tq), 0)


def _window_bias(rel, back, tk, kj, window):
    dist = rel + (back * tk + jnp.where(kj >= 0, 0, window))
    return jnp.where((dist >= 0) & (dist < window), 0.0, NEG_INF)


def _key_tile(k_ref, j, tk):
    return k_ref[0, pl.ds(pl.multiple_of(j * tk, tk), tk), :]


def _pipe_stage(s_buf, p_buf, heads, cur=None, nxt=None, prv=None):
    if nxt is not None:
        s_next = [_dot(nxt[0], nxt[1][r][...]) for r in heads]
    if prv is not None:
        pv = [_dot(prv[0], p_buf[r]) for r in heads]
    out = None
    if cur is not None:
        m, bias = cur
        out = []
        for r in heads:
            s = s_buf[r] if bias is None else s_buf[r] + bias
            m_r, a_r, p_buf[r] = _softmax_tile(m[r], s)
            out.append((m_r, a_r))
        out = tuple(zip(*out))
    for r in heads:
        if prv is not None:
            prv[1][r] = prv[2][r] * prv[1][r] + pv[r]
        if nxt is not None:
            s_buf[r] = s_next[r]
    return out


def _nsa_kernel(q_ref, mb_ref, oc_ref, gate_ref, ks_ref, ex_ref, vs_ref, kw_ref, vw_ref, o_ref,
                qa_buf, s_buf, p_buf, sel_acc, win_acc):
    tq = q_ref.shape[1]
    tk = tq
    n_sel = mb_ref.shape[1]
    rep = NSA_REP
    heads = range(rep)
    qi = pl.program_id(2)
    stage = functools.partial(_pipe_stage, s_buf, p_buf, heads)

    def sel_keys(j):
        at = pl.ds(pl.multiple_of(j * tk, tk), tk)
        return jnp.concatenate([ex_ref[at, :], ks_ref[0, at, :]], axis=1)

    q_heads = [q_ref.at[pl.ds(r * HEAD_DIM, HEAD_DIM), :] for r in heads]
    q_aug = [qa_buf.at[r] for r in heads]
    for r in heads:
        qa_buf[r, 0:n_sel, :] = mb_ref[0]
        qa_buf[r, n_sel:n_sel + HEAD_DIM, :] = q_heads[r][...]
        s_buf[r] = _dot(sel_keys(0), qa_buf[r])
        p_buf[r] = jnp.zeros((tk, tq), BF16)
        sel_acc[r] = jnp.zeros((VAL_ROWS, tq), F32)
        win_acc[r] = jnp.zeros((VAL_ROWS, tq), F32)
    row = lambda v: tuple(jnp.full((1, tq), v, F32) for _ in heads)

    def sel_stage(i, carry):
        m, alpha = carry
        return stage(cur=(m, None), nxt=(sel_keys(i + 1), q_aug),
                     prv=(vs_ref[jnp.maximum(i - 1, 0)], sel_acc, alpha))

    carry = lax.fori_loop(0, qi >> 1, lambda j, c: sel_stage(2 * j + 1, sel_stage(2 * j, c)),
                          (row(NEG_INF), row(1.0)))
    m, alpha = lax.fori_loop(0, qi & 1, lambda _, c: sel_stage(qi - 1, c), carry)

    rel = _rel_pos(tk, tq)
    backs = list(range(NSA_WINDOW // tk, -1, -1))
    win_tile = [jnp.maximum(qi - back, 0) for back in backs]
    win_bias = [_window_bias(rel, back, tk, qi - back, NSA_WINDOW) for back in backs]
    _, alpha_d = stage(cur=(m, jnp.where(rel >= 0, 0.0, NEG_INF)),
                       nxt=(_key_tile(kw_ref, win_tile[0], tk), q_heads),
                       prv=(vs_ref[jnp.maximum(qi - 1, 0)], sel_acc, alpha))
    prv = (vs_ref[qi], sel_acc, alpha_d)
    m_w = row(NEG_INF)
    for n in range(len(backs)):
        nxt = (_key_tile(kw_ref, win_tile[n + 1], tk), q_heads) if n + 1 < len(backs) else None
        m_w, alpha_w = stage(cur=(m_w, win_bias[n]), nxt=nxt, prv=prv)
        prv = (vw_ref[win_tile[n]], win_acc, alpha_w)
    stage(prv=prv)

    outs = [oc_ref[r * HEAD_DIM:(r + 1) * HEAD_DIM, :]
            + gate_ref[1, 0, r:r + 1, :] * _normalize(sel_acc[r])
            + gate_ref[2, 0, r:r + 1, :] * _normalize(win_acc[r]) for r in heads]
    o_ref[...] = jnp.concatenate(outs, axis=0).T.astype(o_ref.dtype)


def _nsa(qn_t, mb, oc_t, gates, ksel, expand, vsel, kwin, vwin, b, g_kv, s_len):
    tq = NSA_TILE
    nq = s_len // tq
    rep = NSA_REP
    n_sel = mb.shape[1]
    keys = pl.BlockSpec((1, s_len, HEAD_DIM), lambda bi, gi, qi: (gi * b + bi, 0, 0))
    vals = pl.BlockSpec((nq, VAL_ROWS, tq), lambda bi, gi, qi: (bi, gi, 0))
    return pl.pallas_call(
        _nsa_kernel,
        grid=(b, g_kv, nq),
        in_specs=[
            pl.BlockSpec((rep * HEAD_DIM, tq), lambda bi, gi, qi: (gi, bi * nq + qi)),
            pl.BlockSpec((1, n_sel, tq), lambda bi, gi, qi: (bi * g_kv + gi, 0, qi)),
            pl.BlockSpec((rep * HEAD_DIM, tq), lambda bi, gi, qi: (gi, bi * nq + qi)),
            pl.BlockSpec((3, 1, rep, tq), lambda bi, gi, qi: (0, gi, 0, bi * nq + qi)),
            keys, _resident(expand.shape), vals, keys, vals,
        ],
        out_specs=pl.BlockSpec((tq, rep * HEAD_DIM), lambda bi, gi, qi: (bi * nq + qi, gi)),
        out_shape=jax.ShapeDtypeStruct((b * s_len, NSA_Q_W), BF16),
        scratch_shapes=[pltpu.VMEM((rep, n_sel + HEAD_DIM, tq), BF16), pltpu.VMEM((rep, tq, tq), F32),
                        pltpu.VMEM((rep, tq, tq), BF16), pltpu.VMEM((rep, VAL_ROWS, tq), F32),
                        pltpu.VMEM((rep, VAL_ROWS, tq), F32)],
        compiler_params=_params(3),
        name="nsa",
    )(qn_t, mb, oc_t, gates, ksel.reshape(g_kv * b, s_len, HEAD_DIM), expand, vsel,
      kwin.reshape(g_kv * b, s_len, HEAD_DIM), vwin)


def _swa_kernel(q_ref, sink_ref, k_ref, v_ref, o_ref):
    tq = q_ref.shape[1]
    tk = tq
    heads = SWA_HEADS
    qi = pl.program_id(1)

    q_heads = [q_ref[r * HEAD_DIM:(r + 1) * HEAD_DIM, :] for r in range(heads)]
    rel = _rel_pos(tk, tq)
    n_back = SWA_WINDOW // tk
    state = _flash_init(heads, tq)
    for back in range(n_back, -1, -1):
        kj = qi - back
        kjc = jnp.maximum(kj, 0)
        state = _flash_heads(state, _key_tile(k_ref, kjc, tk), q_heads, v_ref[kjc],
                             _window_bias(rel, back, tk, kj, SWA_WINDOW))
    outs = []
    for r in range(heads):
        m, acc = state[r]
        sink = sink_ref[:, r * tq:(r + 1) * tq]
        m_all = jnp.maximum(m, sink)
        scale = jnp.exp2(m - m_all)
        outs.append(acc[:HEAD_DIM] * scale / (acc[HEAD_DIM:HEAD_DIM + 1] * scale + jnp.exp2(sink - m_all)))
    o_ref[...] = jnp.concatenate(outs, axis=0).T.astype(o_ref.dtype)


def _swa(sq_t, sink_row, k, v, b, s_len):
    tq = SWA_TILE
    nq = s_len // tq
    return pl.pallas_call(
        _swa_kernel,
        grid=(b, nq),
        in_specs=[pl.BlockSpec((SWA_Q_W, tq), lambda bi, qi: (0, bi * nq + qi)),
                  _resident(sink_row.shape),
                  pl.BlockSpec((1, s_len, HEAD_DIM), lambda bi, qi: (bi, 0, 0)),
                  pl.BlockSpec((nq, VAL_ROWS, tq), lambda bi, qi: (bi, 0, 0))],
        out_specs=pl.BlockSpec((tq, SWA_Q_W), lambda bi, qi: (bi * nq + qi, 0)),
        out_shape=jax.ShapeDtypeStruct((b * s_len, SWA_Q_W), BF16),
        compiler_params=_params(2),
        name="swa",
    )(sq_t, sink_row, k.reshape(b, s_len, HEAD_DIM), v)


def _merge_kernel(h_ref, oa_ref, ob_ref, gab_ref, wa_ref, wb_ref, wo_ref, o_ref):
    d = h_ref.shape[1]
    gab = gab_ref[...]
    merged = gab[:, :d] * _dot(oa_ref[...], wa_ref[...]) + gab[:, d:] * _dot(ob_ref[...], wb_ref[...])
    o_ref[...] = h_ref[...] + _dot(merged.astype(BF16), wo_ref[...])


def _merge(h2, oa, ob, gab, wa, wb, wo):
    n, d = h2.shape
    tm = ROW_TILE
    rows = lambda w: pl.BlockSpec((tm, w), lambda i: (i, 0))
    return pl.pallas_call(
        _merge_kernel,
        grid=(n // tm,),
        in_specs=[rows(d), rows(NSA_Q_W), rows(SWA_Q_W), rows(2 * d),
                  _resident(wa.shape), _resident(wb.shape), _resident(wo.shape)],
        out_specs=rows(d),
        out_shape=jax.ShapeDtypeStruct((n, d), F32),
        compiler_params=_params(1),
        name="merge",
    )(h2, oa, ob, gab, wa, wb, wo)


def _overlap_matrix(n_sel, n_cmp, n_cmp_pad):
    cs = np.arange(n_cmp) * CMP_STRIDE
    ss = np.arange(n_sel) * SEL_LEN
    ov = np.clip(np.minimum(cs[None, :] + CMP_LEN, ss[:, None] + SEL_LEN)
                 - np.maximum(cs[None, :], ss[:, None]), 0, None).astype(np.float32) / CMP_LEN
    return np.pad(ov, ((0, 0), (0, n_cmp_pad - n_cmp)))


def _layer(h, positions, w, b, s_len):
    n, d = h.shape
    g_kv = NSA_KV
    n_sel = s_len // SEL_LEN
    n_top = min(SEL_TOPN, n_sel)
    nc = s_len // CMP_STRIDE
    bf = lambda a: a.astype(BF16)

    h = _ffn(h, w['norm_ffn1'][None], bf(w['ffn1_gate']), bf(w['ffn1_up']), bf(w['ffn1_down']),
             w['norm_ffn1'][None], False)

    pts = np.cumsum((NSA_Q_W,) + (NSA_KV_W,) * 6 + (NSA_GATE_W, SWA_Q_W, SWA_KV_W, SWA_KV_W, d, d))[:-1]
    (w_nq, w_kc, w_vc, w_ksl, w_vsl, w_kwn, w_vwn, w_ng, w_sq, w_sk, w_sv, w_ga, w_gb) = jnp.split(
        w['w_in'], pts, axis=1)
    parts = dict(qn=w_nq, sq=w_sq, ksel=w_ksl, kwin=w_kwn, kswa=w_sk, vsel=w_vsl, vwin=w_vwn, vswa=w_sv,
                 ng=w_ng, kc=w_kc, vc=w_vc)
    w_t = jnp.concatenate([jnp.pad(parts[name], ((0, 0), (0, width - parts[name].shape[1])))
                           for name, width in PROJ_ROWS], axis=1).T
    w_n = jnp.concatenate([w_ga, w_gb], axis=1)

    freq_col = (ROPE_THETA ** (-jnp.arange(HALF_DIM, dtype=F32) / HALF_DIM))[:, None]
    (qn_t, sq_t, ksel, kwin, kswa, vsel, vwin, vswa, ng_t, kcv, gab) = _proj(
        h, w['norm_mix'][None], bf(w_t), bf(w_n), positions.reshape(1, n), freq_col)

    pos_c = jnp.pad(positions[:, CMP_LEN - 1::CMP_STRIDE], ((0, 0), (0, 1)))[:, None, :]
    w1 = lambda a: bf(a.reshape(CMP_LEN, HEAD_DIM, a.shape[-1]))
    kc, vc_t = _compress(kcv, w['cmp_pe_k'], w['cmp_pe_v'], w1(w['cmp_k_w1']), w1(w['cmp_v_w1']),
                         bf(w['cmp_k_w2'].T), bf(w['cmp_v_w2'].T), pos_c, freq_col, b, g_kv, s_len)

    gates = ng_t.reshape(3, g_kv, NSA_REP, n)
    ov = jnp.asarray(_overlap_matrix(n_sel, nc - 1, nc), BF16)
    mb, oc_t = _cmp_topk(qn_t, kc, vc_t, ov, gates, b, g_kv, s_len, n_top)

    expand = jnp.asarray(np.arange(s_len)[:, None] // SEL_LEN == np.arange(n_sel)[None, :], BF16)
    o_a = _nsa(qn_t, mb, oc_t, gates, ksel, expand, vsel, kwin, vwin, b, g_kv, s_len)

    sink_row = jnp.repeat(w['swa_sinks'].astype(F32) * LOG2E, SWA_TILE)[None]
    o_b = _swa(sq_t, sink_row, kswa, vswa, b, s_len)

    return _merge(h, o_a, o_b, gab, bf(w['w_branch_a']), bf(w['w_branch_b']), bf(w['w_out']))


def kernel(x, positions, norm_ffn1, ffn1_gate, ffn1_up, ffn1_down, norm_mix, w_in, cmp_pe_k, cmp_k_w1, cmp_k_w2, cmp_pe_v, cmp_v_w1, cmp_v_w2, swa_sinks, w_branch_a, w_branch_b, w_out, norm_ffn2, ffn2_gate, ffn2_up, ffn2_down, norm_final):
    b, s_len, d = x.shape
    stacked = dict(norm_ffn1=norm_ffn1, ffn1_gate=ffn1_gate, ffn1_up=ffn1_up, ffn1_down=ffn1_down,
                   norm_mix=norm_mix, w_in=w_in, cmp_pe_k=cmp_pe_k, cmp_k_w1=cmp_k_w1, cmp_k_w2=cmp_k_w2,
                   cmp_pe_v=cmp_pe_v, cmp_v_w1=cmp_v_w1, cmp_v_w2=cmp_v_w2, swa_sinks=swa_sinks,
                   w_branch_a=w_branch_a, w_branch_b=w_branch_b, w_out=w_out)
    depth = norm_ffn1.shape[0]
    h = x.reshape(b * s_len, d)
    for i in range(depth):
        w = {k: v[i] for k, v in stacked.items()}
        h = _layer(h, positions, w, b, s_len)
        last = i == depth - 1
        h = _ffn(h, norm_ffn2[i][None], ffn2_gate[i].astype(BF16), ffn2_up[i].astype(BF16),
                 ffn2_down[i].astype(BF16), norm_final[None], last)
    return h.reshape(b, s_len, d)
```

```python
import functools

import numpy as np
import jax
import jax.numpy as jnp
from jax import lax
from jax.experimental import pallas as pl
from jax.experimental.pallas import tpu as pltpu

HEAD_DIM = 64
HALF_DIM = HEAD_DIM // 2
NSA_HEADS = 8
NSA_KV = 2
NSA_REP = NSA_HEADS // NSA_KV
SWA_HEADS = 8
CMP_STRIDE = 16
CMP_LEN = 2 * CMP_STRIDE
SEL_LEN = 64
SEL_SHIFT = 6
SEL_TOPN = 16
SEL_LOCAL = 2
NSA_WINDOW = 512
SWA_WINDOW = 128
ROPE_THETA = 10000.0
RMS_EPS = 1e-6
FFN_HALF = 0.5
NEG_INF = -1e30
FORCE = 1e9
LOG2E = 1.4426950408889634
Q_SCALE = HEAD_DIM ** -0.5 * LOG2E
BF16_ROWS = 16
VAL_ROWS = HEAD_DIM + BF16_ROWS
LANES = 128

NSA_Q_W = NSA_HEADS * HEAD_DIM
NSA_KV_W = NSA_KV * HEAD_DIM
NSA_GATE_W = 3 * NSA_HEADS
SWA_Q_W = SWA_HEADS * HEAD_DIM
SWA_KV_W = HEAD_DIM

VMEM_LIMIT_BYTES = 56 * 1024 * 1024

BF16 = jnp.bfloat16
F32 = jnp.float32

ROW_TILE = 512
CMP_TILE = 1024
CMP_SUBTILE = 256
NSA_TILE = 256
SWA_TILE = 256
SWA_VTILE = 128


def _params(n_axes):
    return pltpu.CompilerParams(dimension_semantics=("arbitrary",) * n_axes,
                                vmem_limit_bytes=VMEM_LIMIT_BYTES)


def _resident(shape):
    zeros = (0,) * len(shape)
    return pl.BlockSpec(shape, lambda *_: zeros, pipeline_mode=pl.Buffered(1))


def _rms(x, g):
    y = x * lax.rsqrt(jnp.mean(x * x, axis=-1, keepdims=True) + RMS_EPS)
    return y * g


def _dot(a, b):
    return jnp.dot(a, b, preferred_element_type=F32)


def _dot_nt(a, b):
    return lax.dot_general(a, b, (((1,), (1,)), ((), ())), preferred_element_type=F32)


def _rope_angles(pos_row, freq_col):
    ang = pos_row.astype(F32) * freq_col
    return jnp.cos(ang), jnp.sin(ang)


def _rope_rows(block, cos_t, sin_t):
    out = []
    for hd in range(block.shape[0] // HEAD_DIM):
        x1 = block[hd * HEAD_DIM:hd * HEAD_DIM + HALF_DIM]
        x2 = block[hd * HEAD_DIM + HALF_DIM:(hd + 1) * HEAD_DIM]
        out += [x1 * cos_t - x2 * sin_t, x2 * cos_t + x1 * sin_t]
    return out


def _ones_rows(tk):
    return jnp.where(lax.broadcasted_iota(jnp.int32, (BF16_ROWS, tk), 0) == 0, 1.0, 0.0).astype(BF16)


def _ffn_kernel(x_ref, g_ref, wg_ref, wu_ref, wd_ref, gf_ref, o_ref, *, final_norm):
    x = x_ref[...]
    xb = _rms(x, g_ref[...]).astype(BF16)
    a = _dot(xb, wg_ref[...])
    b = _dot(xb, wu_ref[...])
    t = (a * jax.nn.sigmoid(a)) * b
    h = x + FFN_HALF * _dot(t.astype(BF16), wd_ref[...])
    if final_norm:
        h = _rms(h, gf_ref[...])
    o_ref[...] = h


def _ffn(x2, g, wg, wu, wd, gf, final_norm):
    n, d = x2.shape
    f = wg.shape[1]
    tm = ROW_TILE
    row = pl.BlockSpec((tm, d), lambda i: (i, 0))
    return pl.pallas_call(
        functools.partial(_ffn_kernel, final_norm=final_norm),
        grid=(n // tm,),
        in_specs=[row, _resident((1, d)), _resident((d, f)), _resident((d, f)), _resident((f, d)),
                  _resident((1, d))],
        out_specs=row,
        out_shape=jax.ShapeDtypeStruct((n, d), F32),
        compiler_params=_params(1),
        name="ffn_final" if final_norm else "ffn",
    )(x2, g, wg, wu, wd, gf)


PROJ_ROWS = (('qn', NSA_Q_W), ('sq', SWA_Q_W), ('ksel', NSA_KV_W), ('kwin', NSA_KV_W), ('kswa', LANES),
             ('vsel', NSA_KV_W), ('vwin', NSA_KV_W), ('vswa', SWA_KV_W), ('ng', 32),
             ('kc', NSA_KV_W), ('vc', NSA_KV_W))


def _proj_kernel(h_ref, g_ref, wt_ref, wn_ref, pos_ref, freq_ref,
                 qn_ref, sq_ref, ksel_ref, kwin_ref, kswa_ref, vsel_ref, vwin_ref, vswa_ref,
                 ng_ref, kcv_ref, gab_ref):
    ub = _rms(h_ref[...], g_ref[...]).astype(BF16)
    yt = _dot_nt(wt_ref[...], ub)
    rows, o = {}, 0
    for name, width in PROJ_ROWS:
        rows[name] = yt[o:o + width]
        o += width
    cos_t, sin_t = _rope_angles(pos_ref[...], freq_ref[...])

    for name, ref in (('qn', qn_ref), ('sq', sq_ref)):
        for i, piece in enumerate(_rope_rows(rows[name], cos_t, sin_t)):
            ref[i * HALF_DIM:(i + 1) * HALF_DIM, :] = (piece * Q_SCALE).astype(BF16)

    for name, ref in (('ksel', ksel_ref), ('kwin', kwin_ref)):
        k_nat = jnp.concatenate(_rope_rows(rows[name], cos_t, sin_t), axis=0).T
        for g in range(NSA_KV):
            ref[g] = k_nat[:, g * HEAD_DIM:(g + 1) * HEAD_DIM].astype(BF16)
    kswa = _rope_rows(rows['kswa'][:HEAD_DIM], cos_t, sin_t) + [rows['kswa'][HEAD_DIM:]]
    kswa_ref[...] = jnp.concatenate(kswa, axis=0).T[:, :HEAD_DIM].astype(BF16)

    for name, ref, groups in (('vsel', vsel_ref, NSA_KV), ('vwin', vwin_ref, NSA_KV), ('vswa', vswa_ref, 1)):
        tk = ref.shape[2]
        for j in range(ref.shape[0]):
            for g in range(groups):
                ref[j, g * VAL_ROWS:g * VAL_ROWS + HEAD_DIM, :] = (
                    rows[name][g * HEAD_DIM:(g + 1) * HEAD_DIM, j * tk:(j + 1) * tk].astype(BF16))
                ref[j, g * VAL_ROWS + HEAD_DIM:(g + 1) * VAL_ROWS, :] = _ones_rows(tk)

    ng_ref[...] = jax.nn.sigmoid(rows['ng'][:NSA_GATE_W])

    for i, name in enumerate(('kc', 'vc')):
        nat = rows[name].T
        for g in range(NSA_KV):
            kcv_ref[i * NSA_KV + g] = nat[:, g * HEAD_DIM:(g + 1) * HEAD_DIM]

    gab_ref[...] = jax.nn.sigmoid(_dot(ub, wn_ref[...]))


def _proj(h2, g, wt, wn, pos_row, freq_col):
    n, d = h2.shape
    tm = ROW_TILE
    rows = lambda w: pl.BlockSpec((tm, w), lambda i: (i, 0))
    cols = lambda w: pl.BlockSpec((w, tm), lambda i: (0, i))
    grouped = lambda k: pl.BlockSpec((k, tm, HEAD_DIM), lambda i: (0, i, 0))
    tiles = lambda groups, tk: pl.BlockSpec((tm // tk, groups * VAL_ROWS, tk), lambda i: (i, 0, 0))
    val_shape = lambda groups, tk: jax.ShapeDtypeStruct((n // tk, groups * VAL_ROWS, tk), BF16)
    out_shape = [
        jax.ShapeDtypeStruct((NSA_Q_W, n), BF16), jax.ShapeDtypeStruct((SWA_Q_W, n), BF16),
        jax.ShapeDtypeStruct((NSA_KV, n, HEAD_DIM), BF16), jax.ShapeDtypeStruct((NSA_KV, n, HEAD_DIM), BF16),
        jax.ShapeDtypeStruct((n, HEAD_DIM), BF16),
        val_shape(NSA_KV, NSA_TILE), val_shape(NSA_KV, NSA_TILE), val_shape(1, SWA_VTILE),
        jax.ShapeDtypeStruct((NSA_GATE_W, n), F32), jax.ShapeDtypeStruct((2 * NSA_KV, n, HEAD_DIM), F32),
        jax.ShapeDtypeStruct((n, 2 * d), F32),
    ]
    return pl.pallas_call(
        _proj_kernel,
        grid=(n // tm,),
        in_specs=[rows(d), _resident((1, d)), _resident(wt.shape), _resident(wn.shape),
                  cols(1), _resident(freq_col.shape)],
        out_specs=[cols(NSA_Q_W), cols(SWA_Q_W), grouped(NSA_KV), grouped(NSA_KV), rows(HEAD_DIM),
                   tiles(NSA_KV, NSA_TILE), tiles(NSA_KV, NSA_TILE), tiles(1, SWA_VTILE),
                   cols(NSA_GATE_W), grouped(2 * NSA_KV), rows(2 * d)],
        out_shape=out_shape,
        compiler_params=_params(1),
        name="proj",
    )(h2, g, wt, wn, pos_row, freq_col)


def _gelu(x):
    return jax.nn.gelu(x, approximate=True)


def _compress_kernel(kc_ref, vc_ref, pek_ref, pev_ref, w1k_ref, w1v_ref, w2kt_ref, w2vt_ref,
                     pos_ref, freq_ref, kc_out, vct_out):
    nc = kc_out.shape[1]

    def hidden(x_ref, pe_ref, w1_ref):
        top = bot = None
        for j in range(CMP_STRIDE):
            x = x_ref[0, pl.ds(j, nc, stride=CMP_STRIDE), :]
            t = _dot((x + pe_ref[j:j + 1, :]).astype(BF16), w1_ref[j])
            b = _dot((x + pe_ref[CMP_STRIDE + j:CMP_STRIDE + j + 1, :]).astype(BF16), w1_ref[CMP_STRIDE + j])
            top, bot = (t, b) if top is None else (top + t, bot + b)
        return _gelu(top + pltpu.roll(bot, shift=nc - 1, axis=0)).astype(BF16)

    kt = _dot_nt(w2kt_ref[...], hidden(kc_ref, pek_ref, w1k_ref))
    cos_t, sin_t = _rope_angles(pos_ref[0], freq_ref[...])
    kt = jnp.concatenate(_rope_rows(kt, cos_t, sin_t) + [jnp.zeros((LANES - HEAD_DIM, nc), F32)], axis=0)
    kc_out[0] = kt.T[:, :HEAD_DIM].astype(BF16)
    vct_out[0] = _dot_nt(w2vt_ref[...], hidden(vc_ref, pev_ref, w1v_ref)).astype(BF16)


def _compress(kcv, pek, pev, w1k, w1v, w2kt, w2vt, pos_c, freq_col, b, g_kv, s_len):
    nc = s_len // CMP_STRIDE
    hid = w1k.shape[-1]
    kcv = kcv.reshape(2 * g_kv * b, s_len, HEAD_DIM)
    src = lambda kind: pl.BlockSpec((1, s_len, HEAD_DIM),
                                    lambda i: ((kind * g_kv + i % g_kv) * b + i // g_kv, 0, 0))
    per = lambda *s: pl.BlockSpec((1,) + s, lambda i: (i,) + (0,) * len(s))
    return pl.pallas_call(
        _compress_kernel,
        grid=(b * g_kv,),
        in_specs=[src(0), src(1), _resident(pek.shape), _resident(pev.shape),
                  _resident(w1k.shape), _resident(w1v.shape), _resident((HEAD_DIM, hid)),
                  _resident((HEAD_DIM, hid)),
                  pl.BlockSpec((1, 1, nc), lambda i: (i // g_kv, 0, 0)), _resident(freq_col.shape)],
        out_specs=[per(nc, HEAD_DIM), per(HEAD_DIM, nc)],
        out_shape=[jax.ShapeDtypeStruct((b * g_kv, nc, HEAD_DIM), BF16),
                   jax.ShapeDtypeStruct((b * g_kv, HEAD_DIM, nc), BF16)],
        compiler_params=_params(1),
        name="compress",
    )(kcv, kcv, pek, pev, w1k, w1v, w2kt, w2vt, pos_c, freq_col)


def _cmp_topk_body(q_ref, kc_ref, vct_ref, ov_ref, gate_ref, mb_ref, oc_ref,
                   s_buf, p_buf, bias_buf, psum_buf, x_buf, *, n_top, nc, n_sel):
    tq = q_ref.shape[1]
    sub = s_buf.shape[1]
    rep = NSA_REP
    qi = pl.program_id(2)
    q0 = qi * tq
    stages = [(u, r) for u in range(tq // sub) for r in range(rep)]
    q_cols = lambda u, r: q_ref[r * HEAD_DIM:(r + 1) * HEAD_DIM, u * sub:(u + 1) * sub]
    keys = kc_ref[0, 0:nc, :]
    vals = vct_ref[0, :, 0:nc]

    s_buf[0:nc] = _dot(keys, q_cols(*stages[0]))
    inv_prev = None
    for n, (u, r) in enumerate(stages):
        cols = slice(u * sub, (u + 1) * sub)
        if n + 1 < len(stages):
            s_next = _dot(keys, q_cols(*stages[n + 1]))
        if n > 0:
            pu, pr = stages[n - 1]
            oc_ref[pr * HEAD_DIM:(pr + 1) * HEAD_DIM, pu * sub:(pu + 1) * sub] = (
                _dot(vals, p_buf[0:nc]) * (inv_prev * gate_ref[0, 0, pr:pr + 1, pu * sub:(pu + 1) * sub]))
        last = ((q0 + u * sub + lax.broadcasted_iota(jnp.int32, (1, sub), 1)) - (CMP_LEN - 1)) >> 4
        if r == 0:
            bias_buf[0:nc] = jnp.where(lax.broadcasted_iota(jnp.int32, (nc, sub), 0) <= last, 0.0, NEG_INF)
        s = s_buf[0:nc] + bias_buf[0:nc]
        m = jnp.max(s, axis=0, keepdims=True)
        e = jnp.exp2(s - m)
        inv_prev = jnp.where(last >= 0, 1.0 / jnp.sum(e, axis=0, keepdims=True), 0.0)
        p_buf[0:nc] = e.astype(BF16)
        if r == 0:
            psum_buf[0:nc, cols] = e * inv_prev
        else:
            psum_buf[0:nc, cols] += e * inv_prev
        if n + 1 < len(stages):
            s_buf[0:nc] = s_next
    pu, pr = stages[-1]
    oc_ref[pr * HEAD_DIM:(pr + 1) * HEAD_DIM, pu * sub:(pu + 1) * sub] = (
        _dot(vals, p_buf[0:nc]) * (inv_prev * gate_ref[0, 0, pr:pr + 1, pu * sub:(pu + 1) * sub]))

    psum = psum_buf[0:nc, :]
    p_hi = psum.astype(BF16)
    p_lo = (psum - p_hi.astype(F32)).astype(BF16)
    pool = ov_ref[0:n_sel, 0:nc]
    imp = _dot(pool, p_hi) + _dot(pool, p_lo)

    blk = lax.broadcasted_iota(jnp.int32, (n_sel, tq), 0)
    tb = (q0 + lax.broadcasted_iota(jnp.int32, (n_sel, tq), 1)) >> SEL_SHIFT
    forced = (blk == 0) | ((tb - blk >= 0) & (tb - blk < SEL_LOCAL))
    n_forced = 1 + SEL_LOCAL
    premark = forced & (jnp.full((n_sel, tq), qi, jnp.int32) > 0)
    x_buf[0:n_sel] = jnp.where(premark, -jnp.inf, jnp.where(forced, FORCE, jnp.where(blk > tb, -FORCE, imp)))

    def pick(_, carry):
        x = x_buf[0:n_sel]
        top = jnp.max(x, axis=0, keepdims=True)
        first = jnp.min(jnp.where(x == top, blk, n_sel), axis=0, keepdims=True)
        x_buf[0:n_sel] = jnp.where(blk == first, -jnp.inf, x)
        return carry

    lax.fori_loop(0, jnp.where(qi > 0, n_top - n_forced, n_top), pick, 0)
    mb_ref[0, 0:n_sel, :] = jnp.where(x_buf[0:n_sel] == -jnp.inf, 0.0, NEG_INF).astype(BF16)
    if n_sel < mb_ref.shape[1]:
        mb_ref[0, n_sel:, :] = jnp.full((mb_ref.shape[1] - n_sel, tq), NEG_INF, BF16)


def _cmp_topk_kernel(*refs, n_top, n_tiles):
    nc = refs[1].shape[1]
    n_sel = refs[3].shape[0]
    qi = pl.program_id(2)
    for v in range(max(n_tiles // 2, 1)):
        frac = lambda total: min(total, (2 * v + 2) * total // n_tiles)
        pl.when((qi >> 1) == v)(functools.partial(_cmp_topk_body, *refs, n_top=n_top, nc=frac(nc), n_sel=frac(n_sel)))


def _cmp_topk(qn_t, kc, vct, ov, gates, b, g_kv, s_len, n_top):
    tq = min(CMP_TILE, s_len)
    sub = min(CMP_SUBTILE, tq)
    nq = s_len // tq
    nc = kc.shape[1]
    n_sel = ov.shape[0]
    rep = NSA_REP
    return pl.pallas_call(
        functools.partial(_cmp_topk_kernel, n_top=n_top, n_tiles=nq),
        grid=(b, g_kv, nq),
        in_specs=[
            pl.BlockSpec((rep * HEAD_DIM, tq), lambda bi, gi, qi: (gi, bi * nq + qi)),
            pl.BlockSpec((1, nc, HEAD_DIM), lambda bi, gi, qi: (bi * g_kv + gi, 0, 0)),
            pl.BlockSpec((1, HEAD_DIM, nc), lambda bi, gi, qi: (bi * g_kv + gi, 0, 0)),
            _resident(ov.shape),
            pl.BlockSpec((1, 1, rep, tq), lambda bi, gi, qi: (0, gi, 0, bi * nq + qi)),
        ],
        out_specs=[
            pl.BlockSpec((1, n_sel, tq), lambda bi, gi, qi: (bi * g_kv + gi, 0, qi)),
            pl.BlockSpec((rep * HEAD_DIM, tq), lambda bi, gi, qi: (gi, bi * nq + qi)),
        ],
        out_shape=[jax.ShapeDtypeStruct((b * g_kv, n_sel, s_len), BF16),
                   jax.ShapeDtypeStruct((NSA_Q_W, b * s_len), F32)],
        scratch_shapes=[pltpu.VMEM((nc, sub), F32), pltpu.VMEM((nc, sub), BF16), pltpu.VMEM((nc, sub), F32),
                        pltpu.VMEM((nc, tq), F32), pltpu.VMEM((n_sel, tq), F32)],
        compiler_params=_params(3),
        name="cmp_topk",
    )(qn_t, kc, vct, ov, gates)


def _softmax_tile(m, s):
    m_new = jnp.maximum(m, jnp.max(s, axis=0, keepdims=True))
    return m_new, jnp.exp2(m - m_new), jnp.exp2(s - m_new).astype(BF16)


def _normalize(acc):
    return acc[:HEAD_DIM] / acc[HEAD_DIM:HEAD_DIM + 1]


def _rel_pos(tk, tq):
    return lax.broadcasted_iota(jnp.int32, (tk, tq), 1) - lax.broadcasted_iota(jnp.int32, (tk, tq), 0)


def _window_bias(rel, back, tk, kj, window):
    dist = rel + (back * tk + jnp.where(kj >= 0, 0, window))
    return jnp.where((dist >= 0) & (dist < window), 0.0, NEG_INF)


def _pipe_stage(s_buf, p_buf, heads, cur=None, nxt=None, prv=None):
    if nxt is not None:
        s_next = [_dot(nxt[0], nxt[1][r][...]) for r in heads]
    if prv is not None:
        pv = [_dot(prv[0], p_buf[r]) for r in heads]
    out = None
    if cur is not None:
        m, bias = cur
        out = []
        for r in heads:
            s = s_buf[r] if bias is None else s_buf[r] + bias
            m_r, a_r, p_buf[r] = _softmax_tile(m[r], s)
            out.append((m_r, a_r))
        out = tuple(zip(*out))
    for r in heads:
        if prv is not None:
            prv[1][r] = prv[2][r] * prv[1][r] + pv[r]
        if nxt is not None:
            s_buf[r] = s_next[r]
    return out


def _nsa_kernel(q_ref, mb_ref, oc_ref, gate_ref, ks_ref, ex_ref, vs_ref, kw_ref, vw_ref, o_ref,
                qa_buf, s_buf, p_buf, sel_acc, win_acc):
    tq = q_ref.shape[1]
    tk = tq
    n_sel = mb_ref.shape[1]
    rep = NSA_REP
    heads = range(rep)
    qi = pl.program_id(2)
    stage = functools.partial(_pipe_stage, s_buf, p_buf, heads)
    key_rows = lambda j: pl.ds(pl.multiple_of(j * tk, tk), tk)

    def sel_keys(j):
        return jnp.concatenate([ex_ref[key_rows(j), :], ks_ref[0, key_rows(j), :]], axis=1)

    q_heads = [q_ref.at[pl.ds(r * HEAD_DIM, HEAD_DIM), :] for r in heads]
    q_aug = [qa_buf.at[r] for r in heads]
    for r in heads:
        qa_buf[r, 0:n_sel, :] = mb_ref[0]
        qa_buf[r, n_sel:n_sel + HEAD_DIM, :] = q_heads[r][...]
        s_buf[r] = _dot(sel_keys(0), qa_buf[r])
        p_buf[r] = jnp.zeros((tk, tq), BF16)
        sel_acc[r] = jnp.zeros((VAL_ROWS, tq), F32)
        win_acc[r] = jnp.zeros((VAL_ROWS, tq), F32)
    row = lambda v: tuple(jnp.full((1, tq), v, F32) for _ in heads)

    def sel_stage(i, carry):
        m, alpha = carry
        return stage(cur=(m, None), nxt=(sel_keys(i + 1), q_aug),
                     prv=(vs_ref[jnp.maximum(i - 1, 0)], sel_acc, alpha))

    carry = lax.fori_loop(0, qi >> 1, lambda j, c: sel_stage(2 * j + 1, sel_stage(2 * j, c)),
                          (row(NEG_INF), row(1.0)))
    m, alpha = lax.fori_loop(0, qi & 1, lambda _, c: sel_stage(qi - 1, c), carry)

    rel = _rel_pos(tk, tq)
    backs = list(range(NSA_WINDOW // tk, -1, -1))
    win_tile = [jnp.maximum(qi - back, 0) for back in backs]
    win_bias = [_window_bias(rel, back, tk, qi - back, NSA_WINDOW) for back in backs]
    _, alpha_d = stage(cur=(m, jnp.where(rel >= 0, 0.0, NEG_INF)),
                       nxt=(kw_ref[0, key_rows(win_tile[0]), :], q_heads),
                       prv=(vs_ref[jnp.maximum(qi - 1, 0)], sel_acc, alpha))
    prv = (vs_ref[qi], sel_acc, alpha_d)
    m_w = row(NEG_INF)
    for n in range(len(backs)):
        nxt = (kw_ref[0, key_rows(win_tile[n + 1]), :], q_heads) if n + 1 < len(backs) else None
        m_w, alpha_w = stage(cur=(m_w, win_bias[n]), nxt=nxt, prv=prv)
        prv = (vw_ref[win_tile[n]], win_acc, alpha_w)
    stage(prv=prv)

    outs = [oc_ref[r * HEAD_DIM:(r + 1) * HEAD_DIM, :]
            + gate_ref[1, 0, r:r + 1, :] * _normalize(sel_acc[r])
            + gate_ref[2, 0, r:r + 1, :] * _normalize(win_acc[r]) for r in heads]
    o_ref[...] = jnp.concatenate(outs, axis=0).T.astype(o_ref.dtype)


def _nsa(qn_t, mb, oc_t, gates, ksel, expand, vsel, kwin, vwin, b, g_kv, s_len):
    tq = NSA_TILE
    nq = s_len // tq
    rep = NSA_REP
    n_sel = mb.shape[1]
    keys = pl.BlockSpec((1, s_len, HEAD_DIM), lambda bi, gi, qi: (gi * b + bi, 0, 0))
    vals = pl.BlockSpec((nq, VAL_ROWS, tq), lambda bi, gi, qi: (bi, gi, 0))
    return pl.pallas_call(
        _nsa_kernel,
        grid=(b, g_kv, nq),
        in_specs=[
            pl.BlockSpec((rep * HEAD_DIM, tq), lambda bi, gi, qi: (gi, bi * nq + qi)),
            pl.BlockSpec((1, n_sel, tq), lambda bi, gi, qi: (bi * g_kv + gi, 0, qi)),
            pl.BlockSpec((rep * HEAD_DIM, tq), lambda bi, gi, qi: (gi, bi * nq + qi)),
            pl.BlockSpec((3, 1, rep, tq), lambda bi, gi, qi: (0, gi, 0, bi * nq + qi)),
            keys, _resident(expand.shape), vals, keys, vals,
        ],
        out_specs=pl.BlockSpec((tq, rep * HEAD_DIM), lambda bi, gi, qi: (bi * nq + qi, gi)),
        out_shape=jax.ShapeDtypeStruct((b * s_len, NSA_Q_W), BF16),
        scratch_shapes=[pltpu.VMEM((rep, n_sel + HEAD_DIM, tq), BF16), pltpu.VMEM((rep, tq, tq), F32),
                        pltpu.VMEM((rep, tq, tq), BF16), pltpu.VMEM((rep, VAL_ROWS, tq), F32),
                        pltpu.VMEM((rep, VAL_ROWS, tq), F32)],
        compiler_params=_params(3),
        name="nsa",
    )(qn_t, mb, oc_t, gates, ksel.reshape(g_kv * b, s_len, HEAD_DIM), expand, vsel,
      kwin.reshape(g_kv * b, s_len, HEAD_DIM), vwin)


def _swa_kernel(q_ref, sink_ref, k_ref, v_ref, o_ref, s_buf, p_buf, bias_buf, o_buf):
    tq = q_ref.shape[1]
    tv = v_ref.shape[2]
    nk = s_buf.shape[0]
    heads = SWA_HEADS
    qi = pl.program_id(1)
    q0 = qi * tq
    k0 = jnp.maximum(q0 - SWA_WINDOW, 0)

    keys = k_ref[0, pl.ds(pl.multiple_of(k0, tv), nk), :]
    vals = jnp.concatenate([v_ref[k0 // tv + j] for j in range(nk // tv)], axis=1)
    dist = _rel_pos(nk, tq) + (q0 - k0)
    bias_buf[...] = jnp.where((dist >= 0) & (dist < SWA_WINDOW), 0.0, NEG_INF)
    q_head = lambda h: q_ref[h * HEAD_DIM:(h + 1) * HEAD_DIM, :]

    def finish(h, m_all, sink):
        acc = _dot(vals, p_buf[...])
        o_buf[h * HEAD_DIM:(h + 1) * HEAD_DIM, :] = (
            acc[:HEAD_DIM] / (acc[HEAD_DIM:HEAD_DIM + 1] + jnp.exp2(sink - m_all)))

    s_buf[...] = _dot(keys, q_head(0))
    prev = None
    for h in range(heads):
        if h + 1 < heads:
            s_next = _dot(keys, q_head(h + 1))
        if prev is not None:
            finish(*prev)
        sink = sink_ref[:, h * tq:(h + 1) * tq]
        s = s_buf[...] + bias_buf[...]
        m_all = jnp.maximum(jnp.max(s, axis=0, keepdims=True), sink)
        p_buf[...] = jnp.exp2(s - m_all).astype(BF16)
        prev = (h, m_all, sink)
        if h + 1 < heads:
            s_buf[...] = s_next
    finish(*prev)
    o_ref[...] = o_buf[...].T.astype(o_ref.dtype)


def _swa(sq_t, sink_row, k, v, b, s_len):
    tq = min(SWA_TILE, s_len)
    nq = s_len // tq
    tv = v.shape[2]
    nk = tq + SWA_WINDOW
    return pl.pallas_call(
        _swa_kernel,
        grid=(b, nq),
        in_specs=[pl.BlockSpec((SWA_Q_W, tq), lambda bi, qi: (0, bi * nq + qi)),
                  _resident(sink_row.shape),
                  pl.BlockSpec((1, s_len, HEAD_DIM), lambda bi, qi: (bi, 0, 0)),
                  pl.BlockSpec((s_len // tv, VAL_ROWS, tv), lambda bi, qi: (bi, 0, 0))],
        out_specs=pl.BlockSpec((tq, SWA_Q_W), lambda bi, qi: (bi * nq + qi, 0)),
        out_shape=jax.ShapeDtypeStruct((b * s_len, SWA_Q_W), BF16),
        scratch_shapes=[pltpu.VMEM((nk, tq), F32), pltpu.VMEM((nk, tq), BF16), pltpu.VMEM((nk, tq), F32),
                        pltpu.VMEM((SWA_Q_W, tq), F32)],
        compiler_params=_params(2),
        name="swa",
    )(sq_t, sink_row, k.reshape(b, s_len, HEAD_DIM), v)


def _merge_kernel(h_ref, oa_ref, ob_ref, gab_ref, wa_ref, wb_ref, wo_ref, o_ref):
    d = h_ref.shape[1]
    gab = gab_ref[...]
    merged = gab[:, :d] * _dot(oa_ref[...], wa_ref[...]) + gab[:, d:] * _dot(ob_ref[...], wb_ref[...])
    o_ref[...] = h_ref[...] + _dot(merged.astype(BF16), wo_ref[...])


def _merge(h2, oa, ob, gab, wa, wb, wo):
    n, d = h2.shape
    tm = ROW_TILE
    rows = lambda w: pl.BlockSpec((tm, w), lambda i: (i, 0))
    return pl.pallas_call(
        _merge_kernel,
        grid=(n // tm,),
        in_specs=[rows(d), rows(NSA_Q_W), rows(SWA_Q_W), rows(2 * d),
                  _resident(wa.shape), _resident(wb.shape), _resident(wo.shape)],
        out_specs=rows(d),
        out_shape=jax.ShapeDtypeStruct((n, d), F32),
        compiler_params=_params(1),
        name="merge",
    )(h2, oa, ob, gab, wa, wb, wo)


def _overlap_matrix(n_sel, n_cmp, n_cmp_pad):
    cs = np.arange(n_cmp) * CMP_STRIDE
    ss = np.arange(n_sel) * SEL_LEN
    ov = np.clip(np.minimum(cs[None, :] + CMP_LEN, ss[:, None] + SEL_LEN)
                 - np.maximum(cs[None, :], ss[:, None]), 0, None).astype(np.float32) / CMP_LEN
    return np.pad(ov, ((0, 0), (0, n_cmp_pad - n_cmp)))


def _layer(h, positions, w, b, s_len):
    n, d = h.shape
    g_kv = NSA_KV
    n_sel = s_len // SEL_LEN
    n_top = min(SEL_TOPN, n_sel)
    nc = s_len // CMP_STRIDE
    bf = lambda a: a.astype(BF16)

    h = _ffn(h, w['norm_ffn1'][None], bf(w['ffn1_gate']), bf(w['ffn1_up']), bf(w['ffn1_down']),
             w['norm_ffn1'][None], False)

    pts = np.cumsum((NSA_Q_W,) + (NSA_KV_W,) * 6 + (NSA_GATE_W, SWA_Q_W, SWA_KV_W, SWA_KV_W, d, d))[:-1]
    (w_nq, w_kc, w_vc, w_ksl, w_vsl, w_kwn, w_vwn, w_ng, w_sq, w_sk, w_sv, w_ga, w_gb) = jnp.split(
        w['w_in'], pts, axis=1)
    parts = dict(qn=w_nq, sq=w_sq, ksel=w_ksl, kwin=w_kwn, kswa=w_sk, vsel=w_vsl, vwin=w_vwn, vswa=w_sv,
                 ng=w_ng, kc=w_kc, vc=w_vc)
    w_t = jnp.concatenate([jnp.pad(parts[name], ((0, 0), (0, width - parts[name].shape[1])))
                           for name, width in PROJ_ROWS], axis=1).T
    w_n = jnp.concatenate([w_ga, w_gb], axis=1)

    freq_col = (ROPE_THETA ** (-jnp.arange(HALF_DIM, dtype=F32) / HALF_DIM))[:, None]
    (qn_t, sq_t, ksel, kwin, kswa, vsel, vwin, vswa, ng_t, kcv, gab) = _proj(
        h, w['norm_mix'][None], bf(w_t), bf(w_n), positions.reshape(1, n), freq_col)

    pos_c = jnp.pad(positions[:, CMP_LEN - 1::CMP_STRIDE], ((0, 0), (0, 1)))[:, None, :]
    w1 = lambda a: bf(a.reshape(CMP_LEN, HEAD_DIM, a.shape[-1]))
    kc, vc_t = _compress(kcv, w['cmp_pe_k'], w['cmp_pe_v'], w1(w['cmp_k_w1']), w1(w['cmp_v_w1']),
                         bf(w['cmp_k_w2'].T), bf(w['cmp_v_w2'].T), pos_c, freq_col, b, g_kv, s_len)

    gates = ng_t.reshape(3, g_kv, NSA_REP, n)
    ov = jnp.asarray(_overlap_matrix(n_sel, nc - 1, nc), BF16)
    mb, oc_t = _cmp_topk(qn_t, kc, vc_t, ov, gates, b, g_kv, s_len, n_top)

    expand = jnp.asarray(np.arange(s_len)[:, None] // SEL_LEN == np.arange(n_sel)[None, :], BF16)
    o_a = _nsa(qn_t, mb, oc_t, gates, ksel, expand, vsel, kwin, vwin, b, g_kv, s_len)

    sink_row = jnp.repeat(w['swa_sinks'].astype(F32) * LOG2E, min(SWA_TILE, s_len))[None]
    o_b = _swa(sq_t, sink_row, kswa, vswa, b, s_len)

    return _merge(h, o_a, o_b, gab, bf(w['w_branch_a']), bf(w['w_branch_b']), bf(w['w_out']))


def kernel(x, positions, norm_ffn1, ffn1_gate, ffn1_up, ffn1_down, norm_mix, w_in, cmp_pe_k, cmp_k_w1, cmp_k_w2, cmp_pe_v, cmp_v_w1, cmp_v_w2, swa_sinks, w_branch_a, w_branch_b, w_out, norm_ffn2, ffn2_gate, ffn2_up, ffn2_down, norm_final):
    b, s_len, d = x.shape
    stacked = dict(norm_ffn1=norm_ffn1, ffn1_gate=ffn1_gate, ffn1_up=ffn1_up, ffn1_down=ffn1_down,
                   norm_mix=norm_mix, w_in=w_in, cmp_pe_k=cmp_pe_k, cmp_k_w1=cmp_k_w1, cmp_k_w2=cmp_k_w2,
                   cmp_pe_v=cmp_pe_v, cmp_v_w1=cmp_v_w1, cmp_v_w2=cmp_v_w2, swa_sinks=swa_sinks,
                   w_branch_a=w_branch_a, w_branch_b=w_branch_b, w_out=w_out)
    depth = norm_ffn1.shape[0]
    h = x.reshape(b * s_len, d)
    for i in range(depth):
        w = {k: v[i] for k, v in stacked.items()}
        h = _layer(h, positions, w, b, s_len)
        last = i == depth - 1
        h = _ffn(h, norm_ffn2[i][None], ffn2_gate[i].astype(BF16), ffn2_up[i].astype(BF16),
                 ffn2_down[i].astype(BF16), norm_final[None], last)
    return h.reshape(b, s_len, d)
```

```python
import functools

import numpy as np
import jax
import jax.numpy as jnp
from jax import lax
from jax.experimental import pallas as pl
from jax.experimental.pallas import tpu as pltpu

HEAD_DIM = 64
HALF_DIM = HEAD_DIM // 2
NSA_HEADS = 8
NSA_KV = 2
NSA_REP = NSA_HEADS // NSA_KV
SWA_HEADS = 8
CMP_STRIDE = 16
CMP_LEN = 2 * CMP_STRIDE
SEL_LEN = 64
SEL_SHIFT = 6
SEL_TOPN = 16
SEL_LOCAL = 2
NSA_WINDOW = 512
SWA_WINDOW = 128
ROPE_THETA = 10000.0
RMS_EPS = 1e-6
FFN_HALF = 0.5
NEG_INF = -1e30
FORCE = 1e9
LOG2E = 1.4426950408889634
Q_SCALE = HEAD_DIM ** -0.5 * LOG2E
BF16_ROWS = 16
VAL_ROWS = HEAD_DIM + BF16_ROWS
LANES = 128

NSA_Q_W = NSA_HEADS * HEAD_DIM
NSA_KV_W = NSA_KV * HEAD_DIM
NSA_GATE_W = 3 * NSA_HEADS
SWA_Q_W = SWA_HEADS * HEAD_DIM
SWA_KV_W = HEAD_DIM

VMEM_LIMIT_BYTES = 56 * 1024 * 1024

BF16 = jnp.bfloat16
F32 = jnp.float32

ROW_TILE = 512
CMP_TILE = 1024
CMP_SUBTILE = 256
NSA_TILE = 256
SWA_TILE = 256
SWA_VTILE = 128


def _params(n_axes):
    return pltpu.CompilerParams(dimension_semantics=("arbitrary",) * n_axes,
                                vmem_limit_bytes=VMEM_LIMIT_BYTES)


def _resident(shape):
    zeros = (0,) * len(shape)
    return pl.BlockSpec(shape, lambda *_: zeros, pipeline_mode=pl.Buffered(1))


def _rms(x, g):
    y = x * lax.rsqrt(jnp.mean(x * x, axis=-1, keepdims=True) + RMS_EPS)
    return y * g


def _dot(a, b):
    return jnp.dot(a, b, preferred_element_type=F32)


def _dot_nt(a, b):
    return lax.dot_general(a, b, (((1,), (1,)), ((), ())), preferred_element_type=F32)


def _rope_angles(pos_row, freq_col):
    ang = pos_row.astype(F32) * freq_col
    return jnp.cos(ang), jnp.sin(ang)


def _rope_rows(block, cos_t, sin_t):
    out = []
    for hd in range(block.shape[0] // HEAD_DIM):
        x1 = block[hd * HEAD_DIM:hd * HEAD_DIM + HALF_DIM]
        x2 = block[hd * HEAD_DIM + HALF_DIM:(hd + 1) * HEAD_DIM]
        out += [x1 * cos_t - x2 * sin_t, x2 * cos_t + x1 * sin_t]
    return out


def _ones_rows(tk):
    return jnp.where(lax.broadcasted_iota(jnp.int32, (BF16_ROWS, tk), 0) == 0, 1.0, 0.0).astype(BF16)


def _merge_branches(h_ref, oa_ref, ob_ref, gab_ref, wa_ref, wb_ref, wo_ref):
    d = h_ref.shape[1]
    gab = gab_ref[...]
    merged = gab[:, :d] * _dot(oa_ref[...], wa_ref[...]) + gab[:, d:] * _dot(ob_ref[...], wb_ref[...])
    return h_ref[...] + _dot(merged.astype(BF16), wo_ref[...])


def _ffn_kernel(*refs, final_norm, merge):
    x_ref, g_ref, wg_ref, wu_ref, wd_ref, gf_ref, o_ref = refs[-7:]
    x = _merge_branches(x_ref, *refs[:-7]) if merge else x_ref[...]
    xb = _rms(x, g_ref[...]).astype(BF16)
    a = _dot(xb, wg_ref[...])
    b = _dot(xb, wu_ref[...])
    t = (a * jax.nn.sigmoid(a)) * b
    h = x + FFN_HALF * _dot(t.astype(BF16), wd_ref[...])
    if final_norm:
        h = _rms(h, gf_ref[...])
    o_ref[...] = h


def _ffn(x2, g, wg, wu, wd, gf, final_norm, branches=None):
    n, d = x2.shape
    f = wg.shape[1]
    tm = ROW_TILE
    rows = lambda w: pl.BlockSpec((tm, w), lambda i: (i, 0))
    merge_specs, merge_args = [], ()
    if branches is not None:
        oa, ob, gab, wa, wb, wo = branches
        merge_specs = [rows(oa.shape[1]), rows(ob.shape[1]), rows(2 * d),
                       _resident(wa.shape), _resident(wb.shape), _resident(wo.shape)]
        merge_args = branches
    return pl.pallas_call(
        functools.partial(_ffn_kernel, final_norm=final_norm, merge=branches is not None),
        grid=(n // tm,),
        in_specs=merge_specs + [rows(d), _resident((1, d)), _resident((d, f)), _resident((d, f)),
                                _resident((f, d)), _resident((1, d))],
        out_specs=rows(d),
        out_shape=jax.ShapeDtypeStruct((n, d), F32),
        compiler_params=_params(1),
        name="ffn_final" if final_norm else "ffn",
    )(*merge_args, x2, g, wg, wu, wd, gf)


PROJ_ROWS = (('qn', NSA_Q_W), ('sq', SWA_Q_W), ('ksel', NSA_KV_W), ('kwin', NSA_KV_W), ('kswa', LANES),
             ('vsel', NSA_KV_W), ('vwin', NSA_KV_W), ('vswa', SWA_KV_W), ('ng', 32),
             ('kc', NSA_KV_W), ('vc', NSA_KV_W))


def _proj_kernel(h_ref, g_ref, wt_ref, wn_ref, pos_ref, freq_ref,
                 qn_ref, sq_ref, ksel_ref, kwin_ref, kswa_ref, vsel_ref, vwin_ref, vswa_ref,
                 ng_ref, kcv_ref, gab_ref):
    ub = _rms(h_ref[...], g_ref[...]).astype(BF16)
    yt = _dot_nt(wt_ref[...], ub)
    rows, o = {}, 0
    for name, width in PROJ_ROWS:
        rows[name] = yt[o:o + width]
        o += width
    cos_t, sin_t = _rope_angles(pos_ref[...], freq_ref[...])

    for name, ref in (('qn', qn_ref), ('sq', sq_ref)):
        for i, piece in enumerate(_rope_rows(rows[name], cos_t, sin_t)):
            ref[i * HALF_DIM:(i + 1) * HALF_DIM, :] = (piece * Q_SCALE).astype(BF16)

    for name, ref in (('ksel', ksel_ref), ('kwin', kwin_ref)):
        k_nat = jnp.concatenate(_rope_rows(rows[name], cos_t, sin_t), axis=0).T
        for g in range(NSA_KV):
            ref[g] = k_nat[:, g * HEAD_DIM:(g + 1) * HEAD_DIM].astype(BF16)
    kswa = _rope_rows(rows['kswa'][:HEAD_DIM], cos_t, sin_t) + [rows['kswa'][HEAD_DIM:]]
    kswa_ref[...] = jnp.concatenate(kswa, axis=0).T[:, :HEAD_DIM].astype(BF16)

    for name, ref, groups in (('vsel', vsel_ref, NSA_KV), ('vwin', vwin_ref, NSA_KV), ('vswa', vswa_ref, 1)):
        tk = ref.shape[2]
        for j in range(ref.shape[0]):
            for g in range(groups):
                ref[j, g * VAL_ROWS:g * VAL_ROWS + HEAD_DIM, :] = (
                    rows[name][g * HEAD_DIM:(g + 1) * HEAD_DIM, j * tk:(j + 1) * tk].astype(BF16))
                ref[j, g * VAL_ROWS + HEAD_DIM:(g + 1) * VAL_ROWS, :] = _ones_rows(tk)

    ng_ref[...] = jax.nn.sigmoid(rows['ng'][:NSA_GATE_W])

    for i, name in enumerate(('kc', 'vc')):
        nat = rows[name].T
        for g in range(NSA_KV):
            kcv_ref[i * NSA_KV + g] = nat[:, g * HEAD_DIM:(g + 1) * HEAD_DIM]

    gab_ref[...] = jax.nn.sigmoid(_dot(ub, wn_ref[...]))


def _proj(h2, g, wt, wn, pos_row, freq_col):
    n, d = h2.shape
    tm = ROW_TILE
    rows = lambda w: pl.BlockSpec((tm, w), lambda i: (i, 0))
    cols = lambda w: pl.BlockSpec((w, tm), lambda i: (0, i))
    grouped = lambda k: pl.BlockSpec((k, tm, HEAD_DIM), lambda i: (0, i, 0))
    tiles = lambda groups, tk: pl.BlockSpec((tm // tk, groups * VAL_ROWS, tk), lambda i: (i, 0, 0))
    val_shape = lambda groups, tk: jax.ShapeDtypeStruct((n // tk, groups * VAL_ROWS, tk), BF16)
    out_shape = [
        jax.ShapeDtypeStruct((NSA_Q_W, n), BF16), jax.ShapeDtypeStruct((SWA_Q_W, n), BF16),
        jax.ShapeDtypeStruct((NSA_KV, n, HEAD_DIM), BF16), jax.ShapeDtypeStruct((NSA_KV, n, HEAD_DIM), BF16),
        jax.ShapeDtypeStruct((n, HEAD_DIM), BF16),
        val_shape(NSA_KV, NSA_TILE), val_shape(NSA_KV, NSA_TILE), val_shape(1, SWA_VTILE),
        jax.ShapeDtypeStruct((NSA_GATE_W, n), F32), jax.ShapeDtypeStruct((2 * NSA_KV, n, HEAD_DIM), F32),
        jax.ShapeDtypeStruct((n, 2 * d), F32),
    ]
    return pl.pallas_call(
        _proj_kernel,
        grid=(n // tm,),
        in_specs=[rows(d), _resident((1, d)), _resident(wt.shape), _resident(wn.shape),
                  cols(1), _resident(freq_col.shape)],
        out_specs=[cols(NSA_Q_W), cols(SWA_Q_W), grouped(NSA_KV), grouped(NSA_KV), rows(HEAD_DIM),
                   tiles(NSA_KV, NSA_TILE), tiles(NSA_KV, NSA_TILE), tiles(1, SWA_VTILE),
                   cols(NSA_GATE_W), grouped(2 * NSA_KV), rows(2 * d)],
        out_shape=out_shape,
        compiler_params=_params(1),
        name="proj",
    )(h2, g, wt, wn, pos_row, freq_col)


def _gelu(x):
    return jax.nn.gelu(x, approximate=True)


def _compress_kernel(kc_ref, vc_ref, pek_ref, pev_ref, w1k_ref, w1v_ref, w2kt_ref, w2vt_ref,
                     pos_ref, freq_ref, kc_out, vct_out):
    nc = kc_out.shape[1]

    def hidden(x_ref, pe_ref, w1_ref):
        top = bot = None
        for j in range(CMP_STRIDE):
            x = x_ref[0, pl.ds(j, nc, stride=CMP_STRIDE), :]
            t = _dot((x + pe_ref[j:j + 1, :]).astype(BF16), w1_ref[j])
            b = _dot((x + pe_ref[CMP_STRIDE + j:CMP_STRIDE + j + 1, :]).astype(BF16), w1_ref[CMP_STRIDE + j])
            top, bot = (t, b) if top is None else (top + t, bot + b)
        return _gelu(top + pltpu.roll(bot, shift=nc - 1, axis=0)).astype(BF16)

    kt = _dot_nt(w2kt_ref[...], hidden(kc_ref, pek_ref, w1k_ref))
    cos_t, sin_t = _rope_angles(pos_ref[0], freq_ref[...])
    kt = jnp.concatenate(_rope_rows(kt, cos_t, sin_t) + [jnp.zeros((LANES - HEAD_DIM, nc), F32)], axis=0)
    kc_out[0] = kt.T[:, :HEAD_DIM].astype(BF16)
    vct_out[0] = _dot_nt(w2vt_ref[...], hidden(vc_ref, pev_ref, w1v_ref)).astype(BF16)


def _compress(kcv, pek, pev, w1k, w1v, w2kt, w2vt, pos_c, freq_col, b, g_kv, s_len):
    nc = s_len // CMP_STRIDE
    hid = w1k.shape[-1]
    kcv = kcv.reshape(2 * g_kv * b, s_len, HEAD_DIM)
    src = lambda kind: pl.BlockSpec((1, s_len, HEAD_DIM),
                                    lambda i: ((kind * g_kv + i % g_kv) * b + i // g_kv, 0, 0))
    per = lambda *s: pl.BlockSpec((1,) + s, lambda i: (i,) + (0,) * len(s))
    return pl.pallas_call(
        _compress_kernel,
        grid=(b * g_kv,),
        in_specs=[src(0), src(1), _resident(pek.shape), _resident(pev.shape),
                  _resident(w1k.shape), _resident(w1v.shape), _resident((HEAD_DIM, hid)),
                  _resident((HEAD_DIM, hid)),
                  pl.BlockSpec((1, 1, nc), lambda i: (i // g_kv, 0, 0)), _resident(freq_col.shape)],
        out_specs=[per(nc, HEAD_DIM), per(HEAD_DIM, nc)],
        out_shape=[jax.ShapeDtypeStruct((b * g_kv, nc, HEAD_DIM), BF16),
                   jax.ShapeDtypeStruct((b * g_kv, HEAD_DIM, nc), BF16)],
        compiler_params=_params(1),
        name="compress",
    )(kcv, kcv, pek, pev, w1k, w1v, w2kt, w2vt, pos_c, freq_col)


def _cmp_topk_body(q_ref, kc_ref, vct_ref, ov_ref, gate_ref, mb_ref, oc_ref,
                   s_buf, p_buf, bias_buf, psum_buf, x_buf, *, n_top, nc, n_sel):
    tq = q_ref.shape[1]
    sub = s_buf.shape[1]
    rep = NSA_REP
    qi = pl.program_id(2)
    q0 = qi * tq
    stages = [(u, r) for u in range(tq // sub) for r in range(rep)]
    q_cols = lambda u, r: q_ref[r * HEAD_DIM:(r + 1) * HEAD_DIM, u * sub:(u + 1) * sub]
    keys = kc_ref[0, 0:nc, :]
    vals = vct_ref[0, :, 0:nc]

    s_buf[0:nc] = _dot(keys, q_cols(*stages[0]))
    inv_prev = None
    for n, (u, r) in enumerate(stages):
        cols = slice(u * sub, (u + 1) * sub)
        if n + 1 < len(stages):
            s_next = _dot(keys, q_cols(*stages[n + 1]))
        if n > 0:
            pu, pr = stages[n - 1]
            oc_ref[pr * HEAD_DIM:(pr + 1) * HEAD_DIM, pu * sub:(pu + 1) * sub] = (
                _dot(vals, p_buf[0:nc]) * (inv_prev * gate_ref[0, 0, pr:pr + 1, pu * sub:(pu + 1) * sub]))
        last = ((q0 + u * sub + lax.broadcasted_iota(jnp.int32, (1, sub), 1)) - (CMP_LEN - 1)) >> 4
        if r == 0:
            bias_buf[0:nc] = jnp.where(lax.broadcasted_iota(jnp.int32, (nc, sub), 0) <= last, 0.0, NEG_INF)
        s = s_buf[0:nc] + bias_buf[0:nc]
        m = jnp.max(s, axis=0, keepdims=True)
        e = jnp.exp2(s - m)
        inv_prev = jnp.where(last >= 0, 1.0 / jnp.sum(e, axis=0, keepdims=True), 0.0)
        p_buf[0:nc] = e.astype(BF16)
        if r == 0:
            psum_buf[0:nc, cols] = e * inv_prev
        else:
            psum_buf[0:nc, cols] += e * inv_prev
        if n + 1 < len(stages):
            s_buf[0:nc] = s_next
    pu, pr = stages[-1]
    oc_ref[pr * HEAD_DIM:(pr + 1) * HEAD_DIM, pu * sub:(pu + 1) * sub] = (
        _dot(vals, p_buf[0:nc]) * (inv_prev * gate_ref[0, 0, pr:pr + 1, pu * sub:(pu + 1) * sub]))

    psum = psum_buf[0:nc, :]
    p_hi = psum.astype(BF16)
    p_lo = (psum - p_hi.astype(F32)).astype(BF16)
    pool = ov_ref[0:n_sel, 0:nc]
    imp = _dot(pool, p_hi) + _dot(pool, p_lo)

    blk = lax.broadcasted_iota(jnp.int32, (n_sel, tq), 0)
    tb = (q0 + lax.broadcasted_iota(jnp.int32, (n_sel, tq), 1)) >> SEL_SHIFT
    forced = (blk == 0) | ((tb - blk >= 0) & (tb - blk < SEL_LOCAL))
    n_forced = 1 + SEL_LOCAL
    premark = forced & (jnp.full((n_sel, tq), qi, jnp.int32) > 0)
    x_buf[0:n_sel] = jnp.where(premark, -jnp.inf, jnp.where(forced, FORCE, jnp.where(blk > tb, -FORCE, imp)))

    def pick(_, carry):
        x = x_buf[0:n_sel]
        top = jnp.max(x, axis=0, keepdims=True)
        first = jnp.min(jnp.where(x == top, blk, n_sel), axis=0, keepdims=True)
        x_buf[0:n_sel] = jnp.where(blk == first, -jnp.inf, x)
        return carry

    lax.fori_loop(0, jnp.where(qi > 0, n_top - n_forced, n_top), pick, 0)
    mb_ref[0, 0:n_sel, :] = jnp.where(x_buf[0:n_sel] == -jnp.inf, 0.0, NEG_INF).astype(BF16)
    if n_sel < mb_ref.shape[1]:
        mb_ref[0, n_sel:, :] = jnp.full((mb_ref.shape[1] - n_sel, tq), NEG_INF, BF16)


def _cmp_topk_kernel(*refs, n_top, n_tiles):
    nc = refs[1].shape[1]
    n_sel = refs[3].shape[0]
    qi = pl.program_id(2)
    for v in range(max(n_tiles // 2, 1)):
        frac = lambda total: min(total, (2 * v + 2) * total // n_tiles)
        pl.when((qi >> 1) == v)(functools.partial(_cmp_topk_body, *refs, n_top=n_top, nc=frac(nc), n_sel=frac(n_sel)))


def _cmp_topk(qn_t, kc, vct, ov, gates, b, g_kv, s_len, n_top):
    tq = min(CMP_TILE, s_len)
    sub = min(CMP_SUBTILE, tq)
    nq = s_len // tq
    nc = kc.shape[1]
    n_sel = ov.shape[0]
    rep = NSA_REP
    return pl.pallas_call(
        functools.partial(_cmp_topk_kernel, n_top=n_top, n_tiles=nq),
        grid=(b, g_kv, nq),
        in_specs=[
            pl.BlockSpec((rep * HEAD_DIM, tq), lambda bi, gi, qi: (gi, bi * nq + qi)),
            pl.BlockSpec((1, nc, HEAD_DIM), lambda bi, gi, qi: (bi * g_kv + gi, 0, 0)),
            pl.BlockSpec((1, HEAD_DIM, nc), lambda bi, gi, qi: (bi * g_kv + gi, 0, 0)),
            _resident(ov.shape),
            pl.BlockSpec((1, 1, rep, tq), lambda bi, gi, qi: (0, gi, 0, bi * nq + qi)),
        ],
        out_specs=[
            pl.BlockSpec((1, n_sel, tq), lambda bi, gi, qi: (bi * g_kv + gi, 0, qi)),
            pl.BlockSpec((rep * HEAD_DIM, tq), lambda bi, gi, qi: (gi, bi * nq + qi)),
        ],
        out_shape=[jax.ShapeDtypeStruct((b * g_kv, n_sel, s_len), BF16),
                   jax.ShapeDtypeStruct((NSA_Q_W, b * s_len), F32)],
        scratch_shapes=[pltpu.VMEM((nc, sub), F32), pltpu.VMEM((nc, sub), BF16), pltpu.VMEM((nc, sub), F32),
                        pltpu.VMEM((nc, tq), F32), pltpu.VMEM((n_sel, tq), F32)],
        compiler_params=_params(3),
        name="cmp_topk",
    )(qn_t, kc, vct, ov, gates)


def _softmax_tile(m, s):
    m_new = jnp.maximum(m, jnp.max(s, axis=0, keepdims=True))
    return m_new, jnp.exp2(m - m_new), jnp.exp2(s - m_new).astype(BF16)


def _normalize(acc):
    return acc[:HEAD_DIM] / acc[HEAD_DIM:HEAD_DIM + 1]


def _rel_pos(tk, tq):
    return lax.broadcasted_iota(jnp.int32, (tk, tq), 1) - lax.broadcasted_iota(jnp.int32, (tk, tq), 0)


def _window_bias(rel, back, tk, kj, window):
    dist = rel + (back * tk + jnp.where(kj >= 0, 0, window))
    return jnp.where((dist >= 0) & (dist < window), 0.0, NEG_INF)


def _pipe_stage(s_buf, p_buf, heads, cur=None, nxt=None, prv=None):
    if nxt is not None:
        s_next = [_dot(nxt[0], nxt[1][r][...]) for r in heads]
    if prv is not None:
        pv = [_dot(prv[0], p_buf[r]) for r in heads]
    out = None
    if cur is not None:
        m, bias = cur
        out = []
        for r in heads:
            s = s_buf[r] if bias is None else s_buf[r] + bias
            m_r, a_r, p_buf[r] = _softmax_tile(m[r], s)
            out.append((m_r, a_r))
        out = tuple(zip(*out))
    for r in heads:
        if prv is not None:
            prv[1][r] = prv[2][r] * prv[1][r] + pv[r]
        if nxt is not None:
            s_buf[r] = s_next[r]
    return out


def _nsa_kernel(q_ref, mb_ref, oc_ref, gate_ref, ks_ref, ex_ref, vs_ref, kw_ref, vw_ref, o_ref,
                qa_buf, s_buf, p_buf, sel_acc, win_acc):
    tq = q_ref.shape[1]
    tk = tq
    n_sel = mb_ref.shape[1]
    rep = NSA_REP
    heads = range(rep)
    qi = pl.program_id(2)
    stage = functools.partial(_pipe_stage, s_buf, p_buf, heads)
    key_rows = lambda j: pl.ds(pl.multiple_of(j * tk, tk), tk)

    def sel_keys(j):
        return jnp.concatenate([ex_ref[key_rows(j), :], ks_ref[0, key_rows(j), :]], axis=1)

    q_heads = [q_ref.at[pl.ds(r * HEAD_DIM, HEAD_DIM), :] for r in heads]
    q_aug = [qa_buf.at[r] for r in heads]
    for r in heads:
        qa_buf[r, 0:n_sel, :] = mb_ref[0]
        qa_buf[r, n_sel:n_sel + HEAD_DIM, :] = q_heads[r][...]
        s_buf[r] = _dot(sel_keys(0), qa_buf[r])
        p_buf[r] = jnp.zeros((tk, tq), BF16)
        sel_acc[r] = jnp.zeros((VAL_ROWS, tq), F32)
        win_acc[r] = jnp.zeros((VAL_ROWS, tq), F32)
    row = lambda v: tuple(jnp.full((1, tq), v, F32) for _ in heads)

    def sel_stage(i, carry):
        m, alpha = carry
        return stage(cur=(m, None), nxt=(sel_keys(i + 1), q_aug),
                     prv=(vs_ref[jnp.maximum(i - 1, 0)], sel_acc, alpha))

    carry = lax.fori_loop(0, qi >> 1, lambda j, c: sel_stage(2 * j + 1, sel_stage(2 * j, c)),
                          (row(NEG_INF), row(1.0)))
    m, alpha = lax.fori_loop(0, qi & 1, lambda _, c: sel_stage(qi - 1, c), carry)

    rel = _rel_pos(tk, tq)
    backs = list(range(NSA_WINDOW // tk, -1, -1))
    win_tile = [jnp.maximum(qi - back, 0) for back in backs]
    win_bias = [_window_bias(rel, back, tk, qi - back, NSA_WINDOW) for back in backs]
    _, alpha_d = stage(cur=(m, jnp.where(rel >= 0, 0.0, NEG_INF)),
                       nxt=(kw_ref[0, key_rows(win_tile[0]), :], q_heads),
                       prv=(vs_ref[jnp.maximum(qi - 1, 0)], sel_acc, alpha))
    prv = (vs_ref[qi], sel_acc, alpha_d)
    m_w = row(NEG_INF)
    for n in range(len(backs)):
        nxt = (kw_ref[0, key_rows(win_tile[n + 1]), :], q_heads) if n + 1 < len(backs) else None
        m_w, alpha_w = stage(cur=(m_w, win_bias[n]), nxt=nxt, prv=prv)
        prv = (vw_ref[win_tile[n]], win_acc, alpha_w)
    stage(prv=prv)

    outs = [oc_ref[r * HEAD_DIM:(r + 1) * HEAD_DIM, :]
            + gate_ref[1, 0, r:r + 1, :] * _normalize(sel_acc[r])
            + gate_ref[2, 0, r:r + 1, :] * _normalize(win_acc[r]) for r in heads]
    o_ref[...] = jnp.concatenate(outs, axis=0).T.astype(o_ref.dtype)


def _nsa(qn_t, mb, oc_t, gates, ksel, expand, vsel, kwin, vwin, b, g_kv, s_len):
    tq = NSA_TILE
    nq = s_len // tq
    rep = NSA_REP
    n_sel = mb.shape[1]
    keys = pl.BlockSpec((1, s_len, HEAD_DIM), lambda bi, gi, qi: (gi * b + bi, 0, 0))
    vals = pl.BlockSpec((nq, VAL_ROWS, tq), lambda bi, gi, qi: (bi, gi, 0))
    return pl.pallas_call(
        _nsa_kernel,
        grid=(b, g_kv, nq),
        in_specs=[
            pl.BlockSpec((rep * HEAD_DIM, tq), lambda bi, gi, qi: (gi, bi * nq + qi)),
            pl.BlockSpec((1, n_sel, tq), lambda bi, gi, qi: (bi * g_kv + gi, 0, qi)),
            pl.BlockSpec((rep * HEAD_DIM, tq), lambda bi, gi, qi: (gi, bi * nq + qi)),
            pl.BlockSpec((3, 1, rep, tq), lambda bi, gi, qi: (0, gi, 0, bi * nq + qi)),
            keys, _resident(expand.shape), vals, keys, vals,
        ],
        out_specs=pl.BlockSpec((tq, rep * HEAD_DIM), lambda bi, gi, qi: (bi * nq + qi, gi)),
        out_shape=jax.ShapeDtypeStruct((b * s_len, NSA_Q_W), BF16),
        scratch_shapes=[pltpu.VMEM((rep, n_sel + HEAD_DIM, tq), BF16), pltpu.VMEM((rep, tq, tq), F32),
                        pltpu.VMEM((rep, tq, tq), BF16), pltpu.VMEM((rep, VAL_ROWS, tq), F32),
                        pltpu.VMEM((rep, VAL_ROWS, tq), F32)],
        compiler_params=_params(3),
        name="nsa",
    )(qn_t, mb, oc_t, gates, ksel.reshape(g_kv * b, s_len, HEAD_DIM), expand, vsel,
      kwin.reshape(g_kv * b, s_len, HEAD_DIM), vwin)


def _swa_kernel(q_ref, sink_ref, k_ref, v_ref, o_ref, s_buf, p_buf, bias_buf, o_buf):
    tq = q_ref.shape[1]
    tv = v_ref.shape[2]
    nk = s_buf.shape[0]
    heads = SWA_HEADS
    qi = pl.program_id(1)
    q0 = qi * tq
    k0 = jnp.maximum(q0 - SWA_WINDOW, 0)

    keys = k_ref[0, pl.ds(pl.multiple_of(k0, tv), nk), :]
    vals = jnp.concatenate([v_ref[k0 // tv + j] for j in range(nk // tv)], axis=1)
    dist = _rel_pos(nk, tq) + (q0 - k0)
    bias_buf[...] = jnp.where((dist >= 0) & (dist < SWA_WINDOW), 0.0, NEG_INF)
    q_head = lambda h: q_ref[h * HEAD_DIM:(h + 1) * HEAD_DIM, :]

    def finish(h, m_all, sink):
        acc = _dot(vals, p_buf[...])
        o_buf[h * HEAD_DIM:(h + 1) * HEAD_DIM, :] = (
            acc[:HEAD_DIM] / (acc[HEAD_DIM:HEAD_DIM + 1] + jnp.exp2(sink - m_all)))

    s_buf[...] = _dot(keys, q_head(0))
    prev = None
    for h in range(heads):
        if h + 1 < heads:
            s_next = _dot(keys, q_head(h + 1))
        if prev is not None:
            finish(*prev)
        sink = sink_ref[:, h * tq:(h + 1) * tq]
        s = s_buf[...] + bias_buf[...]
        m_all = jnp.maximum(jnp.max(s, axis=0, keepdims=True), sink)
        p_buf[...] = jnp.exp2(s - m_all).astype(BF16)
        prev = (h, m_all, sink)
        if h + 1 < heads:
            s_buf[...] = s_next
    finish(*prev)
    o_ref[...] = o_buf[...].T.astype(o_ref.dtype)


def _swa(sq_t, sink_row, k, v, b, s_len):
    tq = min(SWA_TILE, s_len)
    nq = s_len // tq
    tv = v.shape[2]
    nk = tq + SWA_WINDOW
    return pl.pallas_call(
        _swa_kernel,
        grid=(b, nq),
        in_specs=[pl.BlockSpec((SWA_Q_W, tq), lambda bi, qi: (0, bi * nq + qi)),
                  _resident(sink_row.shape),
                  pl.BlockSpec((1, s_len, HEAD_DIM), lambda bi, qi: (bi, 0, 0)),
                  pl.BlockSpec((s_len // tv, VAL_ROWS, tv), lambda bi, qi: (bi, 0, 0))],
        out_specs=pl.BlockSpec((tq, SWA_Q_W), lambda bi, qi: (bi * nq + qi, 0)),
        out_shape=jax.ShapeDtypeStruct((b * s_len, SWA_Q_W), BF16),
        scratch_shapes=[pltpu.VMEM((nk, tq), F32), pltpu.VMEM((nk, tq), BF16), pltpu.VMEM((nk, tq), F32),
                        pltpu.VMEM((SWA_Q_W, tq), F32)],
        compiler_params=_params(2),
        name="swa",
    )(sq_t, sink_row, k.reshape(b, s_len, HEAD_DIM), v)


def _overlap_matrix(n_sel, n_cmp, n_cmp_pad):
    cs = np.arange(n_cmp) * CMP_STRIDE
    ss = np.arange(n_sel) * SEL_LEN
    ov = np.clip(np.minimum(cs[None, :] + CMP_LEN, ss[:, None] + SEL_LEN)
                 - np.maximum(cs[None, :], ss[:, None]), 0, None).astype(np.float32) / CMP_LEN
    return np.pad(ov, ((0, 0), (0, n_cmp_pad - n_cmp)))


def _layer(h, positions, w, b, s_len, norm_final, last):
    n, d = h.shape
    g_kv = NSA_KV
    n_sel = s_len // SEL_LEN
    n_top = min(SEL_TOPN, n_sel)
    nc = s_len // CMP_STRIDE
    bf = lambda a: a.astype(BF16)

    h = _ffn(h, w['norm_ffn1'][None], bf(w['ffn1_gate']), bf(w['ffn1_up']), bf(w['ffn1_down']),
             w['norm_ffn1'][None], False)

    pts = np.cumsum((NSA_Q_W,) + (NSA_KV_W,) * 6 + (NSA_GATE_W, SWA_Q_W, SWA_KV_W, SWA_KV_W, d, d))[:-1]
    (w_nq, w_kc, w_vc, w_ksl, w_vsl, w_kwn, w_vwn, w_ng, w_sq, w_sk, w_sv, w_ga, w_gb) = jnp.split(
        w['w_in'], pts, axis=1)
    parts = dict(qn=w_nq, sq=w_sq, ksel=w_ksl, kwin=w_kwn, kswa=w_sk, vsel=w_vsl, vwin=w_vwn, vswa=w_sv,
                 ng=w_ng, kc=w_kc, vc=w_vc)
    w_t = jnp.concatenate([jnp.pad(parts[name], ((0, 0), (0, width - parts[name].shape[1])))
                           for name, width in PROJ_ROWS], axis=1).T
    w_n = jnp.concatenate([w_ga, w_gb], axis=1)

    freq_col = (ROPE_THETA ** (-jnp.arange(HALF_DIM, dtype=F32) / HALF_DIM))[:, None]
    (qn_t, sq_t, ksel, kwin, kswa, vsel, vwin, vswa, ng_t, kcv, gab) = _proj(
        h, w['norm_mix'][None], bf(w_t), bf(w_n), positions.reshape(1, n), freq_col)

    pos_c = jnp.pad(positions[:, CMP_LEN - 1::CMP_STRIDE], ((0, 0), (0, 1)))[:, None, :]
    w1 = lambda a: bf(a.reshape(CMP_LEN, HEAD_DIM, a.shape[-1]))
    kc, vc_t = _compress(kcv, w['cmp_pe_k'], w['cmp_pe_v'], w1(w['cmp_k_w1']), w1(w['cmp_v_w1']),
                         bf(w['cmp_k_w2'].T), bf(w['cmp_v_w2'].T), pos_c, freq_col, b, g_kv, s_len)

    gates = ng_t.reshape(3, g_kv, NSA_REP, n)
    ov = jnp.asarray(_overlap_matrix(n_sel, nc - 1, nc), BF16)
    mb, oc_t = _cmp_topk(qn_t, kc, vc_t, ov, gates, b, g_kv, s_len, n_top)

    expand = jnp.asarray(np.arange(s_len)[:, None] // SEL_LEN == np.arange(n_sel)[None, :], BF16)
    o_a = _nsa(qn_t, mb, oc_t, gates, ksel, expand, vsel, kwin, vwin, b, g_kv, s_len)

    sink_row = jnp.repeat(w['swa_sinks'].astype(F32) * LOG2E, min(SWA_TILE, s_len))[None]
    o_b = _swa(sq_t, sink_row, kswa, vswa, b, s_len)

    branches = (o_a, o_b, gab, bf(w['w_branch_a']), bf(w['w_branch_b']), bf(w['w_out']))
    return _ffn(h, w['norm_ffn2'][None], bf(w['ffn2_gate']), bf(w['ffn2_up']), bf(w['ffn2_down']),
                norm_final[None], last, branches)


def kernel(x, positions, norm_ffn1, ffn1_gate, ffn1_up, ffn1_down, norm_mix, w_in, cmp_pe_k, cmp_k_w1, cmp_k_w2, cmp_pe_v, cmp_v_w1, cmp_v_w2, swa_sinks, w_branch_a, w_branch_b, w_out, norm_ffn2, ffn2_gate, ffn2_up, ffn2_down, norm_final):
    b, s_len, d = x.shape
    stacked = dict(norm_ffn1=norm_ffn1, ffn1_gate=ffn1_gate, ffn1_up=ffn1_up, ffn1_down=ffn1_down,
                   norm_mix=norm_mix, w_in=w_in, cmp_pe_k=cmp_pe_k, cmp_k_w1=cmp_k_w1, cmp_k_w2=cmp_k_w2,
                   cmp_pe_v=cmp_pe_v, cmp_v_w1=cmp_v_w1, cmp_v_w2=cmp_v_w2, swa_sinks=swa_sinks,
                   w_branch_a=w_branch_a, w_branch_b=w_branch_b, w_out=w_out,
                   norm_ffn2=norm_ffn2, ffn2_gate=ffn2_gate, ffn2_up=ffn2_up, ffn2_down=ffn2_down)
    depth = norm_ffn1.shape[0]
    h = x.reshape(b * s_len, d)
    for i in range(depth):
        w = {k: v[i] for k, v in stacked.items()}
        h = _layer(h, positions, w, b, s_len, norm_final, i == depth - 1)
    return h.reshape(b, s_len, d)
```

```python
import functools

import numpy as np
import jax
import jax.numpy as jnp
from jax import lax
from jax.experimental import pallas as pl
from jax.experimental.pallas import tpu as pltpu

HEAD_DIM = 64
HALF_DIM = HEAD_DIM // 2
NSA_HEADS = 8
NSA_KV = 2
NSA_REP = NSA_HEADS // NSA_KV
SWA_HEADS = 8
CMP_STRIDE = 16
CMP_LEN = 2 * CMP_STRIDE
SEL_LEN = 64
SEL_SHIFT = 6
SEL_TOPN = 16
SEL_LOCAL = 2
NSA_WINDOW = 512
SWA_WINDOW = 128
ROPE_THETA = 10000.0
RMS_EPS = 1e-6
FFN_HALF = 0.5
NEG_INF = -1e30
FORCE = 1e9
LOG2E = 1.4426950408889634
Q_SCALE = HEAD_DIM ** -0.5 * LOG2E
BF16_ROWS = 16
VAL_ROWS = HEAD_DIM + BF16_ROWS
LANES = 128

NSA_Q_W = NSA_HEADS * HEAD_DIM
NSA_KV_W = NSA_KV * HEAD_DIM
NSA_GATE_W = 3 * NSA_HEADS
SWA_Q_W = SWA_HEADS * HEAD_DIM
SWA_KV_W = HEAD_DIM

VMEM_LIMIT_BYTES = 56 * 1024 * 1024

BF16 = jnp.bfloat16
F32 = jnp.float32

ROW_TILE = 512
CMP_TILE = 1024
CMP_SUBTILE = 256
NSA_TILE = 256
SWA_TILE = 256
SWA_VTILE = 128


def _params(n_axes):
    return pltpu.CompilerParams(dimension_semantics=("arbitrary",) * n_axes,
                                vmem_limit_bytes=VMEM_LIMIT_BYTES)


def _resident(shape):
    zeros = (0,) * len(shape)
    return pl.BlockSpec(shape, lambda *_: zeros, pipeline_mode=pl.Buffered(1))


def _rms(x, g):
    y = x * lax.rsqrt(jnp.mean(x * x, axis=-1, keepdims=True) + RMS_EPS)
    return y * g


def _dot(a, b):
    return jnp.dot(a, b, preferred_element_type=F32)


def _dot_nt(a, b):
    return lax.dot_general(a, b, (((1,), (1,)), ((), ())), preferred_element_type=F32)


def _rope_angles(pos_row, freq_col):
    ang = pos_row.astype(F32) * freq_col
    return jnp.cos(ang), jnp.sin(ang)


def _rope_rows(block, cos_t, sin_t):
    out = []
    for hd in range(block.shape[0] // HEAD_DIM):
        x1 = block[hd * HEAD_DIM:hd * HEAD_DIM + HALF_DIM]
        x2 = block[hd * HEAD_DIM + HALF_DIM:(hd + 1) * HEAD_DIM]
        out += [x1 * cos_t - x2 * sin_t, x2 * cos_t + x1 * sin_t]
    return out


def _ones_rows(tk):
    return jnp.where(lax.broadcasted_iota(jnp.int32, (BF16_ROWS, tk), 0) == 0, 1.0, 0.0).astype(BF16)


def _merge_branches(h_ref, oa_ref, ob_ref, gab_ref, wa_ref, wb_ref, wo_ref):
    d = h_ref.shape[1]
    gab = gab_ref[...]
    merged = gab[:, :d] * _dot(oa_ref[...], wa_ref[...]) + gab[:, d:] * _dot(ob_ref[...], wb_ref[...])
    return h_ref[...] + _dot(merged.astype(BF16), wo_ref[...])


def _ffn_kernel(*refs, final_norm, merge):
    x_ref, g_ref, wg_ref, wu_ref, wd_ref, gf_ref, o_ref = refs[-7:]
    x = _merge_branches(x_ref, *refs[:-7]) if merge else x_ref[...]
    xb = _rms(x, g_ref[...]).astype(BF16)
    a = _dot(xb, wg_ref[...])
    b = _dot(xb, wu_ref[...])
    t = (a * jax.nn.sigmoid(a)) * b
    h = x + FFN_HALF * _dot(t.astype(BF16), wd_ref[...])
    if final_norm:
        h = _rms(h, gf_ref[...])
    o_ref[...] = h


def _ffn(x2, g, wg, wu, wd, gf, final_norm, branches=None):
    n, d = x2.shape
    f = wg.shape[1]
    tm = ROW_TILE
    rows = lambda w: pl.BlockSpec((tm, w), lambda i: (i, 0))
    merge_specs, merge_args = [], ()
    if branches is not None:
        oa, ob, gab, wa, wb, wo = branches
        merge_specs = [rows(oa.shape[1]), rows(ob.shape[1]), rows(2 * d),
                       _resident(wa.shape), _resident(wb.shape), _resident(wo.shape)]
        merge_args = branches
    return pl.pallas_call(
        functools.partial(_ffn_kernel, final_norm=final_norm, merge=branches is not None),
        grid=(n // tm,),
        in_specs=merge_specs + [rows(d), _resident((1, d)), _resident((d, f)), _resident((d, f)),
                                _resident((f, d)), _resident((1, d))],
        out_specs=rows(d),
        out_shape=jax.ShapeDtypeStruct((n, d), F32),
        compiler_params=_params(1),
        name="ffn_final" if final_norm else "ffn",
    )(*merge_args, x2, g, wg, wu, wd, gf)


PROJ_ROWS = (('qn', NSA_Q_W), ('sq', SWA_Q_W), ('ksel', NSA_KV_W), ('kwin', NSA_KV_W), ('kswa', LANES),
             ('vsel', NSA_KV_W), ('vwin', NSA_KV_W), ('vswa', SWA_KV_W), ('ng', 32),
             ('kc', NSA_KV_W), ('vc', NSA_KV_W))


def _proj_kernel(h_ref, g_ref, wt_ref, wn_ref, pos_ref, freq_ref,
                 qn_ref, sq_ref, ksel_ref, kwin_ref, kswa_ref, vsel_ref, vwin_ref, vswa_ref,
                 ng_ref, kcv_ref, gab_ref):
    ub = _rms(h_ref[...], g_ref[...]).astype(BF16)
    yt = _dot_nt(wt_ref[...], ub)
    rows, o = {}, 0
    for name, width in PROJ_ROWS:
        rows[name] = yt[o:o + width]
        o += width
    cos_t, sin_t = _rope_angles(pos_ref[...], freq_ref[...])

    for name, ref in (('qn', qn_ref), ('sq', sq_ref)):
        for i, piece in enumerate(_rope_rows(rows[name], cos_t, sin_t)):
            ref[i * HALF_DIM:(i + 1) * HALF_DIM, :] = (piece * Q_SCALE).astype(BF16)

    for name, ref in (('ksel', ksel_ref), ('kwin', kwin_ref)):
        k_nat = jnp.concatenate(_rope_rows(rows[name], cos_t, sin_t), axis=0).T
        for g in range(NSA_KV):
            ref[g] = k_nat[:, g * HEAD_DIM:(g + 1) * HEAD_DIM].astype(BF16)
    kswa = _rope_rows(rows['kswa'][:HEAD_DIM], cos_t, sin_t) + [rows['kswa'][HEAD_DIM:]]
    kswa_ref[...] = jnp.concatenate(kswa, axis=0).T[:, :HEAD_DIM].astype(BF16)

    for name, ref, groups in (('vsel', vsel_ref, NSA_KV), ('vwin', vwin_ref, NSA_KV), ('vswa', vswa_ref, 1)):
        tk = ref.shape[2]
        for j in range(ref.shape[0]):
            for g in range(groups):
                ref[j, g * VAL_ROWS:g * VAL_ROWS + HEAD_DIM, :] = (
                    rows[name][g * HEAD_DIM:(g + 1) * HEAD_DIM, j * tk:(j + 1) * tk].astype(BF16))
                ref[j, g * VAL_ROWS + HEAD_DIM:(g + 1) * VAL_ROWS, :] = _ones_rows(tk)

    ng_ref[...] = jax.nn.sigmoid(rows['ng'][:NSA_GATE_W])

    for i, name in enumerate(('kc', 'vc')):
        nat = rows[name].T
        for g in range(NSA_KV):
            kcv_ref[i * NSA_KV + g] = nat[:, g * HEAD_DIM:(g + 1) * HEAD_DIM]

    gab_ref[...] = jax.nn.sigmoid(_dot(ub, wn_ref[...]))


def _proj(h2, g, wt, wn, pos_row, freq_col):
    n, d = h2.shape
    tm = ROW_TILE
    rows = lambda w: pl.BlockSpec((tm, w), lambda i: (i, 0))
    cols = lambda w: pl.BlockSpec((w, tm), lambda i: (0, i))
    grouped = lambda k: pl.BlockSpec((k, tm, HEAD_DIM), lambda i: (0, i, 0))
    tiles = lambda groups, tk: pl.BlockSpec((tm // tk, groups * VAL_ROWS, tk), lambda i: (i, 0, 0))
    val_shape = lambda groups, tk: jax.ShapeDtypeStruct((n // tk, groups * VAL_ROWS, tk), BF16)
    out_shape = [
        jax.ShapeDtypeStruct((NSA_Q_W, n), BF16), jax.ShapeDtypeStruct((SWA_Q_W, n), BF16),
        jax.ShapeDtypeStruct((NSA_KV, n, HEAD_DIM), BF16), jax.ShapeDtypeStruct((NSA_KV, n, HEAD_DIM), BF16),
        jax.ShapeDtypeStruct((n, HEAD_DIM), BF16),
        val_shape(NSA_KV, NSA_TILE), val_shape(NSA_KV, NSA_TILE), val_shape(1, SWA_VTILE),
        jax.ShapeDtypeStruct((NSA_GATE_W, n), F32), jax.ShapeDtypeStruct((2 * NSA_KV, n, HEAD_DIM), F32),
        jax.ShapeDtypeStruct((n, 2 * d), F32),
    ]
    return pl.pallas_call(
        _proj_kernel,
        grid=(n // tm,),
        in_specs=[rows(d), _resident((1, d)), _resident(wt.shape), _resident(wn.shape),
                  cols(1), _resident(freq_col.shape)],
        out_specs=[cols(NSA_Q_W), cols(SWA_Q_W), grouped(NSA_KV), grouped(NSA_KV), rows(HEAD_DIM),
                   tiles(NSA_KV, NSA_TILE), tiles(NSA_KV, NSA_TILE), tiles(1, SWA_VTILE),
                   cols(NSA_GATE_W), grouped(2 * NSA_KV), rows(2 * d)],
        out_shape=out_shape,
        compiler_params=_params(1),
        name="proj",
    )(h2, g, wt, wn, pos_row, freq_col)


def _gelu(x):
    return jax.nn.gelu(x, approximate=True)


def _compress_kernel(kc_ref, vc_ref, pek_ref, pev_ref, w1k_ref, w1v_ref, w2kt_ref, w2vt_ref,
                     pos_ref, freq_ref, kc_out, vct_out):
    nc = kc_out.shape[1]

    def hidden(x_ref, pe_ref, w1_ref):
        top = bot = None
        for j in range(CMP_STRIDE):
            x = x_ref[0, pl.ds(j, nc, stride=CMP_STRIDE), :]
            t = _dot((x + pe_ref[j:j + 1, :]).astype(BF16), w1_ref[j])
            b = _dot((x + pe_ref[CMP_STRIDE + j:CMP_STRIDE + j + 1, :]).astype(BF16), w1_ref[CMP_STRIDE + j])
            top, bot = (t, b) if top is None else (top + t, bot + b)
        return _gelu(top + pltpu.roll(bot, shift=nc - 1, axis=0)).astype(BF16)

    kt = _dot_nt(w2kt_ref[...], hidden(kc_ref, pek_ref, w1k_ref))
    cos_t, sin_t = _rope_angles(pos_ref[0], freq_ref[...])
    kt = jnp.concatenate(_rope_rows(kt, cos_t, sin_t) + [jnp.zeros((LANES - HEAD_DIM, nc), F32)], axis=0)
    kc_out[0] = kt.T[:, :HEAD_DIM].astype(BF16)
    vct_out[0] = _dot_nt(w2vt_ref[...], hidden(vc_ref, pev_ref, w1v_ref)).astype(BF16)


def _compress(kcv, pek, pev, w1k, w1v, w2kt, w2vt, pos_c, freq_col, b, g_kv, s_len):
    nc = s_len // CMP_STRIDE
    hid = w1k.shape[-1]
    kcv = kcv.reshape(2 * g_kv * b, s_len, HEAD_DIM)
    src = lambda kind: pl.BlockSpec((1, s_len, HEAD_DIM),
                                    lambda i: ((kind * g_kv + i % g_kv) * b + i // g_kv, 0, 0))
    per = lambda *s: pl.BlockSpec((1,) + s, lambda i: (i,) + (0,) * len(s))
    return pl.pallas_call(
        _compress_kernel,
        grid=(b * g_kv,),
        in_specs=[src(0), src(1), _resident(pek.shape), _resident(pev.shape),
                  _resident(w1k.shape), _resident(w1v.shape), _resident((HEAD_DIM, hid)),
                  _resident((HEAD_DIM, hid)),
                  pl.BlockSpec((1, 1, nc), lambda i: (i // g_kv, 0, 0)), _resident(freq_col.shape)],
        out_specs=[per(nc, HEAD_DIM), per(HEAD_DIM, nc)],
        out_shape=[jax.ShapeDtypeStruct((b * g_kv, nc, HEAD_DIM), BF16),
                   jax.ShapeDtypeStruct((b * g_kv, HEAD_DIM, nc), BF16)],
        compiler_params=_params(1),
        name="compress",
    )(kcv, kcv, pek, pev, w1k, w1v, w2kt, w2vt, pos_c, freq_col)


def _cmp_topk_body(q_ref, kc_ref, vct_ref, ov_ref, gate_ref, mb_ref, oc_ref,
                   s_buf, p_buf, bias_buf, psum_buf, x_buf, *, n_top, nc, n_sel):
    tq = q_ref.shape[1]
    sub = s_buf.shape[1]
    rep = NSA_REP
    qi = pl.program_id(2)
    q0 = qi * tq
    stages = [(u, r) for u in range(tq // sub) for r in range(rep)]
    q_cols = lambda u, r: q_ref[r * HEAD_DIM:(r + 1) * HEAD_DIM, u * sub:(u + 1) * sub]
    keys = kc_ref[0, 0:nc, :]
    vals = vct_ref[0, :, 0:nc]

    s_buf[0:nc] = _dot(keys, q_cols(*stages[0]))
    inv_prev = None
    for n, (u, r) in enumerate(stages):
        cols = slice(u * sub, (u + 1) * sub)
        if n + 1 < len(stages):
            s_next = _dot(keys, q_cols(*stages[n + 1]))
        if n > 0:
            pu, pr = stages[n - 1]
            oc_ref[pr * HEAD_DIM:(pr + 1) * HEAD_DIM, pu * sub:(pu + 1) * sub] = (
                _dot(vals, p_buf[0:nc]) * (inv_prev * gate_ref[0, 0, pr:pr + 1, pu * sub:(pu + 1) * sub]))
        last = ((q0 + u * sub + lax.broadcasted_iota(jnp.int32, (1, sub), 1)) - (CMP_LEN - 1)) >> 4
        if r == 0:
            bias_buf[0:nc] = jnp.where(lax.broadcasted_iota(jnp.int32, (nc, sub), 0) <= last, 0.0, NEG_INF)
        s = s_buf[0:nc] + bias_buf[0:nc]
        m = jnp.max(s, axis=0, keepdims=True)
        e = jnp.exp2(s - m)
        inv_prev = jnp.where(last >= 0, 1.0 / jnp.sum(e, axis=0, keepdims=True), 0.0)
        p_buf[0:nc] = e.astype(BF16)
        if r == 0:
            psum_buf[0:nc, cols] = e * inv_prev
        else:
            psum_buf[0:nc, cols] += e * inv_prev
        if n + 1 < len(stages):
            s_buf[0:nc] = s_next
    pu, pr = stages[-1]
    oc_ref[pr * HEAD_DIM:(pr + 1) * HEAD_DIM, pu * sub:(pu + 1) * sub] = (
        _dot(vals, p_buf[0:nc]) * (inv_prev * gate_ref[0, 0, pr:pr + 1, pu * sub:(pu + 1) * sub]))

    psum = psum_buf[0:nc, :]
    p_hi = psum.astype(BF16)
    p_lo = (psum - p_hi.astype(F32)).astype(BF16)
    pool = ov_ref[0:n_sel, 0:nc]
    imp = _dot(pool, p_hi) + _dot(pool, p_lo)

    blk = lax.broadcasted_iota(jnp.int32, (n_sel, tq), 0)
    tb = (q0 + lax.broadcasted_iota(jnp.int32, (n_sel, tq), 1)) >> SEL_SHIFT
    forced = (blk == 0) | ((tb - blk >= 0) & (tb - blk < SEL_LOCAL))
    n_forced = 1 + SEL_LOCAL
    premark = forced & (jnp.full((n_sel, tq), qi, jnp.int32) > 0)
    x_buf[0:n_sel] = jnp.where(premark, -jnp.inf, jnp.where(forced, FORCE, jnp.where(blk > tb, -FORCE, imp)))

    def pick(_, carry):
        x = x_buf[0:n_sel]
        top = jnp.max(x, axis=0, keepdims=True)
        first = jnp.min(jnp.where(x == top, blk, n_sel), axis=0, keepdims=True)
        x_buf[0:n_sel] = jnp.where(blk == first, -jnp.inf, x)
        return carry

    lax.fori_loop(0, jnp.where(qi > 0, n_top - n_forced, n_top), pick, 0)
    mb_ref[0, 0:n_sel, :] = jnp.where(x_buf[0:n_sel] == -jnp.inf, 0.0, NEG_INF).astype(BF16)
    if n_sel < mb_ref.shape[1]:
        mb_ref[0, n_sel:, :] = jnp.full((mb_ref.shape[1] - n_sel, tq), NEG_INF, BF16)


def _cmp_topk_kernel(*refs, n_top, n_tiles):
    nc = refs[1].shape[1]
    n_sel = refs[3].shape[0]
    qi = pl.program_id(2)
    for v in range(max(n_tiles // 2, 1)):
        frac = lambda total: min(total, (2 * v + 2) * total // n_tiles)
        pl.when((qi >> 1) == v)(functools.partial(_cmp_topk_body, *refs, n_top=n_top, nc=frac(nc), n_sel=frac(n_sel)))


def _cmp_topk(qn_t, kc, vct, ov, gates, b, g_kv, s_len, n_top):
    tq = min(CMP_TILE, s_len)
    sub = min(CMP_SUBTILE, tq)
    nq = s_len // tq
    nc = kc.shape[1]
    n_sel = ov.shape[0]
    rep = NSA_REP
    return pl.pallas_call(
        functools.partial(_cmp_topk_kernel, n_top=n_top, n_tiles=nq),
        grid=(b, g_kv, nq),
        in_specs=[
            pl.BlockSpec((rep * HEAD_DIM, tq), lambda bi, gi, qi: (gi, bi * nq + qi)),
            pl.BlockSpec((1, nc, HEAD_DIM), lambda bi, gi, qi: (bi * g_kv + gi, 0, 0)),
            pl.BlockSpec((1, HEAD_DIM, nc), lambda bi, gi, qi: (bi * g_kv + gi, 0, 0)),
            _resident(ov.shape),
            pl.BlockSpec((1, 1, rep, tq), lambda bi, gi, qi: (0, gi, 0, bi * nq + qi)),
        ],
        out_specs=[
            pl.BlockSpec((1, n_sel, tq), lambda bi, gi, qi: (bi * g_kv + gi, 0, qi)),
            pl.BlockSpec((rep * HEAD_DIM, tq), lambda bi, gi, qi: (gi, bi * nq + qi)),
        ],
        out_shape=[jax.ShapeDtypeStruct((b * g_kv, n_sel, s_len), BF16),
                   jax.ShapeDtypeStruct((NSA_Q_W, b * s_len), F32)],
        scratch_shapes=[pltpu.VMEM((nc, sub), F32), pltpu.VMEM((nc, sub), BF16), pltpu.VMEM((nc, sub), F32),
                        pltpu.VMEM((nc, tq), F32), pltpu.VMEM((n_sel, tq), F32)],
        compiler_params=_params(3),
        name="cmp_topk",
    )(qn_t, kc, vct, ov, gates)


def _softmax_tile(m, s):
    m_new = jnp.maximum(m, jnp.max(s, axis=0, keepdims=True))
    return m_new, jnp.exp2(m - m_new), jnp.exp2(s - m_new).astype(BF16)


def _normalize(acc):
    return acc[:HEAD_DIM] / acc[HEAD_DIM:HEAD_DIM + 1]


def _rel_pos(tk, tq):
    return lax.broadcasted_iota(jnp.int32, (tk, tq), 1) - lax.broadcasted_iota(jnp.int32, (tk, tq), 0)


def _window_bias(rel, back, tk, kj, window):
    dist = rel + (back * tk + jnp.where(kj >= 0, 0, window))
    return jnp.where((dist >= 0) & (dist < window), 0.0, NEG_INF)


def _pipe_stage(s_buf, p_buf, heads, cur=None, nxt=None, prv=None):
    if nxt is not None:
        s_next = [_dot(nxt[0], nxt[1][r][...]) for r in heads]
    if prv is not None:
        pv = [_dot(prv[0], p_buf[r]) for r in heads]
    out = None
    if cur is not None:
        m, bias = cur
        out = []
        for r in heads:
            s = s_buf[r] if bias is None else s_buf[r] + bias
            m_r, a_r, p_buf[r] = _softmax_tile(m[r], s)
            out.append((m_r, a_r))
        out = tuple(zip(*out))
    for r in heads:
        if prv is not None:
            prv[1][r] = prv[2][r] * prv[1][r] + pv[r]
        if nxt is not None:
            s_buf[r] = s_next[r]
    return out


def _nsa_kernel(q_ref, mb_ref, oc_ref, gate_ref, ks_ref, ex_ref, vs_ref, kw_ref, vw_ref, o_ref,
                qa_buf, s_buf, p_buf, sel_acc, win_acc):
    tq = q_ref.shape[1]
    tk = tq
    n_sel = mb_ref.shape[1]
    rep = NSA_REP
    heads = range(rep)
    qi = pl.program_id(2)
    stage = functools.partial(_pipe_stage, s_buf, p_buf, heads)
    key_rows = lambda j: pl.ds(pl.multiple_of(j * tk, tk), tk)

    def sel_keys(j):
        return jnp.concatenate([ex_ref[key_rows(j), :], ks_ref[0, key_rows(j), :]], axis=1)

    q_heads = [q_ref.at[pl.ds(r * HEAD_DIM, HEAD_DIM), :] for r in heads]
    q_aug = [qa_buf.at[r] for r in heads]
    for r in heads:
        qa_buf[r, 0:n_sel, :] = mb_ref[0]
        qa_buf[r, n_sel:n_sel + HEAD_DIM, :] = q_heads[r][...]
        s_buf[r] = _dot(sel_keys(0), qa_buf[r])
        p_buf[r] = jnp.zeros((tk, tq), BF16)
        sel_acc[r] = jnp.zeros((VAL_ROWS, tq), F32)
        win_acc[r] = jnp.zeros((VAL_ROWS, tq), F32)
    row = lambda v: tuple(jnp.full((1, tq), v, F32) for _ in heads)

    def sel_stage(i, carry):
        m, alpha = carry
        return stage(cur=(m, None), nxt=(sel_keys(i + 1), q_aug),
                     prv=(vs_ref[jnp.maximum(i - 1, 0)], sel_acc, alpha))

    carry = lax.fori_loop(0, qi >> 1, lambda j, c: sel_stage(2 * j + 1, sel_stage(2 * j, c)),
                          (row(NEG_INF), row(1.0)))
    m, alpha = lax.fori_loop(0, qi & 1, lambda _, c: sel_stage(qi - 1, c), carry)

    rel = _rel_pos(tk, tq)
    backs = list(range(NSA_WINDOW // tk, -1, -1))
    win_tile = [jnp.maximum(qi - back, 0) for back in backs]
    win_bias = [_window_bias(rel, back, tk, qi - back, NSA_WINDOW) for back in backs]
    _, alpha_d = stage(cur=(m, jnp.where(rel >= 0, 0.0, NEG_INF)),
                       nxt=(kw_ref[0, key_rows(win_tile[0]), :], q_heads),
                       prv=(vs_ref[jnp.maximum(qi - 1, 0)], sel_acc, alpha))
    prv = (vs_ref[qi], sel_acc, alpha_d)
    m_w = row(NEG_INF)
    for n in range(len(backs)):
        nxt = (kw_ref[0, key_rows(win_tile[n + 1]), :], q_heads) if n + 1 < len(backs) else None
        m_w, alpha_w = stage(cur=(m_w, win_bias[n]), nxt=nxt, prv=prv)
        prv = (vw_ref[win_tile[n]], win_acc, alpha_w)
    stage(prv=prv)

    outs = [oc_ref[r * HEAD_DIM:(r + 1) * HEAD_DIM, :]
            + gate_ref[1, 0, r:r + 1, :] * _normalize(sel_acc[r])
            + gate_ref[2, 0, r:r + 1, :] * _normalize(win_acc[r]) for r in heads]
    o_ref[...] = jnp.concatenate(outs, axis=0).T.astype(o_ref.dtype)


def _nsa(qn_t, mb, oc_t, gates, ksel, expand, vsel, kwin, vwin, b, g_kv, s_len):
    tq = NSA_TILE
    nq = s_len // tq
    rep = NSA_REP
    n_sel = mb.shape[1]
    keys = pl.BlockSpec((1, s_len, HEAD_DIM), lambda bi, gi, qi: (gi * b + bi, 0, 0))
    vals = pl.BlockSpec((nq, VAL_ROWS, tq), lambda bi, gi, qi: (bi, gi, 0))
    return pl.pallas_call(
        _nsa_kernel,
        grid=(b, g_kv, nq),
        in_specs=[
            pl.BlockSpec((rep * HEAD_DIM, tq), lambda bi, gi, qi: (gi, bi * nq + qi)),
            pl.BlockSpec((1, n_sel, tq), lambda bi, gi, qi: (bi * g_kv + gi, 0, qi)),
            pl.BlockSpec((rep * HEAD_DIM, tq), lambda bi, gi, qi: (gi, bi * nq + qi)),
            pl.BlockSpec((3, 1, rep, tq), lambda bi, gi, qi: (0, gi, 0, bi * nq + qi)),
            keys, _resident(expand.shape), vals, keys, vals,
        ],
        out_specs=pl.BlockSpec((tq, rep * HEAD_DIM), lambda bi, gi, qi: (bi * nq + qi, gi)),
        out_shape=jax.ShapeDtypeStruct((b * s_len, NSA_Q_W), BF16),
        scratch_shapes=[pltpu.VMEM((rep, n_sel + HEAD_DIM, tq), BF16), pltpu.VMEM((rep, tq, tq), F32),
                        pltpu.VMEM((rep, tq, tq), BF16), pltpu.VMEM((rep, VAL_ROWS, tq), F32),
                        pltpu.VMEM((rep, VAL_ROWS, tq), F32)],
        compiler_params=_params(3),
        name="nsa",
    )(qn_t, mb, oc_t, gates, ksel.reshape(g_kv * b, s_len, HEAD_DIM), expand, vsel,
      kwin.reshape(g_kv * b, s_len, HEAD_DIM), vwin)


def _swa_kernel(q_ref, sink_ref, k_ref, v_ref, o_ref, s_buf, p_buf, bias_buf, o_buf):
    tq = q_ref.shape[1]
    tv = v_ref.shape[2]
    nk = s_buf.shape[0]
    heads = SWA_HEADS
    qi = pl.program_id(1)
    q0 = qi * tq
    k0 = jnp.maximum(q0 - SWA_WINDOW, 0)

    keys = k_ref[0, pl.ds(pl.multiple_of(k0, tv), nk), :]
    vals = jnp.concatenate([v_ref[k0 // tv + j] for j in range(nk // tv)], axis=1)
    dist = _rel_pos(nk, tq) + (q0 - k0)
    bias_buf[...] = jnp.where((dist >= 0) & (dist < SWA_WINDOW), 0.0, NEG_INF)
    q_head = lambda h: q_ref[h * HEAD_DIM:(h + 1) * HEAD_DIM, :]

    def finish(h, m_all, sink):
        acc = _dot(vals, p_buf[...])
        o_buf[h * HEAD_DIM:(h + 1) * HEAD_DIM, :] = (
            acc[:HEAD_DIM] / (acc[HEAD_DIM:HEAD_DIM + 1] + jnp.exp2(sink - m_all)))

    s_buf[...] = _dot(keys, q_head(0))
    prev = None
    for h in range(heads):
        if h + 1 < heads:
            s_next = _dot(keys, q_head(h + 1))
        if prev is not None:
            finish(*prev)
        sink = sink_ref[:, h * tq:(h + 1) * tq]
        s = s_buf[...] + bias_buf[...]
        m_all = jnp.maximum(jnp.max(s, axis=0, keepdims=True), sink)
        p_buf[...] = jnp.exp2(s - m_all).astype(BF16)
        prev = (h, m_all, sink)
        if h + 1 < heads:
            s_buf[...] = s_next
    finish(*prev)
    o_ref[...] = o_buf[...].T.astype(o_ref.dtype)


def _swa(sq_t, sink_row, k, v, b, s_len):
    tq = min(SWA_TILE, s_len)
    nq = s_len // tq
    tv = v.shape[2]
    nk = tq + SWA_WINDOW
    return pl.pallas_call(
        _swa_kernel,
        grid=(b, nq),
        in_specs=[pl.BlockSpec((SWA_Q_W, tq), lambda bi, qi: (0, bi * nq + qi)),
                  _resident(sink_row.shape),
                  pl.BlockSpec((1, s_len, HEAD_DIM), lambda bi, qi: (bi, 0, 0)),
                  pl.BlockSpec((s_len // tv, VAL_ROWS, tv), lambda bi, qi: (bi, 0, 0))],
        out_specs=pl.BlockSpec((tq, SWA_Q_W), lambda bi, qi: (bi * nq + qi, 0)),
        out_shape=jax.ShapeDtypeStruct((b * s_len, SWA_Q_W), BF16),
        scratch_shapes=[pltpu.VMEM((nk, tq), F32), pltpu.VMEM((nk, tq), BF16), pltpu.VMEM((nk, tq), F32),
                        pltpu.VMEM((SWA_Q_W, tq), F32)],
        compiler_params=_params(2),
        name="swa",
    )(sq_t, sink_row, k.reshape(b, s_len, HEAD_DIM), v)


def _overlap_matrix(n_sel, n_cmp, n_cmp_pad):
    cs = np.arange(n_cmp) * CMP_STRIDE
    ss = np.arange(n_sel) * SEL_LEN
    ov = np.clip(np.minimum(cs[None, :] + CMP_LEN, ss[:, None] + SEL_LEN)
                 - np.maximum(cs[None, :], ss[:, None]), 0, None).astype(np.float32) / CMP_LEN
    return np.pad(ov, ((0, 0), (0, n_cmp_pad - n_cmp)))


def _layer(h, positions, w, b, s_len, norm_final, last):
    n, d = h.shape
    g_kv = NSA_KV
    n_sel = s_len // SEL_LEN
    n_top = min(SEL_TOPN, n_sel)
    nc = s_len // CMP_STRIDE
    bf = lambda a: a.astype(BF16)

    h = _ffn(h, w['norm_ffn1'][None], bf(w['ffn1_gate']), bf(w['ffn1_up']), bf(w['ffn1_down']),
             w['norm_ffn1'][None], False)

    pts = np.cumsum((NSA_Q_W,) + (NSA_KV_W,) * 6 + (NSA_GATE_W, SWA_Q_W, SWA_KV_W, SWA_KV_W, d, d))[:-1]
    (w_nq, w_kc, w_vc, w_ksl, w_vsl, w_kwn, w_vwn, w_ng, w_sq, w_sk, w_sv, w_ga, w_gb) = jnp.split(
        bf(w['w_in']), pts, axis=1)
    parts = dict(qn=w_nq, sq=w_sq, ksel=w_ksl, kwin=w_kwn, kswa=w_sk, vsel=w_vsl, vwin=w_vwn, vswa=w_sv,
                 ng=w_ng, kc=w_kc, vc=w_vc)
    w_t = jnp.concatenate([jnp.pad(parts[name], ((0, 0), (0, width - parts[name].shape[1])))
                           for name, width in PROJ_ROWS], axis=1).T
    w_n = jnp.concatenate([w_ga, w_gb], axis=1)

    freq_col = (ROPE_THETA ** (-jnp.arange(HALF_DIM, dtype=F32) / HALF_DIM))[:, None]
    (qn_t, sq_t, ksel, kwin, kswa, vsel, vwin, vswa, ng_t, kcv, gab) = _proj(
        h, w['norm_mix'][None], w_t, w_n, positions.reshape(1, n), freq_col)

    pos_c = jnp.pad(positions[:, CMP_LEN - 1::CMP_STRIDE], ((0, 0), (0, 1)))[:, None, :]
    w1 = lambda a: bf(a.reshape(CMP_LEN, HEAD_DIM, a.shape[-1]))
    kc, vc_t = _compress(kcv, w['cmp_pe_k'], w['cmp_pe_v'], w1(w['cmp_k_w1']), w1(w['cmp_v_w1']),
                         bf(w['cmp_k_w2'].T), bf(w['cmp_v_w2'].T), pos_c, freq_col, b, g_kv, s_len)

    gates = ng_t.reshape(3, g_kv, NSA_REP, n)
    ov = jnp.asarray(_overlap_matrix(n_sel, nc - 1, nc), BF16)
    mb, oc_t = _cmp_topk(qn_t, kc, vc_t, ov, gates, b, g_kv, s_len, n_top)

    expand = jnp.asarray(np.arange(s_len)[:, None] // SEL_LEN == np.arange(n_sel)[None, :], BF16)
    o_a = _nsa(qn_t, mb, oc_t, gates, ksel, expand, vsel, kwin, vwin, b, g_kv, s_len)

    sink_row = jnp.repeat(w['swa_sinks'].astype(F32) * LOG2E, min(SWA_TILE, s_len))[None]
    o_b = _swa(sq_t, sink_row, kswa, vswa, b, s_len)

    branches = (o_a, o_b, gab, bf(w['w_branch_a']), bf(w['w_branch_b']), bf(w['w_out']))
    return _ffn(h, w['norm_ffn2'][None], bf(w['ffn2_gate']), bf(w['ffn2_up']), bf(w['ffn2_down']),
                norm_final[None], last, branches)


def kernel(x, positions, norm_ffn1, ffn1_gate, ffn1_up, ffn1_down, norm_mix, w_in, cmp_pe_k, cmp_k_w1, cmp_k_w2, cmp_pe_v, cmp_v_w1, cmp_v_w2, swa_sinks, w_branch_a, w_branch_b, w_out, norm_ffn2, ffn2_gate, ffn2_up, ffn2_down, norm_final):
    b, s_len, d = x.shape
    stacked = dict(norm_ffn1=norm_ffn1, ffn1_gate=ffn1_gate, ffn1_up=ffn1_up, ffn1_down=ffn1_down,
                   norm_mix=norm_mix, w_in=w_in, cmp_pe_k=cmp_pe_k, cmp_k_w1=cmp_k_w1, cmp_k_w2=cmp_k_w2,
                   cmp_pe_v=cmp_pe_v, cmp_v_w1=cmp_v_w1, cmp_v_w2=cmp_v_w2, swa_sinks=swa_sinks,
                   w_branch_a=w_branch_a, w_branch_b=w_branch_b, w_out=w_out,
                   norm_ffn2=norm_ffn2, ffn2_gate=ffn2_gate, ffn2_up=ffn2_up, ffn2_down=ffn2_down)
    depth = norm_ffn1.shape[0]
    h = x.reshape(b * s_len, d)
    for i in range(depth):
        w = {k: v[i] for k, v in stacked.items()}
        h = _layer(h, positions, w, b, s_len, norm_final, i == depth - 1)
    return h.reshape(b, s_len, d)
```

```python
import functools

import numpy as np
import jax
import jax.numpy as jnp
from jax import lax
from jax.experimental import pallas as pl
from jax.experimental.pallas import tpu as pltpu

HEAD_DIM = 64
HALF_DIM = HEAD_DIM // 2
NSA_HEADS = 8
NSA_KV = 2
NSA_REP = NSA_HEADS // NSA_KV
SWA_HEADS = 8
CMP_STRIDE = 16
CMP_SHIFT = 4
CMP_LEN = 2 * CMP_STRIDE
SEL_LEN = 64
SEL_SHIFT = 6
SEL_TOPN = 16
SEL_LOCAL = 2
NSA_WINDOW = 512
SWA_WINDOW = 128
ROPE_THETA = 10000.0
RMS_EPS = 1e-6
FFN_HALF = 0.5
NEG_INF = -1e30
FORCE = 1e9
LOG2E = 1.4426950408889634
Q_SCALE = HEAD_DIM ** -0.5 * LOG2E
BF16_ROWS = 16
VAL_ROWS = HEAD_DIM + BF16_ROWS
LANES = 128

NSA_Q_W = NSA_HEADS * HEAD_DIM
NSA_KV_W = NSA_KV * HEAD_DIM
NSA_GATE_W = 3 * NSA_HEADS
SWA_Q_W = SWA_HEADS * HEAD_DIM
SWA_KV_W = HEAD_DIM

VMEM_LIMIT_BYTES = 56 * 1024 * 1024

BF16 = jnp.bfloat16
F32 = jnp.float32

ROW_TILE = 512
CMP_TILE = 1024
CMP_SUBTILE = 256
NSA_TILE = 256
SWA_TILE = 256
SWA_VTILE = 128


def _params(n_axes):
    return pltpu.CompilerParams(dimension_semantics=("arbitrary",) * n_axes,
                                vmem_limit_bytes=VMEM_LIMIT_BYTES)


def _resident(shape):
    zeros = (0,) * len(shape)
    return pl.BlockSpec(shape, lambda *_: zeros, pipeline_mode=pl.Buffered(1))


def _rms(x, g):
    y = x * lax.rsqrt(jnp.mean(x * x, axis=-1, keepdims=True) + RMS_EPS)
    return y * g


def _dot(a, b):
    return jnp.dot(a, b, preferred_element_type=F32)


def _dot_nt(a, b):
    return lax.dot_general(a, b, (((1,), (1,)), ((), ())), preferred_element_type=F32)


def _rope_angles(pos_row, freq_col):
    ang = pos_row.astype(F32) * freq_col
    return jnp.cos(ang), jnp.sin(ang)


def _rope_rows(block, cos_t, sin_t):
    out = []
    for hd in range(block.shape[0] // HEAD_DIM):
        x1 = block[hd * HEAD_DIM:hd * HEAD_DIM + HALF_DIM]
        x2 = block[hd * HEAD_DIM + HALF_DIM:(hd + 1) * HEAD_DIM]
        out += [x1 * cos_t - x2 * sin_t, x2 * cos_t + x1 * sin_t]
    return out


def _ones_rows(tk):
    return jnp.where(lax.broadcasted_iota(jnp.int32, (BF16_ROWS, tk), 0) == 0, 1.0, 0.0).astype(BF16)


def _merge_branches(h_ref, oa_ref, ob_ref, gab_ref, wa_ref, wb_ref, wo_ref):
    d = h_ref.shape[1]
    gab = gab_ref[...]
    merged = gab[:, :d] * _dot(oa_ref[...], wa_ref[...]) + gab[:, d:] * _dot(ob_ref[...], wb_ref[...])
    return h_ref[...] + _dot(merged.astype(BF16), wo_ref[...])


def _ffn_kernel(*refs, final_norm, merge):
    x_ref, g_ref, wg_ref, wu_ref, wd_ref, gf_ref, o_ref = refs[-7:]
    x = _merge_branches(x_ref, *refs[:-7]) if merge else x_ref[...]
    xb = _rms(x, g_ref[...]).astype(BF16)
    a = _dot(xb, wg_ref[...])
    b = _dot(xb, wu_ref[...])
    t = (a * jax.nn.sigmoid(a)) * b
    h = x + FFN_HALF * _dot(t.astype(BF16), wd_ref[...])
    if final_norm:
        h = _rms(h, gf_ref[...])
    o_ref[...] = h


def _ffn(x2, g, wg, wu, wd, gf, final_norm, branches=None):
    n, d = x2.shape
    f = wg.shape[1]
    tm = ROW_TILE
    rows = lambda w: pl.BlockSpec((tm, w), lambda i: (i, 0))
    merge_specs, merge_args = [], ()
    if branches is not None:
        oa, ob, gab, wa, wb, wo = branches
        merge_specs = [rows(oa.shape[1]), rows(ob.shape[1]), rows(2 * d),
                       _resident(wa.shape), _resident(wb.shape), _resident(wo.shape)]
        merge_args = branches
    return pl.pallas_call(
        functools.partial(_ffn_kernel, final_norm=final_norm, merge=branches is not None),
        grid=(n // tm,),
        in_specs=merge_specs + [rows(d), _resident((1, d)), _resident((d, f)), _resident((d, f)),
                                _resident((f, d)), _resident((1, d))],
        out_specs=rows(d),
        out_shape=jax.ShapeDtypeStruct((n, d), F32),
        compiler_params=_params(1),
        name="ffn_final" if final_norm else "ffn",
    )(*merge_args, x2, g, wg, wu, wd, gf)


PROJ_ROWS = (('qn', NSA_Q_W), ('sq', SWA_Q_W), ('ksel', NSA_KV_W), ('kwin', NSA_KV_W), ('kswa', LANES),
             ('vsel', NSA_KV_W), ('vwin', NSA_KV_W), ('vswa', SWA_KV_W), ('ng', 32),
             ('kc', NSA_KV_W), ('vc', NSA_KV_W))


def _proj_kernel(h_ref, g_ref, wt_ref, wn_ref, pos_ref, freq_ref,
                 qn_ref, sq_ref, ksel_ref, kwin_ref, kswa_ref, vsel_ref, vwin_ref, vswa_ref,
                 ng_ref, kcv_ref, gab_ref):
    ub = _rms(h_ref[...], g_ref[...]).astype(BF16)
    yt = _dot_nt(wt_ref[...], ub)
    rows, o = {}, 0
    for name, width in PROJ_ROWS:
        rows[name] = yt[o:o + width]
        o += width
    cos_t, sin_t = _rope_angles(pos_ref[...], freq_ref[...])

    for name, ref in (('qn', qn_ref), ('sq', sq_ref)):
        for i, piece in enumerate(_rope_rows(rows[name], cos_t, sin_t)):
            ref[i * HALF_DIM:(i + 1) * HALF_DIM, :] = (piece * Q_SCALE).astype(BF16)

    for name, ref in (('ksel', ksel_ref), ('kwin', kwin_ref)):
        k_nat = jnp.concatenate(_rope_rows(rows[name], cos_t, sin_t), axis=0).T
        for g in range(NSA_KV):
            ref[g] = k_nat[:, g * HEAD_DIM:(g + 1) * HEAD_DIM].astype(BF16)
    kswa = _rope_rows(rows['kswa'][:HEAD_DIM], cos_t, sin_t) + [rows['kswa'][HEAD_DIM:]]
    kswa_ref[...] = jnp.concatenate(kswa, axis=0).T[:, :HEAD_DIM].astype(BF16)

    for name, ref, groups in (('vsel', vsel_ref, NSA_KV), ('vwin', vwin_ref, NSA_KV), ('vswa', vswa_ref, 1)):
        tk = ref.shape[2]
        for j in range(ref.shape[0]):
            for g in range(groups):
                ref[j, g * VAL_ROWS:g * VAL_ROWS + HEAD_DIM, :] = (
                    rows[name][g * HEAD_DIM:(g + 1) * HEAD_DIM, j * tk:(j + 1) * tk].astype(BF16))
                ref[j, g * VAL_ROWS + HEAD_DIM:(g + 1) * VAL_ROWS, :] = _ones_rows(tk)

    ng_ref[...] = jax.nn.sigmoid(rows['ng'][:NSA_GATE_W])

    for i, name in enumerate(('kc', 'vc')):
        nat = rows[name].T
        for g in range(NSA_KV):
            kcv_ref[i * NSA_KV + g] = nat[:, g * HEAD_DIM:(g + 1) * HEAD_DIM]

    gab_ref[...] = jax.nn.sigmoid(_dot(ub, wn_ref[...]))


def _proj(h2, g, wt, wn, pos_row, freq_col):
    n, d = h2.shape
    tm = ROW_TILE
    rows = lambda w: pl.BlockSpec((tm, w), lambda i: (i, 0))
    cols = lambda w: pl.BlockSpec((w, tm), lambda i: (0, i))
    grouped = lambda k: pl.BlockSpec((k, tm, HEAD_DIM), lambda i: (0, i, 0))
    tiles = lambda groups, tk: pl.BlockSpec((tm // tk, groups * VAL_ROWS, tk), lambda i: (i, 0, 0))
    val_shape = lambda groups, tk: jax.ShapeDtypeStruct((n // tk, groups * VAL_ROWS, tk), BF16)
    out_shape = [
        jax.ShapeDtypeStruct((NSA_Q_W, n), BF16), jax.ShapeDtypeStruct((SWA_Q_W, n), BF16),
        jax.ShapeDtypeStruct((NSA_KV, n, HEAD_DIM), BF16), jax.ShapeDtypeStruct((NSA_KV, n, HEAD_DIM), BF16),
        jax.ShapeDtypeStruct((n, HEAD_DIM), BF16),
        val_shape(NSA_KV, NSA_TILE), val_shape(NSA_KV, NSA_TILE), val_shape(1, SWA_VTILE),
        jax.ShapeDtypeStruct((NSA_GATE_W, n), F32), jax.ShapeDtypeStruct((2 * NSA_KV, n, HEAD_DIM), F32),
        jax.ShapeDtypeStruct((n, 2 * d), F32),
    ]
    return pl.pallas_call(
        _proj_kernel,
        grid=(n // tm,),
        in_specs=[rows(d), _resident((1, d)), _resident(wt.shape), _resident(wn.shape),
                  cols(1), _resident(freq_col.shape)],
        out_specs=[cols(NSA_Q_W), cols(SWA_Q_W), grouped(NSA_KV), grouped(NSA_KV), rows(HEAD_DIM),
                   tiles(NSA_KV, NSA_TILE), tiles(NSA_KV, NSA_TILE), tiles(1, SWA_VTILE),
                   cols(NSA_GATE_W), grouped(2 * NSA_KV), rows(2 * d)],
        out_shape=out_shape,
        compiler_params=_params(1),
        name="proj",
    )(h2, g, wt, wn, pos_row, freq_col)


def _gelu(x):
    return jax.nn.gelu(x, approximate=True)


def _compress_kernel(kc_ref, vc_ref, pek_ref, pev_ref, w1k_ref, w1v_ref, w2kt_ref, w2vt_ref,
                     pos_ref, freq_ref, kc_out, vct_out):
    nc = kc_out.shape[1]

    def hidden(x_ref, pe_ref, w1_ref):
        top = bot = None
        for j in range(CMP_STRIDE):
            x = x_ref[0, pl.ds(j, nc, stride=CMP_STRIDE), :]
            t = _dot((x + pe_ref[j:j + 1, :]).astype(BF16), w1_ref[j])
            b = _dot((x + pe_ref[CMP_STRIDE + j:CMP_STRIDE + j + 1, :]).astype(BF16), w1_ref[CMP_STRIDE + j])
            top, bot = (t, b) if top is None else (top + t, bot + b)
        return _gelu(top + pltpu.roll(bot, shift=nc - 1, axis=0)).astype(BF16)

    kt = _dot_nt(w2kt_ref[...], hidden(kc_ref, pek_ref, w1k_ref))
    cos_t, sin_t = _rope_angles(pos_ref[0], freq_ref[...])
    kt = jnp.concatenate(_rope_rows(kt, cos_t, sin_t) + [jnp.zeros((LANES - HEAD_DIM, nc), F32)], axis=0)
    kc_out[0] = kt.T[:, :HEAD_DIM].astype(BF16)
    vct_out[0] = _dot_nt(w2vt_ref[...], hidden(vc_ref, pev_ref, w1v_ref)).astype(BF16)


def _compress(kcv, pek, pev, w1k, w1v, w2kt, w2vt, pos_c, freq_col, b, g_kv, s_len):
    nc = s_len // CMP_STRIDE
    hid = w1k.shape[-1]
    kcv = kcv.reshape(2 * g_kv * b, s_len, HEAD_DIM)
    src = lambda kind: pl.BlockSpec((1, s_len, HEAD_DIM),
                                    lambda i: ((kind * g_kv + i % g_kv) * b + i // g_kv, 0, 0))
    per = lambda *s: pl.BlockSpec((1,) + s, lambda i: (i,) + (0,) * len(s))
    return pl.pallas_call(
        _compress_kernel,
        grid=(b * g_kv,),
        in_specs=[src(0), src(1), _resident(pek.shape), _resident(pev.shape),
                  _resident(w1k.shape), _resident(w1v.shape), _resident((HEAD_DIM, hid)),
                  _resident((HEAD_DIM, hid)),
                  pl.BlockSpec((1, 1, nc), lambda i: (i // g_kv, 0, 0)), _resident(freq_col.shape)],
        out_specs=[per(nc, HEAD_DIM), per(HEAD_DIM, nc)],
        out_shape=[jax.ShapeDtypeStruct((b * g_kv, nc, HEAD_DIM), BF16),
                   jax.ShapeDtypeStruct((b * g_kv, HEAD_DIM, nc), BF16)],
        compiler_params=_params(1),
        name="compress",
    )(kcv, kcv, pek, pev, w1k, w1v, w2kt, w2vt, pos_c, freq_col)


def _cmp_topk_body(q_ref, kc_ref, vct_ref, ov_ref, gate_ref, mb_ref, oc_ref,
                   s_buf, p_buf, bias_buf, psum_buf, x_buf, *, n_top, nc, n_sel):
    tq = q_ref.shape[1]
    sub = s_buf.shape[1]
    rep = NSA_REP
    qi = pl.program_id(2)
    q0 = qi * tq
    stages = [(u, r) for u in range(tq // sub) for r in range(rep)]
    q_cols = lambda u, r: q_ref[r * HEAD_DIM:(r + 1) * HEAD_DIM, u * sub:(u + 1) * sub]
    keys = kc_ref[0, 0:nc, :]
    vals = vct_ref[0, :, 0:nc]

    s_buf[0:nc] = _dot(keys, q_cols(*stages[0]))
    inv_prev = None
    for n, (u, r) in enumerate(stages):
        cols = slice(u * sub, (u + 1) * sub)
        if n + 1 < len(stages):
            s_next = _dot(keys, q_cols(*stages[n + 1]))
        if n > 0:
            pu, pr = stages[n - 1]
            oc_ref[pr * HEAD_DIM:(pr + 1) * HEAD_DIM, pu * sub:(pu + 1) * sub] = (
                _dot(vals, p_buf[0:nc]) * (inv_prev * gate_ref[0, 0, pr:pr + 1, pu * sub:(pu + 1) * sub]))
        last = ((q0 + u * sub + lax.broadcasted_iota(jnp.int32, (1, sub), 1)) - (CMP_LEN - 1)) >> CMP_SHIFT
        if r == 0:
            bias_buf[0:nc] = jnp.where(lax.broadcasted_iota(jnp.int32, (nc, sub), 0) <= last, 0.0, NEG_INF)
        s = s_buf[0:nc] + bias_buf[0:nc]
        m = jnp.max(s, axis=0, keepdims=True)
        e = jnp.exp2(s - m)
        inv_prev = jnp.where(last >= 0, 1.0 / jnp.sum(e, axis=0, keepdims=True), 0.0)
        p_buf[0:nc] = e.astype(BF16)
        if r == 0:
            psum_buf[0:nc, cols] = e * inv_prev
        else:
            psum_buf[0:nc, cols] += e * inv_prev
        if n + 1 < len(stages):
            s_buf[0:nc] = s_next
    pu, pr = stages[-1]
    oc_ref[pr * HEAD_DIM:(pr + 1) * HEAD_DIM, pu * sub:(pu + 1) * sub] = (
        _dot(vals, p_buf[0:nc]) * (inv_prev * gate_ref[0, 0, pr:pr + 1, pu * sub:(pu + 1) * sub]))

    psum = psum_buf[0:nc, :]
    p_hi = psum.astype(BF16)
    p_lo = (psum - p_hi.astype(F32)).astype(BF16)
    pool = ov_ref[0:n_sel, 0:nc]
    imp = _dot(pool, p_hi) + _dot(pool, p_lo)

    blk = lax.broadcasted_iota(jnp.int32, (n_sel, tq), 0)
    tb = (q0 + lax.broadcasted_iota(jnp.int32, (n_sel, tq), 1)) >> SEL_SHIFT
    forced = (blk == 0) | ((tb - blk >= 0) & (tb - blk < SEL_LOCAL))
    n_forced = 1 + SEL_LOCAL
    premark = forced & (jnp.full((n_sel, tq), qi, jnp.int32) > 0)
    x_buf[0:n_sel] = jnp.where(premark, -jnp.inf, jnp.where(forced, FORCE, jnp.where(blk > tb, -FORCE, imp)))

    blk_f = blk.astype(F32)

    def pick(_, carry):
        x = x_buf[0:n_sel]
        top = jnp.max(x, axis=0, keepdims=True)
        first = jnp.min(jnp.where(x == top, blk_f, float(n_sel)), axis=0, keepdims=True)
        x_buf[0:n_sel] = jnp.where(blk_f == first, -jnp.inf, x)
        return carry

    lax.fori_loop(0, jnp.where(qi > 0, n_top - n_forced, n_top), pick, 0)
    mb_ref[0, 0:n_sel, :] = jnp.where(x_buf[0:n_sel] == -jnp.inf, 0.0, NEG_INF).astype(BF16)
    if n_sel < mb_ref.shape[1]:
        mb_ref[0, n_sel:, :] = jnp.full((mb_ref.shape[1] - n_sel, tq), NEG_INF, BF16)


def _cmp_topk_kernel(*refs, n_top, n_tiles):
    nc = refs[1].shape[1]
    n_sel = refs[3].shape[0]
    qi = pl.program_id(2)
    for v in range(max(n_tiles // 2, 1)):
        frac = lambda total: min(total, (2 * v + 2) * total // n_tiles)
        pl.when((qi >> 1) == v)(functools.partial(_cmp_topk_body, *refs, n_top=n_top, nc=frac(nc), n_sel=frac(n_sel)))


def _cmp_topk(qn_t, kc, vct, ov, gates, b, g_kv, s_len, n_top):
    tq = min(CMP_TILE, s_len)
    sub = min(CMP_SUBTILE, tq)
    nq = s_len // tq
    nc = kc.shape[1]
    n_sel = ov.shape[0]
    rep = NSA_REP
    return pl.pallas_call(
        functools.partial(_cmp_topk_kernel, n_top=n_top, n_tiles=nq),
        grid=(b, g_kv, nq),
        in_specs=[
            pl.BlockSpec((rep * HEAD_DIM, tq), lambda bi, gi, qi: (gi, bi * nq + qi)),
            pl.BlockSpec((1, nc, HEAD_DIM), lambda bi, gi, qi: (bi * g_kv + gi, 0, 0)),
            pl.BlockSpec((1, HEAD_DIM, nc), lambda bi, gi, qi: (bi * g_kv + gi, 0, 0)),
            _resident(ov.shape),
            pl.BlockSpec((1, 1, rep, tq), lambda bi, gi, qi: (0, gi, 0, bi * nq + qi)),
        ],
        out_specs=[
            pl.BlockSpec((1, n_sel, tq), lambda bi, gi, qi: (bi * g_kv + gi, 0, qi)),
            pl.BlockSpec((rep * HEAD_DIM, tq), lambda bi, gi, qi: (gi, bi * nq + qi)),
        ],
        out_shape=[jax.ShapeDtypeStruct((b * g_kv, n_sel, s_len), BF16),
                   jax.ShapeDtypeStruct((NSA_Q_W, b * s_len), F32)],
        scratch_shapes=[pltpu.VMEM((nc, sub), F32), pltpu.VMEM((nc, sub), BF16), pltpu.VMEM((nc, sub), F32),
                        pltpu.VMEM((nc, tq), F32), pltpu.VMEM((n_sel, tq), F32)],
        compiler_params=_params(3),
        name="cmp_topk",
    )(qn_t, kc, vct, ov, gates)


def _softmax_tile(m, s):
    m_new = jnp.maximum(m, jnp.max(s, axis=0, keepdims=True))
    return m_new, jnp.exp2(m - m_new), jnp.exp2(s - m_new).astype(BF16)


def _normalize(acc):
    return acc[:HEAD_DIM] / acc[HEAD_DIM:HEAD_DIM + 1]


def _rel_pos(tk, tq):
    return lax.broadcasted_iota(jnp.int32, (tk, tq), 1) - lax.broadcasted_iota(jnp.int32, (tk, tq), 0)


def _window_bias(rel, back, tk, kj, window):
    dist = rel + (back * tk + jnp.where(kj >= 0, 0, window))
    return jnp.where((dist >= 0) & (dist < window), 0.0, NEG_INF)


def _pipe_stage(s_buf, p_buf, heads, cur=None, nxt=None, prv=None):
    if nxt is not None:
        s_next = [_dot(nxt[0], nxt[1][r][...]) for r in heads]
    if prv is not None:
        pv = [_dot(prv[0], p_buf[r]) for r in heads]
    out = None
    if cur is not None:
        m, bias = cur
        out = []
        for r in heads:
            s = s_buf[r] if bias is None else s_buf[r] + bias
            m_r, a_r, p_buf[r] = _softmax_tile(m[r], s)
            out.append((m_r, a_r))
        out = tuple(zip(*out))
    for r in heads:
        if prv is not None:
            prv[1][r] = prv[2][r] * prv[1][r] + pv[r]
        if nxt is not None:
            s_buf[r] = s_next[r]
    return out


def _nsa_kernel(q_ref, mb_ref, oc_ref, gate_ref, ks_ref, ex_ref, vs_ref, kw_ref, vw_ref, o_ref,
                qa_buf, s_buf, p_buf, sel_acc, win_acc):
    tq = q_ref.shape[1]
    tk = tq
    n_sel = mb_ref.shape[1]
    rep = NSA_REP
    heads = range(rep)
    qi = pl.program_id(2)
    stage = functools.partial(_pipe_stage, s_buf, p_buf, heads)
    key_rows = lambda j: pl.ds(pl.multiple_of(j * tk, tk), tk)

    def sel_keys(j):
        return jnp.concatenate([ex_ref[key_rows(j), :], ks_ref[0, key_rows(j), :]], axis=1)

    q_heads = [q_ref.at[pl.ds(r * HEAD_DIM, HEAD_DIM), :] for r in heads]
    q_aug = [qa_buf.at[r] for r in heads]
    for r in heads:
        qa_buf[r, 0:n_sel, :] = mb_ref[0]
        qa_buf[r, n_sel:n_sel + HEAD_DIM, :] = q_heads[r][...]
        s_buf[r] = _dot(sel_keys(0), qa_buf[r])
        p_buf[r] = jnp.zeros((tk, tq), BF16)
        sel_acc[r] = jnp.zeros((VAL_ROWS, tq), F32)
        win_acc[r] = jnp.zeros((VAL_ROWS, tq), F32)
    row = lambda v: tuple(jnp.full((1, tq), v, F32) for _ in heads)

    def sel_stage(i, carry):
        m, alpha = carry
        return stage(cur=(m, None), nxt=(sel_keys(i + 1), q_aug),
                     prv=(vs_ref[jnp.maximum(i - 1, 0)], sel_acc, alpha))

    carry = lax.fori_loop(0, qi >> 1, lambda j, c: sel_stage(2 * j + 1, sel_stage(2 * j, c)),
                          (row(NEG_INF), row(1.0)))
    m, alpha = lax.fori_loop(0, qi & 1, lambda _, c: sel_stage(qi - 1, c), carry)

    rel = _rel_pos(tk, tq)
    backs = list(range(NSA_WINDOW // tk, -1, -1))
    win_tile = [jnp.maximum(qi - back, 0) for back in backs]
    win_bias = [_window_bias(rel, back, tk, qi - back, NSA_WINDOW) for back in backs]
    _, alpha_d = stage(cur=(m, jnp.where(rel >= 0, 0.0, NEG_INF)),
                       nxt=(kw_ref[0, key_rows(win_tile[0]), :], q_heads),
                       prv=(vs_ref[jnp.maximum(qi - 1, 0)], sel_acc, alpha))
    prv = (vs_ref[qi], sel_acc, alpha_d)
    m_w = row(NEG_INF)
    for n in range(len(backs)):
        nxt = (kw_ref[0, key_rows(win_tile[n + 1]), :], q_heads) if n + 1 < len(backs) else None
        m_w, alpha_w = stage(cur=(m_w, win_bias[n]), nxt=nxt, prv=prv)
        prv = (vw_ref[win_tile[n]], win_acc, alpha_w)
    stage(prv=prv)

    outs = [oc_ref[r * HEAD_DIM:(r + 1) * HEAD_DIM, :]
            + gate_ref[1, 0, r:r + 1, :] * _normalize(sel_acc[r])
            + gate_ref[2, 0, r:r + 1, :] * _normalize(win_acc[r]) for r in heads]
    o_ref[...] = jnp.concatenate(outs, axis=0).T.astype(o_ref.dtype)


def _nsa(qn_t, mb, oc_t, gates, ksel, expand, vsel, kwin, vwin, b, g_kv, s_len):
    tq = NSA_TILE
    nq = s_len // tq
    rep = NSA_REP
    n_sel = mb.shape[1]
    keys = pl.BlockSpec((1, s_len, HEAD_DIM), lambda bi, gi, qi: (gi * b + bi, 0, 0))
    vals = pl.BlockSpec((nq, VAL_ROWS, tq), lambda bi, gi, qi: (bi, gi, 0))
    return pl.pallas_call(
        _nsa_kernel,
        grid=(b, g_kv, nq),
        in_specs=[
            pl.BlockSpec((rep * HEAD_DIM, tq), lambda bi, gi, qi: (gi, bi * nq + qi)),
            pl.BlockSpec((1, n_sel, tq), lambda bi, gi, qi: (bi * g_kv + gi, 0, qi)),
            pl.BlockSpec((rep * HEAD_DIM, tq), lambda bi, gi, qi: (gi, bi * nq + qi)),
            pl.BlockSpec((3, 1, rep, tq), lambda bi, gi, qi: (0, gi, 0, bi * nq + qi)),
            keys, _resident(expand.shape), vals, keys, vals,
        ],
        out_specs=pl.BlockSpec((tq, rep * HEAD_DIM), lambda bi, gi, qi: (bi * nq + qi, gi)),
        out_shape=jax.ShapeDtypeStruct((b * s_len, NSA_Q_W), BF16),
        scratch_shapes=[pltpu.VMEM((rep, n_sel + HEAD_DIM, tq), BF16), pltpu.VMEM((rep, tq, tq), F32),
                        pltpu.VMEM((rep, tq, tq), BF16), pltpu.VMEM((rep, VAL_ROWS, tq), F32),
                        pltpu.VMEM((rep, VAL_ROWS, tq), F32)],
        compiler_params=_params(3),
        name="nsa",
    )(qn_t, mb, oc_t, gates, ksel.reshape(g_kv * b, s_len, HEAD_DIM), expand, vsel,
      kwin.reshape(g_kv * b, s_len, HEAD_DIM), vwin)


def _swa_kernel(q_ref, sink_ref, k_ref, v_ref, o_ref, s_buf, p_buf, bias_buf, o_buf):
    tq = q_ref.shape[1]
    tv = v_ref.shape[2]
    nk = s_buf.shape[0]
    heads = SWA_HEADS
    qi = pl.program_id(1)
    q0 = qi * tq
    k0 = jnp.maximum(q0 - SWA_WINDOW, 0)

    keys = k_ref[0, pl.ds(pl.multiple_of(k0, tv), nk), :]
    vals = jnp.concatenate([v_ref[k0 // tv + j] for j in range(nk // tv)], axis=1)
    dist = _rel_pos(nk, tq) + (q0 - k0)
    bias_buf[...] = jnp.where((dist >= 0) & (dist < SWA_WINDOW), 0.0, NEG_INF)
    q_head = lambda h: q_ref[h * HEAD_DIM:(h + 1) * HEAD_DIM, :]

    def finish(h, m_all, sink):
        acc = _dot(vals, p_buf[...])
        o_buf[h * HEAD_DIM:(h + 1) * HEAD_DIM, :] = (
            acc[:HEAD_DIM] / (acc[HEAD_DIM:HEAD_DIM + 1] + jnp.exp2(sink - m_all)))

    s_buf[...] = _dot(keys, q_head(0))
    prev = None
    for h in range(heads):
        if h + 1 < heads:
            s_next = _dot(keys, q_head(h + 1))
        if prev is not None:
            finish(*prev)
        sink = sink_ref[:, h * tq:(h + 1) * tq]
        s = s_buf[...] + bias_buf[...]
        m_all = jnp.maximum(jnp.max(s, axis=0, keepdims=True), sink)
        p_buf[...] = jnp.exp2(s - m_all).astype(BF16)
        prev = (h, m_all, sink)
        if h + 1 < heads:
            s_buf[...] = s_next
    finish(*prev)
    o_ref[...] = o_buf[...].T.astype(o_ref.dtype)


def _swa(sq_t, sink_row, k, v, b, s_len):
    tq = min(SWA_TILE, s_len)
    nq = s_len // tq
    tv = v.shape[2]
    nk = tq + SWA_WINDOW
    return pl.pallas_call(
        _swa_kernel,
        grid=(b, nq),
        in_specs=[pl.BlockSpec((SWA_Q_W, tq), lambda bi, qi: (0, bi * nq + qi)),
                  _resident(sink_row.shape),
                  pl.BlockSpec((1, s_len, HEAD_DIM), lambda bi, qi: (bi, 0, 0)),
                  pl.BlockSpec((s_len // tv, VAL_ROWS, tv), lambda bi, qi: (bi, 0, 0))],
        out_specs=pl.BlockSpec((tq, SWA_Q_W), lambda bi, qi: (bi * nq + qi, 0)),
        out_shape=jax.ShapeDtypeStruct((b * s_len, SWA_Q_W), BF16),
        scratch_shapes=[pltpu.VMEM((nk, tq), F32), pltpu.VMEM((nk, tq), BF16), pltpu.VMEM((nk, tq), F32),
                        pltpu.VMEM((SWA_Q_W, tq), F32)],
        compiler_params=_params(2),
        name="swa",
    )(sq_t, sink_row, k.reshape(b, s_len, HEAD_DIM), v)


def _overlap_matrix(n_sel, n_cmp, n_cmp_pad):
    cs = np.arange(n_cmp) * CMP_STRIDE
    ss = np.arange(n_sel) * SEL_LEN
    ov = np.clip(np.minimum(cs[None, :] + CMP_LEN, ss[:, None] + SEL_LEN)
                 - np.maximum(cs[None, :], ss[:, None]), 0, None).astype(np.float32) / CMP_LEN
    return np.pad(ov, ((0, 0), (0, n_cmp_pad - n_cmp)))


def _layer(h, positions, w, b, s_len, norm_final, last):
    n, d = h.shape
    g_kv = NSA_KV
    n_sel = s_len // SEL_LEN
    n_top = min(SEL_TOPN, n_sel)
    nc = s_len // CMP_STRIDE
    bf = lambda a: a.astype(BF16)

    h = _ffn(h, w['norm_ffn1'][None], bf(w['ffn1_gate']), bf(w['ffn1_up']), bf(w['ffn1_down']),
             w['norm_ffn1'][None], False)

    pts = np.cumsum((NSA_Q_W,) + (NSA_KV_W,) * 6 + (NSA_GATE_W, SWA_Q_W, SWA_KV_W, SWA_KV_W, d, d))[:-1]
    (w_nq, w_kc, w_vc, w_ksl, w_vsl, w_kwn, w_vwn, w_ng, w_sq, w_sk, w_sv, w_ga, w_gb) = jnp.split(
        bf(w['w_in']), pts, axis=1)
    parts = dict(qn=w_nq, sq=w_sq, ksel=w_ksl, kwin=w_kwn, kswa=w_sk, vsel=w_vsl, vwin=w_vwn, vswa=w_sv,
                 ng=w_ng, kc=w_kc, vc=w_vc)
    w_t = jnp.concatenate([jnp.pad(parts[name], ((0, 0), (0, width - parts[name].shape[1])))
                           for name, width in PROJ_ROWS], axis=1).T
    w_n = jnp.concatenate([w_ga, w_gb], axis=1)

    freq_col = (ROPE_THETA ** (-jnp.arange(HALF_DIM, dtype=F32) / HALF_DIM))[:, None]
    (qn_t, sq_t, ksel, kwin, kswa, vsel, vwin, vswa, ng_t, kcv, gab) = _proj(
        h, w['norm_mix'][None], w_t, w_n, positions.reshape(1, n), freq_col)

    pos_c = jnp.pad(positions[:, CMP_LEN - 1::CMP_STRIDE], ((0, 0), (0, 1)))[:, None, :]
    w1 = lambda a: bf(a.reshape(CMP_LEN, HEAD_DIM, a.shape[-1]))
    kc, vc_t = _compress(kcv, w['cmp_pe_k'], w['cmp_pe_v'], w1(w['cmp_k_w1']), w1(w['cmp_v_w1']),
                         bf(w['cmp_k_w2'].T), bf(w['cmp_v_w2'].T), pos_c, freq_col, b, g_kv, s_len)

    gates = ng_t.reshape(3, g_kv, NSA_REP, n)
    ov = jnp.asarray(_overlap_matrix(n_sel, nc - 1, nc), BF16)
    mb, oc_t = _cmp_topk(qn_t, kc, vc_t, ov, gates, b, g_kv, s_len, n_top)

    expand = jnp.asarray(np.arange(s_len)[:, None] // SEL_LEN == np.arange(n_sel)[None, :], BF16)
    o_a = _nsa(qn_t, mb, oc_t, gates, ksel, expand, vsel, kwin, vwin, b, g_kv, s_len)

    sink_row = jnp.repeat(w['swa_sinks'].astype(F32) * LOG2E, min(SWA_TILE, s_len))[None]
    o_b = _swa(sq_t, sink_row, kswa, vswa, b, s_len)

    branches = (o_a, o_b, gab, bf(w['w_branch_a']), bf(w['w_branch_b']), bf(w['w_out']))
    return _ffn(h, w['norm_ffn2'][None], bf(w['ffn2_gate']), bf(w['ffn2_up']), bf(w['ffn2_down']),
                norm_final[None], last, branches)


def kernel(x, positions, norm_ffn1, ffn1_gate, ffn1_up, ffn1_down, norm_mix, w_in, cmp_pe_k, cmp_k_w1, cmp_k_w2, cmp_pe_v, cmp_v_w1, cmp_v_w2, swa_sinks, w_branch_a, w_branch_b, w_out, norm_ffn2, ffn2_gate, ffn2_up, ffn2_down, norm_final):
    b, s_len, d = x.shape
    stacked = dict(norm_ffn1=norm_ffn1, ffn1_gate=ffn1_gate, ffn1_up=ffn1_up, ffn1_down=ffn1_down,
                   norm_mix=norm_mix, w_in=w_in, cmp_pe_k=cmp_pe_k, cmp_k_w1=cmp_k_w1, cmp_k_w2=cmp_k_w2,
                   cmp_pe_v=cmp_pe_v, cmp_v_w1=cmp_v_w1, cmp_v_w2=cmp_v_w2, swa_sinks=swa_sinks,
                   w_branch_a=w_branch_a, w_branch_b=w_branch_b, w_out=w_out,
                   norm_ffn2=norm_ffn2, ffn2_gate=ffn2_gate, ffn2_up=ffn2_up, ffn2_down=ffn2_down)
    depth = norm_ffn1.shape[0]
    h = x.reshape(b * s_len, d)
    for i in range(depth):
        w = {k: v[i] for k, v in stacked.items()}
        h = _layer(h, positions, w, b, s_len, norm_final, i == depth - 1)
    return h.reshape(b, s_len, d)
```

```python
import functools

import numpy as np
import jax
import jax.numpy as jnp
from jax import lax
from jax.experimental import pallas as pl
from jax.experimental.pallas import tpu as pltpu

HEAD_DIM = 64
HALF_DIM = HEAD_DIM // 2
NSA_HEADS = 8
NSA_KV = 2
NSA_REP = NSA_HEADS // NSA_KV
SWA_HEADS = 8
CMP_STRIDE = 16
CMP_SHIFT = 4
CMP_LEN = 2 * CMP_STRIDE
SEL_LEN = 64
SEL_SHIFT = 6
SEL_TOPN = 16
SEL_LOCAL = 2
NSA_WINDOW = 512
SWA_WINDOW = 128
ROPE_THETA = 10000.0
RMS_EPS = 1e-6
FFN_HALF = 0.5
NEG_INF = -1e30
FORCE = 1e9
LOG2E = 1.4426950408889634
Q_SCALE = HEAD_DIM ** -0.5 * LOG2E
BF16_ROWS = 16
VAL_ROWS = HEAD_DIM + BF16_ROWS
LANES = 128

NSA_Q_W = NSA_HEADS * HEAD_DIM
NSA_KV_W = NSA_KV * HEAD_DIM
NSA_GATE_W = 3 * NSA_HEADS
SWA_Q_W = SWA_HEADS * HEAD_DIM
SWA_KV_W = HEAD_DIM

VMEM_LIMIT_BYTES = 56 * 1024 * 1024

BF16 = jnp.bfloat16
F32 = jnp.float32

ROW_TILE = 512
CMP_TILE = 1024
CMP_SUBTILE = 256
NSA_TILE = 256
NSA_STEP = 512
SWA_TILE = 256
SWA_VTILE = 128


def _params(n_axes):
    return pltpu.CompilerParams(dimension_semantics=("arbitrary",) * n_axes,
                                vmem_limit_bytes=VMEM_LIMIT_BYTES)


def _resident(shape):
    zeros = (0,) * len(shape)
    return pl.BlockSpec(shape, lambda *_: zeros, pipeline_mode=pl.Buffered(1))


def _rms(x, g):
    y = x * lax.rsqrt(jnp.mean(x * x, axis=-1, keepdims=True) + RMS_EPS)
    return y * g


def _dot(a, b):
    return jnp.dot(a, b, preferred_element_type=F32)


def _dot_nt(a, b):
    return lax.dot_general(a, b, (((1,), (1,)), ((), ())), preferred_element_type=F32)


def _rope_angles(pos_row, freq_col):
    ang = pos_row.astype(F32) * freq_col
    return jnp.cos(ang), jnp.sin(ang)


def _rope_rows(block, cos_t, sin_t):
    out = []
    for hd in range(block.shape[0] // HEAD_DIM):
        x1 = block[hd * HEAD_DIM:hd * HEAD_DIM + HALF_DIM]
        x2 = block[hd * HEAD_DIM + HALF_DIM:(hd + 1) * HEAD_DIM]
        out += [x1 * cos_t - x2 * sin_t, x2 * cos_t + x1 * sin_t]
    return out


def _ones_rows(tk):
    return jnp.where(lax.broadcasted_iota(jnp.int32, (BF16_ROWS, tk), 0) == 0, 1.0, 0.0).astype(BF16)


def _merge_branches(h_ref, oa_ref, ob_ref, gab_ref, wa_ref, wb_ref, wo_ref):
    d = h_ref.shape[1]
    gab = gab_ref[...]
    merged = gab[:, :d] * _dot(oa_ref[...], wa_ref[...]) + gab[:, d:] * _dot(ob_ref[...], wb_ref[...])
    return h_ref[...] + _dot(merged.astype(BF16), wo_ref[...])


def _ffn_kernel(*refs, final_norm, merge):
    x_ref, g_ref, wg_ref, wu_ref, wd_ref, gf_ref, o_ref = refs[-7:]
    x = _merge_branches(x_ref, *refs[:-7]) if merge else x_ref[...]
    xb = _rms(x, g_ref[...]).astype(BF16)
    a = _dot(xb, wg_ref[...])
    b = _dot(xb, wu_ref[...])
    t = (a * jax.nn.sigmoid(a)) * b
    h = x + FFN_HALF * _dot(t.astype(BF16), wd_ref[...])
    if final_norm:
        h = _rms(h, gf_ref[...])
    o_ref[...] = h


def _ffn(x2, g, wg, wu, wd, gf, final_norm, branches=None):
    n, d = x2.shape
    f = wg.shape[1]
    tm = ROW_TILE
    rows = lambda w: pl.BlockSpec((tm, w), lambda i: (i, 0))
    merge_specs, merge_args = [], ()
    if branches is not None:
        oa, ob, gab, wa, wb, wo = branches
        merge_specs = [rows(oa.shape[1]), rows(ob.shape[1]), rows(2 * d),
                       _resident(wa.shape), _resident(wb.shape), _resident(wo.shape)]
        merge_args = branches
    return pl.pallas_call(
        functools.partial(_ffn_kernel, final_norm=final_norm, merge=branches is not None),
        grid=(n // tm,),
        in_specs=merge_specs + [rows(d), _resident((1, d)), _resident((d, f)), _resident((d, f)),
                                _resident((f, d)), _resident((1, d))],
        out_specs=rows(d),
        out_shape=jax.ShapeDtypeStruct((n, d), F32),
        compiler_params=_params(1),
        name="ffn_final" if final_norm else "ffn",
    )(*merge_args, x2, g, wg, wu, wd, gf)


PROJ_ROWS = (('qn', NSA_Q_W), ('sq', SWA_Q_W), ('ksel', NSA_KV_W), ('kwin', NSA_KV_W), ('kswa', LANES),
             ('vsel', NSA_KV_W), ('vwin', NSA_KV_W), ('vswa', SWA_KV_W), ('ng', 32),
             ('kc', NSA_KV_W), ('vc', NSA_KV_W))


def _proj_kernel(h_ref, g_ref, wt_ref, wn_ref, pos_ref, freq_ref,
                 qn_ref, sq_ref, ksel_ref, kwin_ref, kswa_ref, vsel_ref, vwin_ref, vswa_ref,
                 ng_ref, kcv_ref, gab_ref):
    ub = _rms(h_ref[...], g_ref[...]).astype(BF16)
    yt = _dot_nt(wt_ref[...], ub)
    rows, o = {}, 0
    for name, width in PROJ_ROWS:
        rows[name] = yt[o:o + width]
        o += width
    cos_t, sin_t = _rope_angles(pos_ref[...], freq_ref[...])

    for name, ref in (('qn', qn_ref), ('sq', sq_ref)):
        for i, piece in enumerate(_rope_rows(rows[name], cos_t, sin_t)):
            ref[i * HALF_DIM:(i + 1) * HALF_DIM, :] = (piece * Q_SCALE).astype(BF16)

    for name, ref in (('ksel', ksel_ref), ('kwin', kwin_ref)):
        k_nat = jnp.concatenate(_rope_rows(rows[name], cos_t, sin_t), axis=0).T
        for g in range(NSA_KV):
            ref[g] = k_nat[:, g * HEAD_DIM:(g + 1) * HEAD_DIM].astype(BF16)
    kswa = _rope_rows(rows['kswa'][:HEAD_DIM], cos_t, sin_t) + [rows['kswa'][HEAD_DIM:]]
    kswa_ref[...] = jnp.concatenate(kswa, axis=0).T[:, :HEAD_DIM].astype(BF16)

    for name, ref, groups in (('vsel', vsel_ref, NSA_KV), ('vwin', vwin_ref, NSA_KV), ('vswa', vswa_ref, 1)):
        tk = ref.shape[2]
        for j in range(ref.shape[0]):
            for g in range(groups):
                ref[j, g * VAL_ROWS:g * VAL_ROWS + HEAD_DIM, :] = (
                    rows[name][g * HEAD_DIM:(g + 1) * HEAD_DIM, j * tk:(j + 1) * tk].astype(BF16))
                ref[j, g * VAL_ROWS + HEAD_DIM:(g + 1) * VAL_ROWS, :] = _ones_rows(tk)

    ng_ref[...] = jax.nn.sigmoid(rows['ng'][:NSA_GATE_W])

    for i, name in enumerate(('kc', 'vc')):
        nat = rows[name].T
        for g in range(NSA_KV):
            kcv_ref[i * NSA_KV + g] = nat[:, g * HEAD_DIM:(g + 1) * HEAD_DIM]

    gab_ref[...] = jax.nn.sigmoid(_dot(ub, wn_ref[...]))


def _proj(h2, g, wt, wn, pos_row, freq_col):
    n, d = h2.shape
    tm = ROW_TILE
    rows = lambda w: pl.BlockSpec((tm, w), lambda i: (i, 0))
    cols = lambda w: pl.BlockSpec((w, tm), lambda i: (0, i))
    grouped = lambda k: pl.BlockSpec((k, tm, HEAD_DIM), lambda i: (0, i, 0))
    tiles = lambda groups, tk: pl.BlockSpec((tm // tk, groups * VAL_ROWS, tk), lambda i: (i, 0, 0))
    val_shape = lambda groups, tk: jax.ShapeDtypeStruct((n // tk, groups * VAL_ROWS, tk), BF16)
    out_shape = [
        jax.ShapeDtypeStruct((NSA_Q_W, n), BF16), jax.ShapeDtypeStruct((SWA_Q_W, n), BF16),
        jax.ShapeDtypeStruct((NSA_KV, n, HEAD_DIM), BF16), jax.ShapeDtypeStruct((NSA_KV, n, HEAD_DIM), BF16),
        jax.ShapeDtypeStruct((n, HEAD_DIM), BF16),
        val_shape(NSA_KV, NSA_TILE), val_shape(NSA_KV, NSA_TILE), val_shape(1, SWA_VTILE),
        jax.ShapeDtypeStruct((NSA_GATE_W, n), F32), jax.ShapeDtypeStruct((2 * NSA_KV, n, HEAD_DIM), F32),
        jax.ShapeDtypeStruct((n, 2 * d), F32),
    ]
    return pl.pallas_call(
        _proj_kernel,
        grid=(n // tm,),
        in_specs=[rows(d), _resident((1, d)), _resident(wt.shape), _resident(wn.shape),
                  cols(1), _resident(freq_col.shape)],
        out_specs=[cols(NSA_Q_W), cols(SWA_Q_W), grouped(NSA_KV), grouped(NSA_KV), rows(HEAD_DIM),
                   tiles(NSA_KV, NSA_TILE), tiles(NSA_KV, NSA_TILE), tiles(1, SWA_VTILE),
                   cols(NSA_GATE_W), grouped(2 * NSA_KV), rows(2 * d)],
        out_shape=out_shape,
        compiler_params=_params(1),
        name="proj",
    )(h2, g, wt, wn, pos_row, freq_col)


def _gelu(x):
    return jax.nn.gelu(x, approximate=True)


def _compress_kernel(kc_ref, vc_ref, pek_ref, pev_ref, w1k_ref, w1v_ref, w2kt_ref, w2vt_ref,
                     pos_ref, freq_ref, kc_out, vct_out):
    nc = kc_out.shape[1]

    def hidden(x_ref, pe_ref, w1_ref):
        top = bot = None
        for j in range(CMP_STRIDE):
            x = x_ref[0, pl.ds(j, nc, stride=CMP_STRIDE), :]
            t = _dot((x + pe_ref[j:j + 1, :]).astype(BF16), w1_ref[j])
            b = _dot((x + pe_ref[CMP_STRIDE + j:CMP_STRIDE + j + 1, :]).astype(BF16), w1_ref[CMP_STRIDE + j])
            top, bot = (t, b) if top is None else (top + t, bot + b)
        return _gelu(top + pltpu.roll(bot, shift=nc - 1, axis=0)).astype(BF16)

    kt = _dot_nt(w2kt_ref[...], hidden(kc_ref, pek_ref, w1k_ref))
    cos_t, sin_t = _rope_angles(pos_ref[0], freq_ref[...])
    kt = jnp.concatenate(_rope_rows(kt, cos_t, sin_t) + [jnp.zeros((LANES - HEAD_DIM, nc), F32)], axis=0)
    kc_out[0] = kt.T[:, :HEAD_DIM].astype(BF16)
    vct_out[0] = _dot_nt(w2vt_ref[...], hidden(vc_ref, pev_ref, w1v_ref)).astype(BF16)


def _compress(kcv, pek, pev, w1k, w1v, w2kt, w2vt, pos_c, freq_col, b, g_kv, s_len):
    nc = s_len // CMP_STRIDE
    hid = w1k.shape[-1]
    kcv = kcv.reshape(2 * g_kv * b, s_len, HEAD_DIM)
    src = lambda kind: pl.BlockSpec((1, s_len, HEAD_DIM),
                                    lambda i: ((kind * g_kv + i % g_kv) * b + i // g_kv, 0, 0))
    per = lambda *s: pl.BlockSpec((1,) + s, lambda i: (i,) + (0,) * len(s))
    return pl.pallas_call(
        _compress_kernel,
        grid=(b * g_kv,),
        in_specs=[src(0), src(1), _resident(pek.shape), _resident(pev.shape),
                  _resident(w1k.shape), _resident(w1v.shape), _resident((HEAD_DIM, hid)),
                  _resident((HEAD_DIM, hid)),
                  pl.BlockSpec((1, 1, nc), lambda i: (i // g_kv, 0, 0)), _resident(freq_col.shape)],
        out_specs=[per(nc, HEAD_DIM), per(HEAD_DIM, nc)],
        out_shape=[jax.ShapeDtypeStruct((b * g_kv, nc, HEAD_DIM), BF16),
                   jax.ShapeDtypeStruct((b * g_kv, HEAD_DIM, nc), BF16)],
        compiler_params=_params(1),
        name="compress",
    )(kcv, kcv, pek, pev, w1k, w1v, w2kt, w2vt, pos_c, freq_col)


def _cmp_topk_body(q_ref, kc_ref, vct_ref, ov_ref, gate_ref, mb_ref, oc_ref,
                   s_buf, p_buf, bias_buf, psum_buf, x_buf, *, n_top, nc, n_sel):
    tq = q_ref.shape[1]
    sub = s_buf.shape[1]
    rep = NSA_REP
    qi = pl.program_id(2)
    q0 = qi * tq
    stages = [(u, r) for u in range(tq // sub) for r in range(rep)]
    q_cols = lambda u, r: q_ref[r * HEAD_DIM:(r + 1) * HEAD_DIM, u * sub:(u + 1) * sub]
    keys = kc_ref[0, 0:nc, :]
    vals = vct_ref[0, :, 0:nc]

    s_buf[0:nc] = _dot(keys, q_cols(*stages[0]))
    inv_prev = None
    for n, (u, r) in enumerate(stages):
        cols = slice(u * sub, (u + 1) * sub)
        if n + 1 < len(stages):
            s_next = _dot(keys, q_cols(*stages[n + 1]))
        if n > 0:
            pu, pr = stages[n - 1]
            oc_ref[pr * HEAD_DIM:(pr + 1) * HEAD_DIM, pu * sub:(pu + 1) * sub] = (
                _dot(vals, p_buf[0:nc]) * (inv_prev * gate_ref[0, 0, pr:pr + 1, pu * sub:(pu + 1) * sub]))
        last = ((q0 + u * sub + lax.broadcasted_iota(jnp.int32, (1, sub), 1)) - (CMP_LEN - 1)) >> CMP_SHIFT
        if r == 0:
            bias_buf[0:nc] = jnp.where(lax.broadcasted_iota(jnp.int32, (nc, sub), 0) <= last, 0.0, NEG_INF)
        s = s_buf[0:nc] + bias_buf[0:nc]
        m = jnp.max(s, axis=0, keepdims=True)
        e = jnp.exp2(s - m)
        inv_prev = jnp.where(last >= 0, 1.0 / jnp.sum(e, axis=0, keepdims=True), 0.0)
        p_buf[0:nc] = e.astype(BF16)
        if r == 0:
            psum_buf[0:nc, cols] = e * inv_prev
        else:
            psum_buf[0:nc, cols] += e * inv_prev
        if n + 1 < len(stages):
            s_buf[0:nc] = s_next
    pu, pr = stages[-1]
    oc_ref[pr * HEAD_DIM:(pr + 1) * HEAD_DIM, pu * sub:(pu + 1) * sub] = (
        _dot(vals, p_buf[0:nc]) * (inv_prev * gate_ref[0, 0, pr:pr + 1, pu * sub:(pu + 1) * sub]))

    psum = psum_buf[0:nc, :]
    p_hi = psum.astype(BF16)
    p_lo = (psum - p_hi.astype(F32)).astype(BF16)
    pool = ov_ref[0:n_sel, 0:nc]
    imp = _dot(pool, p_hi) + _dot(pool, p_lo)

    blk = lax.broadcasted_iota(jnp.int32, (n_sel, tq), 0)
    tb = (q0 + lax.broadcasted_iota(jnp.int32, (n_sel, tq), 1)) >> SEL_SHIFT
    forced = (blk == 0) | ((tb - blk >= 0) & (tb - blk < SEL_LOCAL))
    n_forced = 1 + SEL_LOCAL
    premark = forced & (jnp.full((n_sel, tq), qi, jnp.int32) > 0)
    x_buf[0:n_sel] = jnp.where(premark, -jnp.inf, jnp.where(forced, FORCE, jnp.where(blk > tb, -FORCE, imp)))

    blk_f = blk.astype(F32)

    def pick(_, carry):
        x = x_buf[0:n_sel]
        top = jnp.max(x, axis=0, keepdims=True)
        first = jnp.min(jnp.where(x == top, blk_f, float(n_sel)), axis=0, keepdims=True)
        x_buf[0:n_sel] = jnp.where(blk_f == first, -jnp.inf, x)
        return carry

    lax.fori_loop(0, jnp.where(qi > 0, n_top - n_forced, n_top), pick, 0)
    mb_ref[0, 0:n_sel, :] = jnp.where(x_buf[0:n_sel] == -jnp.inf, 0.0, NEG_INF).astype(BF16)
    if n_sel < mb_ref.shape[1]:
        mb_ref[0, n_sel:, :] = jnp.full((mb_ref.shape[1] - n_sel, tq), NEG_INF, BF16)


def _cmp_topk_kernel(*refs, n_top, n_tiles):
    nc = refs[1].shape[1]
    n_sel = refs[3].shape[0]
    qi = pl.program_id(2)
    for v in range(max(n_tiles // 2, 1)):
        frac = lambda total: min(total, (2 * v + 2) * total // n_tiles)
        pl.when((qi >> 1) == v)(functools.partial(_cmp_topk_body, *refs, n_top=n_top, nc=frac(nc), n_sel=frac(n_sel)))


def _cmp_topk(qn_t, kc, vct, ov, gates, b, g_kv, s_len, n_top):
    tq = min(CMP_TILE, s_len)
    sub = min(CMP_SUBTILE, tq)
    nq = s_len // tq
    nc = kc.shape[1]
    n_sel = ov.shape[0]
    rep = NSA_REP
    return pl.pallas_call(
        functools.partial(_cmp_topk_kernel, n_top=n_top, n_tiles=nq),
        grid=(b, g_kv, nq),
        in_specs=[
            pl.BlockSpec((rep * HEAD_DIM, tq), lambda bi, gi, qi: (gi, bi * nq + qi)),
            pl.BlockSpec((1, nc, HEAD_DIM), lambda bi, gi, qi: (bi * g_kv + gi, 0, 0)),
            pl.BlockSpec((1, HEAD_DIM, nc), lambda bi, gi, qi: (bi * g_kv + gi, 0, 0)),
            _resident(ov.shape),
            pl.BlockSpec((1, 1, rep, tq), lambda bi, gi, qi: (0, gi, 0, bi * nq + qi)),
        ],
        out_specs=[
            pl.BlockSpec((1, n_sel, tq), lambda bi, gi, qi: (bi * g_kv + gi, 0, qi)),
            pl.BlockSpec((rep * HEAD_DIM, tq), lambda bi, gi, qi: (gi, bi * nq + qi)),
        ],
        out_shape=[jax.ShapeDtypeStruct((b * g_kv, n_sel, s_len), BF16),
                   jax.ShapeDtypeStruct((NSA_Q_W, b * s_len), F32)],
        scratch_shapes=[pltpu.VMEM((nc, sub), F32), pltpu.VMEM((nc, sub), BF16), pltpu.VMEM((nc, sub), F32),
                        pltpu.VMEM((nc, tq), F32), pltpu.VMEM((n_sel, tq), F32)],
        compiler_params=_params(3),
        name="cmp_topk",
    )(qn_t, kc, vct, ov, gates)


def _softmax_tile(m, s):
    m_new = jnp.maximum(m, jnp.max(s, axis=0, keepdims=True))
    return m_new, jnp.exp2(m - m_new), jnp.exp2(s - m_new).astype(BF16)


def _normalize(acc):
    return acc[:HEAD_DIM] / acc[HEAD_DIM:HEAD_DIM + 1]


def _rel_pos(tk, tq):
    return lax.broadcasted_iota(jnp.int32, (tk, tq), 1) - lax.broadcasted_iota(jnp.int32, (tk, tq), 0)


def _window_bias(rel, back, tk, kj, window):
    dist = rel + (back * tk + jnp.where(kj >= 0, 0, window))
    return jnp.where((dist >= 0) & (dist < window), 0.0, NEG_INF)


def _pipe_stage(s_buf, p_buf, heads, cur=None, nxt=None, prv=None):
    if nxt is not None:
        s_next = [_dot(nxt[0], nxt[1][r][...]) for r in heads]
    if prv is not None:
        pv = [_dot(prv[0], p_buf[r]) for r in heads]
    out = None
    if cur is not None:
        m, bias = cur
        out = []
        for r in heads:
            s = s_buf[r] if bias is None else s_buf[r] + bias
            m_r, a_r, p_buf[r] = _softmax_tile(m[r], s)
            out.append((m_r, a_r))
        out = tuple(zip(*out))
    for r in heads:
        if prv is not None:
            prv[1][r] = prv[2][r] * prv[1][r] + pv[r]
        if nxt is not None:
            s_buf[r] = s_next[r]
    return out


def _nsa_kernel(q_ref, mb_ref, oc_ref, gate_ref, ks_ref, ex_ref, vs_ref, kw_ref, vw_ref, o_ref, *scratch):
    n_sub = q_ref.shape[1] // NSA_TILE
    per_tile = len(scratch) // n_sub
    for sub in range(n_sub):
        at = pl.ds(sub * NSA_TILE, NSA_TILE)
        _nsa_tile(pl.program_id(2) * n_sub + sub, q_ref.at[:, at], mb_ref.at[:, :, at], oc_ref.at[:, at],
                  gate_ref.at[:, :, :, at], ks_ref, ex_ref, vs_ref, kw_ref, vw_ref, o_ref.at[at, :],
                  *scratch[sub * per_tile:(sub + 1) * per_tile])


def _nsa_tile(qi, q_ref, mb_ref, oc_ref, gate_ref, ks_ref, ex_ref, vs_ref, kw_ref, vw_ref, o_ref,
              qa_buf, s_buf, p_buf, sel_acc, win_acc):
    tq = q_ref.shape[1]
    tk = tq
    n_sel = mb_ref.shape[1]
    rep = NSA_REP
    heads = range(rep)
    stage = functools.partial(_pipe_stage, s_buf, p_buf, heads)
    key_rows = lambda j: pl.ds(pl.multiple_of(j * tk, tk), tk)

    def sel_keys(j):
        return jnp.concatenate([ex_ref[key_rows(j), :], ks_ref[0, key_rows(j), :]], axis=1)

    q_heads = [q_ref.at[pl.ds(r * HEAD_DIM, HEAD_DIM), :] for r in heads]
    q_aug = [qa_buf.at[r] for r in heads]
    for r in heads:
        qa_buf[r, 0:n_sel, :] = mb_ref[0]
        qa_buf[r, n_sel:n_sel + HEAD_DIM, :] = q_heads[r][...]
        s_buf[r] = _dot(sel_keys(0), qa_buf[r])
        p_buf[r] = jnp.zeros((tk, tq), BF16)
        sel_acc[r] = jnp.zeros((VAL_ROWS, tq), F32)
        win_acc[r] = jnp.zeros((VAL_ROWS, tq), F32)
    row = lambda v: tuple(jnp.full((1, tq), v, F32) for _ in heads)

    def sel_stage(i, carry):
        m, alpha = carry
        return stage(cur=(m, None), nxt=(sel_keys(i + 1), q_aug),
                     prv=(vs_ref[jnp.maximum(i - 1, 0)], sel_acc, alpha))

    carry = lax.fori_loop(0, qi >> 1, lambda j, c: sel_stage(2 * j + 1, sel_stage(2 * j, c)),
                          (row(NEG_INF), row(1.0)))
    m, alpha = lax.fori_loop(0, qi & 1, lambda _, c: sel_stage(qi - 1, c), carry)

    rel = _rel_pos(tk, tq)
    backs = list(range(NSA_WINDOW // tk, -1, -1))
    win_tile = [jnp.maximum(qi - back, 0) for back in backs]
    win_bias = [_window_bias(rel, back, tk, qi - back, NSA_WINDOW) for back in backs]
    _, alpha_d = stage(cur=(m, jnp.where(rel >= 0, 0.0, NEG_INF)),
                       nxt=(kw_ref[0, key_rows(win_tile[0]), :], q_heads),
                       prv=(vs_ref[jnp.maximum(qi - 1, 0)], sel_acc, alpha))
    prv = (vs_ref[qi], sel_acc, alpha_d)
    m_w = row(NEG_INF)
    for n in range(len(backs)):
        nxt = (kw_ref[0, key_rows(win_tile[n + 1]), :], q_heads) if n + 1 < len(backs) else None
        m_w, alpha_w = stage(cur=(m_w, win_bias[n]), nxt=nxt, prv=prv)
        prv = (vw_ref[win_tile[n]], win_acc, alpha_w)
    stage(prv=prv)

    outs = [oc_ref[r * HEAD_DIM:(r + 1) * HEAD_DIM, :]
            + gate_ref[1, 0, r:r + 1, :] * _normalize(sel_acc[r])
            + gate_ref[2, 0, r:r + 1, :] * _normalize(win_acc[r]) for r in heads]
    o_ref[...] = jnp.concatenate(outs, axis=0).T.astype(o_ref.dtype)


def _nsa(qn_t, mb, oc_t, gates, ksel, expand, vsel, kwin, vwin, b, g_kv, s_len):
    tk = NSA_TILE
    tq = min(NSA_STEP, s_len)
    nq = s_len // tq
    rep = NSA_REP
    n_sel = mb.shape[1]
    keys = pl.BlockSpec((1, s_len, HEAD_DIM), lambda bi, gi, qi: (gi * b + bi, 0, 0))
    vals = pl.BlockSpec((s_len // tk, VAL_ROWS, tk), lambda bi, gi, qi: (bi, gi, 0))
    return pl.pallas_call(
        _nsa_kernel,
        grid=(b, g_kv, nq),
        in_specs=[
            pl.BlockSpec((rep * HEAD_DIM, tq), lambda bi, gi, qi: (gi, bi * nq + qi)),
            pl.BlockSpec((1, n_sel, tq), lambda bi, gi, qi: (bi * g_kv + gi, 0, qi)),
            pl.BlockSpec((rep * HEAD_DIM, tq), lambda bi, gi, qi: (gi, bi * nq + qi)),
            pl.BlockSpec((3, 1, rep, tq), lambda bi, gi, qi: (0, gi, 0, bi * nq + qi)),
            keys, _resident(expand.shape), vals, keys, vals,
        ],
        out_specs=pl.BlockSpec((tq, rep * HEAD_DIM), lambda bi, gi, qi: (bi * nq + qi, gi)),
        out_shape=jax.ShapeDtypeStruct((b * s_len, NSA_Q_W), BF16),
        scratch_shapes=[pltpu.VMEM((rep, n_sel + HEAD_DIM, tk), BF16), pltpu.VMEM((rep, tk, tk), F32),
                        pltpu.VMEM((rep, tk, tk), BF16), pltpu.VMEM((rep, VAL_ROWS, tk), F32),
                        pltpu.VMEM((rep, VAL_ROWS, tk), F32)] * (tq // tk),
        compiler_params=_params(3),
        name="nsa",
    )(qn_t, mb, oc_t, gates, ksel.reshape(g_kv * b, s_len, HEAD_DIM), expand, vsel,
      kwin.reshape(g_kv * b, s_len, HEAD_DIM), vwin)


def _swa_kernel(q_ref, sink_ref, k_ref, v_ref, o_ref, s_buf, p_buf, bias_buf, o_buf):
    tq = q_ref.shape[1]
    tv = v_ref.shape[2]
    nk = s_buf.shape[0]
    heads = SWA_HEADS
    qi = pl.program_id(1)
    q0 = qi * tq
    k0 = jnp.maximum(q0 - SWA_WINDOW, 0)

    keys = k_ref[0, pl.ds(pl.multiple_of(k0, tv), nk), :]
    vals = jnp.concatenate([v_ref[k0 // tv + j] for j in range(nk // tv)], axis=1)
    dist = _rel_pos(nk, tq) + (q0 - k0)
    bias_buf[...] = jnp.where((dist >= 0) & (dist < SWA_WINDOW), 0.0, NEG_INF)
    q_head = lambda h: q_ref[h * HEAD_DIM:(h + 1) * HEAD_DIM, :]

    def finish(h, m_all, sink):
        acc = _dot(vals, p_buf[...])
        o_buf[h * HEAD_DIM:(h + 1) * HEAD_DIM, :] = (
            acc[:HEAD_DIM] / (acc[HEAD_DIM:HEAD_DIM + 1] + jnp.exp2(sink - m_all)))

    s_buf[...] = _dot(keys, q_head(0))
    prev = None
    for h in range(heads):
        if h + 1 < heads:
            s_next = _dot(keys, q_head(h + 1))
        if prev is not None:
            finish(*prev)
        sink = sink_ref[:, h * tq:(h + 1) * tq]
        s = s_buf[...] + bias_buf[...]
        m_all = jnp.maximum(jnp.max(s, axis=0, keepdims=True), sink)
        p_buf[...] = jnp.exp2(s - m_all).astype(BF16)
        prev = (h, m_all, sink)
        if h + 1 < heads:
            s_buf[...] = s_next
    finish(*prev)
    o_ref[...] = o_buf[...].T.astype(o_ref.dtype)


def _swa(sq_t, sink_row, k, v, b, s_len):
    tq = min(SWA_TILE, s_len)
    nq = s_len // tq
    tv = v.shape[2]
    nk = tq + SWA_WINDOW
    return pl.pallas_call(
        _swa_kernel,
        grid=(b, nq),
        in_specs=[pl.BlockSpec((SWA_Q_W, tq), lambda bi, qi: (0, bi * nq + qi)),
                  _resident(sink_row.shape),
                  pl.BlockSpec((1, s_len, HEAD_DIM), lambda bi, qi: (bi, 0, 0)),
                  pl.BlockSpec((s_len // tv, VAL_ROWS, tv), lambda bi, qi: (bi, 0, 0))],
        out_specs=pl.BlockSpec((tq, SWA_Q_W), lambda bi, qi: (bi * nq + qi, 0)),
        out_shape=jax.ShapeDtypeStruct((b * s_len, SWA_Q_W), BF16),
        scratch_shapes=[pltpu.VMEM((nk, tq), F32), pltpu.VMEM((nk, tq), BF16), pltpu.VMEM((nk, tq), F32),
                        pltpu.VMEM((SWA_Q_W, tq), F32)],
        compiler_params=_params(2),
        name="swa",
    )(sq_t, sink_row, k.reshape(b, s_len, HEAD_DIM), v)


def _overlap_matrix(n_sel, n_cmp, n_cmp_pad):
    cs = np.arange(n_cmp) * CMP_STRIDE
    ss = np.arange(n_sel) * SEL_LEN
    ov = np.clip(np.minimum(cs[None, :] + CMP_LEN, ss[:, None] + SEL_LEN)
                 - np.maximum(cs[None, :], ss[:, None]), 0, None).astype(np.float32) / CMP_LEN
    return np.pad(ov, ((0, 0), (0, n_cmp_pad - n_cmp)))


def _layer(h, positions, w, b, s_len, norm_final, last):
    n, d = h.shape
    g_kv = NSA_KV
    n_sel = s_len // SEL_LEN
    n_top = min(SEL_TOPN, n_sel)
    nc = s_len // CMP_STRIDE
    bf = lambda a: a.astype(BF16)

    h = _ffn(h, w['norm_ffn1'][None], bf(w['ffn1_gate']), bf(w['ffn1_up']), bf(w['ffn1_down']),
             w['norm_ffn1'][None], False)

    pts = np.cumsum((NSA_Q_W,) + (NSA_KV_W,) * 6 + (NSA_GATE_W, SWA_Q_W, SWA_KV_W, SWA_KV_W, d, d))[:-1]
    (w_nq, w_kc, w_vc, w_ksl, w_vsl, w_kwn, w_vwn, w_ng, w_sq, w_sk, w_sv, w_ga, w_gb) = jnp.split(
        bf(w['w_in']), pts, axis=1)
    parts = dict(qn=w_nq, sq=w_sq, ksel=w_ksl, kwin=w_kwn, kswa=w_sk, vsel=w_vsl, vwin=w_vwn, vswa=w_sv,
                 ng=w_ng, kc=w_kc, vc=w_vc)
    w_t = jnp.concatenate([jnp.pad(parts[name], ((0, 0), (0, width - parts[name].shape[1])))
                           for name, width in PROJ_ROWS], axis=1).T
    w_n = jnp.concatenate([w_ga, w_gb], axis=1)

    freq_col = (ROPE_THETA ** (-jnp.arange(HALF_DIM, dtype=F32) / HALF_DIM))[:, None]
    (qn_t, sq_t, ksel, kwin, kswa, vsel, vwin, vswa, ng_t, kcv, gab) = _proj(
        h, w['norm_mix'][None], w_t, w_n, positions.reshape(1, n), freq_col)

    pos_c = jnp.pad(positions[:, CMP_LEN - 1::CMP_STRIDE], ((0, 0), (0, 1)))[:, None, :]
    w1 = lambda a: bf(a.reshape(CMP_LEN, HEAD_DIM, a.shape[-1]))
    kc, vc_t = _compress(kcv, w['cmp_pe_k'], w['cmp_pe_v'], w1(w['cmp_k_w1']), w1(w['cmp_v_w1']),
                         bf(w['cmp_k_w2'].T), bf(w['cmp_v_w2'].T), pos_c, freq_col, b, g_kv, s_len)

    gates = ng_t.reshape(3, g_kv, NSA_REP, n)
    ov = jnp.asarray(_overlap_matrix(n_sel, nc - 1, nc), BF16)
    mb, oc_t = _cmp_topk(qn_t, kc, vc_t, ov, gates, b, g_kv, s_len, n_top)

    expand = jnp.asarray(np.arange(s_len)[:, None] // SEL_LEN == np.arange(n_sel)[None, :], BF16)
    o_a = _nsa(qn_t, mb, oc_t, gates, ksel, expand, vsel, kwin, vwin, b, g_kv, s_len)

    sink_row = jnp.repeat(w['swa_sinks'].astype(F32) * LOG2E, min(SWA_TILE, s_len))[None]
    o_b = _swa(sq_t, sink_row, kswa, vswa, b, s_len)

    branches = (o_a, o_b, gab, bf(w['w_branch_a']), bf(w['w_branch_b']), bf(w['w_out']))
    return _ffn(h, w['norm_ffn2'][None], bf(w['ffn2_gate']), bf(w['ffn2_up']), bf(w['ffn2_down']),
                norm_final[None], last, branches)


def kernel(x, positions, norm_ffn1, ffn1_gate, ffn1_up, ffn1_down, norm_mix, w_in, cmp_pe_k, cmp_k_w1, cmp_k_w2, cmp_pe_v, cmp_v_w1, cmp_v_w2, swa_sinks, w_branch_a, w_branch_b, w_out, norm_ffn2, ffn2_gate, ffn2_up, ffn2_down, norm_final):
    b, s_len, d = x.shape
    stacked = dict(norm_ffn1=norm_ffn1, ffn1_gate=ffn1_gate, ffn1_up=ffn1_up, ffn1_down=ffn1_down,
                   norm_mix=norm_mix, w_in=w_in, cmp_pe_k=cmp_pe_k, cmp_k_w1=cmp_k_w1, cmp_k_w2=cmp_k_w2,
                   cmp_pe_v=cmp_pe_v, cmp_v_w1=cmp_v_w1, cmp_v_w2=cmp_v_w2, swa_sinks=swa_sinks,
                   w_branch_a=w_branch_a, w_branch_b=w_branch_b, w_out=w_out,
                   norm_ffn2=norm_ffn2, ffn2_gate=ffn2_gate, ffn2_up=ffn2_up, ffn2_down=ffn2_down)
    depth = norm_ffn1.shape[0]
    h = x.reshape(b * s_len, d)
    for i in range(depth):
        w = {k: v[i] for k, v in stacked.items()}
        h = _layer(h, positions, w, b, s_len, norm_final, i == depth - 1)
    return h.reshape(b, s_len, d)
```

```python
import functools

import numpy as np
import jax
import jax.numpy as jnp
from jax import lax
from jax.experimental import pallas as pl
from jax.experimental.pallas import tpu as pltpu

HEAD_DIM = 64
HALF_DIM = HEAD_DIM // 2
NSA_HEADS = 8
NSA_KV = 2
NSA_REP = NSA_HEADS // NSA_KV
SWA_HEADS = 8
CMP_STRIDE = 16
CMP_SHIFT = 4
CMP_LEN = 2 * CMP_STRIDE
SEL_LEN = 64
SEL_SHIFT = 6
SEL_TOPN = 16
SEL_LOCAL = 2
NSA_WINDOW = 512
SWA_WINDOW = 128
ROPE_THETA = 10000.0
RMS_EPS = 1e-6
FFN_HALF = 0.5
NEG_INF = -1e30
FORCE = 1e9
LOG2E = 1.4426950408889634
Q_SCALE = HEAD_DIM ** -0.5 * LOG2E
BF16_ROWS = 16
VAL_ROWS = HEAD_DIM + BF16_ROWS
LANES = 128

NSA_Q_W = NSA_HEADS * HEAD_DIM
NSA_KV_W = NSA_KV * HEAD_DIM
NSA_GATE_W = 3 * NSA_HEADS
SWA_Q_W = SWA_HEADS * HEAD_DIM
SWA_KV_W = HEAD_DIM

VMEM_LIMIT_BYTES = 56 * 1024 * 1024

BF16 = jnp.bfloat16
F32 = jnp.float32

ROW_TILE = 512
CMP_TILE = 1024
CMP_SUBTILE = 256
NSA_TILE = 256
NSA_STEP = 1024
SWA_TILE = 256
SWA_VTILE = 128


def _params(n_axes):
    return pltpu.CompilerParams(dimension_semantics=("arbitrary",) * n_axes,
                                vmem_limit_bytes=VMEM_LIMIT_BYTES)


def _resident(shape):
    zeros = (0,) * len(shape)
    return pl.BlockSpec(shape, lambda *_: zeros, pipeline_mode=pl.Buffered(1))


def _rms(x, g):
    y = x * lax.rsqrt(jnp.mean(x * x, axis=-1, keepdims=True) + RMS_EPS)
    return y * g


def _dot(a, b):
    return jnp.dot(a, b, preferred_element_type=F32)


def _dot_nt(a, b):
    return lax.dot_general(a, b, (((1,), (1,)), ((), ())), preferred_element_type=F32)


def _rope_angles(pos_row, freq_col):
    ang = pos_row.astype(F32) * freq_col
    return jnp.cos(ang), jnp.sin(ang)


def _rope_rows(block, cos_t, sin_t):
    out = []
    for hd in range(block.shape[0] // HEAD_DIM):
        x1 = block[hd * HEAD_DIM:hd * HEAD_DIM + HALF_DIM]
        x2 = block[hd * HEAD_DIM + HALF_DIM:(hd + 1) * HEAD_DIM]
        out += [x1 * cos_t - x2 * sin_t, x2 * cos_t + x1 * sin_t]
    return out


def _ones_rows(tk):
    return jnp.where(lax.broadcasted_iota(jnp.int32, (BF16_ROWS, tk), 0) == 0, 1.0, 0.0).astype(BF16)


def _merge_branches(h_ref, oa_ref, ob_ref, gab_ref, wa_ref, wb_ref, wo_ref):
    d = h_ref.shape[1]
    gab = gab_ref[...]
    merged = gab[:, :d] * _dot(oa_ref[...], wa_ref[...]) + gab[:, d:] * _dot(ob_ref[...], wb_ref[...])
    return h_ref[...] + _dot(merged.astype(BF16), wo_ref[...])


def _ffn_kernel(*refs, final_norm, merge):
    x_ref, g_ref, wg_ref, wu_ref, wd_ref, gf_ref, o_ref = refs[-7:]
    x = _merge_branches(x_ref, *refs[:-7]) if merge else x_ref[...]
    xb = _rms(x, g_ref[...]).astype(BF16)
    a = _dot(xb, wg_ref[...])
    b = _dot(xb, wu_ref[...])
    t = (a * jax.nn.sigmoid(a)) * b
    h = x + FFN_HALF * _dot(t.astype(BF16), wd_ref[...])
    if final_norm:
        h = _rms(h, gf_ref[...])
    o_ref[...] = h


def _ffn(x2, g, wg, wu, wd, gf, final_norm, branches=None):
    n, d = x2.shape
    f = wg.shape[1]
    tm = ROW_TILE
    rows = lambda w: pl.BlockSpec((tm, w), lambda i: (i, 0))
    merge_specs, merge_args = [], ()
    if branches is not None:
        oa, ob, gab, wa, wb, wo = branches
        merge_specs = [rows(oa.shape[1]), rows(ob.shape[1]), rows(2 * d),
                       _resident(wa.shape), _resident(wb.shape), _resident(wo.shape)]
        merge_args = branches
    return pl.pallas_call(
        functools.partial(_ffn_kernel, final_norm=final_norm, merge=branches is not None),
        grid=(n // tm,),
        in_specs=merge_specs + [rows(d), _resident((1, d)), _resident((d, f)), _resident((d, f)),
                                _resident((f, d)), _resident((1, d))],
        out_specs=rows(d),
        out_shape=jax.ShapeDtypeStruct((n, d), F32),
        compiler_params=_params(1),
        name="ffn_final" if final_norm else "ffn",
    )(*merge_args, x2, g, wg, wu, wd, gf)


PROJ_ROWS = (('qn', NSA_Q_W), ('sq', SWA_Q_W), ('ksel', NSA_KV_W), ('kwin', NSA_KV_W), ('kswa', LANES),
             ('vsel', NSA_KV_W), ('vwin', NSA_KV_W), ('vswa', SWA_KV_W), ('ng', 32),
             ('kc', NSA_KV_W), ('vc', NSA_KV_W))


def _proj_kernel(h_ref, g_ref, wt_ref, wn_ref, pos_ref, freq_ref,
                 qn_ref, sq_ref, ksel_ref, kwin_ref, kswa_ref, vsel_ref, vwin_ref, vswa_ref,
                 ng_ref, kcv_ref, gab_ref):
    ub = _rms(h_ref[...], g_ref[...]).astype(BF16)
    yt = _dot_nt(wt_ref[...], ub)
    rows, o = {}, 0
    for name, width in PROJ_ROWS:
        rows[name] = yt[o:o + width]
        o += width
    cos_t, sin_t = _rope_angles(pos_ref[...], freq_ref[...])

    for name, ref in (('qn', qn_ref), ('sq', sq_ref)):
        for i, piece in enumerate(_rope_rows(rows[name], cos_t, sin_t)):
            ref[i * HALF_DIM:(i + 1) * HALF_DIM, :] = (piece * Q_SCALE).astype(BF16)

    for name, ref in (('ksel', ksel_ref), ('kwin', kwin_ref)):
        k_nat = jnp.concatenate(_rope_rows(rows[name], cos_t, sin_t), axis=0).T
        for g in range(NSA_KV):
            ref[g] = k_nat[:, g * HEAD_DIM:(g + 1) * HEAD_DIM].astype(BF16)
    kswa = _rope_rows(rows['kswa'][:HEAD_DIM], cos_t, sin_t) + [rows['kswa'][HEAD_DIM:]]
    kswa_ref[...] = jnp.concatenate(kswa, axis=0).T[:, :HEAD_DIM].astype(BF16)

    for name, ref, groups in (('vsel', vsel_ref, NSA_KV), ('vwin', vwin_ref, NSA_KV), ('vswa', vswa_ref, 1)):
        tk = ref.shape[2]
        for j in range(ref.shape[0]):
            for g in range(groups):
                ref[j, g * VAL_ROWS:g * VAL_ROWS + HEAD_DIM, :] = (
                    rows[name][g * HEAD_DIM:(g + 1) * HEAD_DIM, j * tk:(j + 1) * tk].astype(BF16))
                ref[j, g * VAL_ROWS + HEAD_DIM:(g + 1) * VAL_ROWS, :] = _ones_rows(tk)

    ng_ref[...] = jax.nn.sigmoid(rows['ng'][:NSA_GATE_W])

    for i, name in enumerate(('kc', 'vc')):
        nat = rows[name].T
        for g in range(NSA_KV):
            kcv_ref[i * NSA_KV + g] = nat[:, g * HEAD_DIM:(g + 1) * HEAD_DIM]

    gab_ref[...] = jax.nn.sigmoid(_dot(ub, wn_ref[...]))


def _proj(h2, g, wt, wn, pos_row, freq_col):
    n, d = h2.shape
    tm = ROW_TILE
    rows = lambda w: pl.BlockSpec((tm, w), lambda i: (i, 0))
    cols = lambda w: pl.BlockSpec((w, tm), lambda i: (0, i))
    grouped = lambda k: pl.BlockSpec((k, tm, HEAD_DIM), lambda i: (0, i, 0))
    tiles = lambda groups, tk: pl.BlockSpec((tm // tk, groups * VAL_ROWS, tk), lambda i: (i, 0, 0))
    val_shape = lambda groups, tk: jax.ShapeDtypeStruct((n // tk, groups * VAL_ROWS, tk), BF16)
    out_shape = [
        jax.ShapeDtypeStruct((NSA_Q_W, n), BF16), jax.ShapeDtypeStruct((SWA_Q_W, n), BF16),
        jax.ShapeDtypeStruct((NSA_KV, n, HEAD_DIM), BF16), jax.ShapeDtypeStruct((NSA_KV, n, HEAD_DIM), BF16),
        jax.ShapeDtypeStruct((n, HEAD_DIM), BF16),
        val_shape(NSA_KV, NSA_TILE), val_shape(NSA_KV, NSA_TILE), val_shape(1, SWA_VTILE),
        jax.ShapeDtypeStruct((NSA_GATE_W, n), F32), jax.ShapeDtypeStruct((2 * NSA_KV, n, HEAD_DIM), F32),
        jax.ShapeDtypeStruct((n, 2 * d), F32),
    ]
    return pl.pallas_call(
        _proj_kernel,
        grid=(n // tm,),
        in_specs=[rows(d), _resident((1, d)), _resident(wt.shape), _resident(wn.shape),
                  cols(1), _resident(freq_col.shape)],
        out_specs=[cols(NSA_Q_W), cols(SWA_Q_W), grouped(NSA_KV), grouped(NSA_KV), rows(HEAD_DIM),
                   tiles(NSA_KV, NSA_TILE), tiles(NSA_KV, NSA_TILE), tiles(1, SWA_VTILE),
                   cols(NSA_GATE_W), grouped(2 * NSA_KV), rows(2 * d)],
        out_shape=out_shape,
        compiler_params=_params(1),
        name="proj",
    )(h2, g, wt, wn, pos_row, freq_col)


def _gelu(x):
    return jax.nn.gelu(x, approximate=True)


def _compress_kernel(kc_ref, vc_ref, pek_ref, pev_ref, w1k_ref, w1v_ref, w2kt_ref, w2vt_ref,
                     pos_ref, freq_ref, kc_out, vct_out):
    nc = kc_out.shape[1]

    def hidden(x_ref, pe_ref, w1_ref):
        top = bot = None
        for j in range(CMP_STRIDE):
            x = x_ref[0, pl.ds(j, nc, stride=CMP_STRIDE), :]
            t = _dot((x + pe_ref[j:j + 1, :]).astype(BF16), w1_ref[j])
            b = _dot((x + pe_ref[CMP_STRIDE + j:CMP_STRIDE + j + 1, :]).astype(BF16), w1_ref[CMP_STRIDE + j])
            top, bot = (t, b) if top is None else (top + t, bot + b)
        return _gelu(top + pltpu.roll(bot, shift=nc - 1, axis=0)).astype(BF16)

    kt = _dot_nt(w2kt_ref[...], hidden(kc_ref, pek_ref, w1k_ref))
    cos_t, sin_t = _rope_angles(pos_ref[0], freq_ref[...])
    kt = jnp.concatenate(_rope_rows(kt, cos_t, sin_t) + [jnp.zeros((LANES - HEAD_DIM, nc), F32)], axis=0)
    kc_out[0] = kt.T[:, :HEAD_DIM].astype(BF16)
    vct_out[0] = _dot_nt(w2vt_ref[...], hidden(vc_ref, pev_ref, w1v_ref)).astype(BF16)


def _compress(kcv, pek, pev, w1k, w1v, w2kt, w2vt, pos_c, freq_col, b, g_kv, s_len):
    nc = s_len // CMP_STRIDE
    hid = w1k.shape[-1]
    kcv = kcv.reshape(2 * g_kv * b, s_len, HEAD_DIM)
    src = lambda kind: pl.BlockSpec((1, s_len, HEAD_DIM),
                                    lambda i: ((kind * g_kv + i % g_kv) * b + i // g_kv, 0, 0))
    per = lambda *s: pl.BlockSpec((1,) + s, lambda i: (i,) + (0,) * len(s))
    return pl.pallas_call(
        _compress_kernel,
        grid=(b * g_kv,),
        in_specs=[src(0), src(1), _resident(pek.shape), _resident(pev.shape),
                  _resident(w1k.shape), _resident(w1v.shape), _resident((HEAD_DIM, hid)),
                  _resident((HEAD_DIM, hid)),
                  pl.BlockSpec((1, 1, nc), lambda i: (i // g_kv, 0, 0)), _resident(freq_col.shape)],
        out_specs=[per(nc, HEAD_DIM), per(HEAD_DIM, nc)],
        out_shape=[jax.ShapeDtypeStruct((b * g_kv, nc, HEAD_DIM), BF16),
                   jax.ShapeDtypeStruct((b * g_kv, HEAD_DIM, nc), BF16)],
        compiler_params=_params(1),
        name="compress",
    )(kcv, kcv, pek, pev, w1k, w1v, w2kt, w2vt, pos_c, freq_col)


def _cmp_topk_body(q_ref, kc_ref, vct_ref, ov_ref, gate_ref, mb_ref, oc_ref,
                   s_buf, p_buf, bias_buf, psum_buf, x_buf, *, n_top, nc, n_sel):
    tq = q_ref.shape[1]
    sub = s_buf.shape[1]
    rep = NSA_REP
    qi = pl.program_id(2)
    q0 = qi * tq
    stages = [(u, r) for u in range(tq // sub) for r in range(rep)]
    q_cols = lambda u, r: q_ref[r * HEAD_DIM:(r + 1) * HEAD_DIM, u * sub:(u + 1) * sub]
    keys = kc_ref[0, 0:nc, :]
    vals = vct_ref[0, :, 0:nc]

    s_buf[0:nc] = _dot(keys, q_cols(*stages[0]))
    inv_prev = None
    for n, (u, r) in enumerate(stages):
        cols = slice(u * sub, (u + 1) * sub)
        if n + 1 < len(stages):
            s_next = _dot(keys, q_cols(*stages[n + 1]))
        if n > 0:
            pu, pr = stages[n - 1]
            oc_ref[pr * HEAD_DIM:(pr + 1) * HEAD_DIM, pu * sub:(pu + 1) * sub] = (
                _dot(vals, p_buf[0:nc]) * (inv_prev * gate_ref[0, 0, pr:pr + 1, pu * sub:(pu + 1) * sub]))
        last = ((q0 + u * sub + lax.broadcasted_iota(jnp.int32, (1, sub), 1)) - (CMP_LEN - 1)) >> CMP_SHIFT
        if r == 0:
            bias_buf[0:nc] = jnp.where(lax.broadcasted_iota(jnp.int32, (nc, sub), 0) <= last, 0.0, NEG_INF)
        s = s_buf[0:nc] + bias_buf[0:nc]
        m = jnp.max(s, axis=0, keepdims=True)
        e = jnp.exp2(s - m)
        inv_prev = jnp.where(last >= 0, 1.0 / jnp.sum(e, axis=0, keepdims=True), 0.0)
        p_buf[0:nc] = e.astype(BF16)
        if r == 0:
            psum_buf[0:nc, cols] = e * inv_prev
        else:
            psum_buf[0:nc, cols] += e * inv_prev
        if n + 1 < len(stages):
            s_buf[0:nc] = s_next
    pu, pr = stages[-1]
    oc_ref[pr * HEAD_DIM:(pr + 1) * HEAD_DIM, pu * sub:(pu + 1) * sub] = (
        _dot(vals, p_buf[0:nc]) * (inv_prev * gate_ref[0, 0, pr:pr + 1, pu * sub:(pu + 1) * sub]))

    psum = psum_buf[0:nc, :]
    p_hi = psum.astype(BF16)
    p_lo = (psum - p_hi.astype(F32)).astype(BF16)
    pool = ov_ref[0:n_sel, 0:nc]
    imp = _dot(pool, p_hi) + _dot(pool, p_lo)

    blk = lax.broadcasted_iota(jnp.int32, (n_sel, tq), 0)
    tb = (q0 + lax.broadcasted_iota(jnp.int32, (n_sel, tq), 1)) >> SEL_SHIFT
    forced = (blk == 0) | ((tb - blk >= 0) & (tb - blk < SEL_LOCAL))
    n_forced = 1 + SEL_LOCAL
    premark = forced & (jnp.full((n_sel, tq), qi, jnp.int32) > 0)
    x_buf[0:n_sel] = jnp.where(premark, -jnp.inf, jnp.where(forced, FORCE, jnp.where(blk > tb, -FORCE, imp)))

    blk_f = blk.astype(F32)

    def pick(_, carry):
        x = x_buf[0:n_sel]
        top = jnp.max(x, axis=0, keepdims=True)
        first = jnp.min(jnp.where(x == top, blk_f, float(n_sel)), axis=0, keepdims=True)
        x_buf[0:n_sel] = jnp.where(blk_f == first, -jnp.inf, x)
        return carry

    lax.fori_loop(0, jnp.where(qi > 0, n_top - n_forced, n_top), pick, 0)
    mb_ref[0, 0:n_sel, :] = jnp.where(x_buf[0:n_sel] == -jnp.inf, 0.0, NEG_INF).astype(BF16)
    if n_sel < mb_ref.shape[1]:
        mb_ref[0, n_sel:, :] = jnp.full((mb_ref.shape[1] - n_sel, tq), NEG_INF, BF16)


def _cmp_topk_kernel(*refs, n_top, n_tiles):
    nc = refs[1].shape[1]
    n_sel = refs[3].shape[0]
    qi = pl.program_id(2)
    for v in range(max(n_tiles // 2, 1)):
        frac = lambda total: min(total, (2 * v + 2) * total // n_tiles)
        pl.when((qi >> 1) == v)(functools.partial(_cmp_topk_body, *refs, n_top=n_top, nc=frac(nc), n_sel=frac(n_sel)))


def _cmp_topk(qn_t, kc, vct, ov, gates, b, g_kv, s_len, n_top):
    tq = min(CMP_TILE, s_len)
    sub = min(CMP_SUBTILE, tq)
    nq = s_len // tq
    nc = kc.shape[1]
    n_sel = ov.shape[0]
    rep = NSA_REP
    return pl.pallas_call(
        functools.partial(_cmp_topk_kernel, n_top=n_top, n_tiles=nq),
        grid=(b, g_kv, nq),
        in_specs=[
            pl.BlockSpec((rep * HEAD_DIM, tq), lambda bi, gi, qi: (gi, bi * nq + qi)),
            pl.BlockSpec((1, nc, HEAD_DIM), lambda bi, gi, qi: (bi * g_kv + gi, 0, 0)),
            pl.BlockSpec((1, HEAD_DIM, nc), lambda bi, gi, qi: (bi * g_kv + gi, 0, 0)),
            _resident(ov.shape),
            pl.BlockSpec((1, 1, rep, tq), lambda bi, gi, qi: (0, gi, 0, bi * nq + qi)),
        ],
        out_specs=[
            pl.BlockSpec((1, n_sel, tq), lambda bi, gi, qi: (bi * g_kv + gi, 0, qi)),
            pl.BlockSpec((rep * HEAD_DIM, tq), lambda bi, gi, qi: (gi, bi * nq + qi)),
        ],
        out_shape=[jax.ShapeDtypeStruct((b * g_kv, n_sel, s_len), BF16),
                   jax.ShapeDtypeStruct((NSA_Q_W, b * s_len), F32)],
        scratch_shapes=[pltpu.VMEM((nc, sub), F32), pltpu.VMEM((nc, sub), BF16), pltpu.VMEM((nc, sub), F32),
                        pltpu.VMEM((nc, tq), F32), pltpu.VMEM((n_sel, tq), F32)],
        compiler_params=_params(3),
        name="cmp_topk",
    )(qn_t, kc, vct, ov, gates)


def _softmax_tile(m, s):
    m_new = jnp.maximum(m, jnp.max(s, axis=0, keepdims=True))
    return m_new, jnp.exp2(m - m_new), jnp.exp2(s - m_new).astype(BF16)


def _normalize(acc):
    return acc[:HEAD_DIM] / acc[HEAD_DIM:HEAD_DIM + 1]


def _rel_pos(tk, tq):
    return lax.broadcasted_iota(jnp.int32, (tk, tq), 1) - lax.broadcasted_iota(jnp.int32, (tk, tq), 0)


def _window_bias(rel, back, tk, kj, window):
    dist = rel + (back * tk + jnp.where(kj >= 0, 0, window))
    return jnp.where((dist >= 0) & (dist < window), 0.0, NEG_INF)


def _pipe_stage(s_buf, p_buf, heads, cur=None, nxt=None, prv=None):
    if nxt is not None:
        s_next = [_dot(nxt[0], nxt[1][r][...]) for r in heads]
    if prv is not None:
        pv = [_dot(prv[0], p_buf[r]) for r in heads]
    out = None
    if cur is not None:
        m, bias = cur
        out = []
        for r in heads:
            s = s_buf[r] if bias is None else s_buf[r] + bias
            m_r, a_r, p_buf[r] = _softmax_tile(m[r], s)
            out.append((m_r, a_r))
        out = tuple(zip(*out))
    for r in heads:
        if prv is not None:
            prv[1][r] = prv[2][r] * prv[1][r] + pv[r]
        if nxt is not None:
            s_buf[r] = s_next[r]
    return out


def _nsa_kernel(q_ref, mb_ref, oc_ref, gate_ref, ks_ref, ex_ref, vs_ref, kw_ref, vw_ref, o_ref, *scratch):
    n_sub = q_ref.shape[1] // NSA_TILE
    per_tile = len(scratch) // n_sub
    for sub in range(n_sub):
        at = pl.ds(sub * NSA_TILE, NSA_TILE)
        _nsa_tile(pl.program_id(2) * n_sub + sub, q_ref.at[:, at], mb_ref.at[:, :, at], oc_ref.at[:, at],
                  gate_ref.at[:, :, :, at], ks_ref, ex_ref, vs_ref, kw_ref, vw_ref, o_ref.at[at, :],
                  *scratch[sub * per_tile:(sub + 1) * per_tile])


def _nsa_tile(qi, q_ref, mb_ref, oc_ref, gate_ref, ks_ref, ex_ref, vs_ref, kw_ref, vw_ref, o_ref,
              qa_buf, s_buf, p_buf, sel_acc, win_acc):
    tq = q_ref.shape[1]
    tk = tq
    n_sel = mb_ref.shape[1]
    rep = NSA_REP
    heads = range(rep)
    stage = functools.partial(_pipe_stage, s_buf, p_buf, heads)
    key_rows = lambda j: pl.ds(pl.multiple_of(j * tk, tk), tk)

    def sel_keys(j):
        return jnp.concatenate([ex_ref[key_rows(j), :], ks_ref[0, key_rows(j), :]], axis=1)

    q_heads = [q_ref.at[pl.ds(r * HEAD_DIM, HEAD_DIM), :] for r in heads]
    q_aug = [qa_buf.at[r] for r in heads]
    for r in heads:
        qa_buf[r, 0:n_sel, :] = mb_ref[0]
        qa_buf[r, n_sel:n_sel + HEAD_DIM, :] = q_heads[r][...]
        s_buf[r] = _dot(sel_keys(0), qa_buf[r])
        p_buf[r] = jnp.zeros((tk, tq), BF16)
        sel_acc[r] = jnp.zeros((VAL_ROWS, tq), F32)
        win_acc[r] = jnp.zeros((VAL_ROWS, tq), F32)
    row = lambda v: tuple(jnp.full((1, tq), v, F32) for _ in heads)

    def sel_stage(i, carry):
        m, alpha = carry
        return stage(cur=(m, None), nxt=(sel_keys(i + 1), q_aug),
                     prv=(vs_ref[jnp.maximum(i - 1, 0)], sel_acc, alpha))

    carry = lax.fori_loop(0, qi >> 1, lambda j, c: sel_stage(2 * j + 1, sel_stage(2 * j, c)),
                          (row(NEG_INF), row(1.0)))
    m, alpha = lax.fori_loop(0, qi & 1, lambda _, c: sel_stage(qi - 1, c), carry)

    rel = _rel_pos(tk, tq)
    backs = list(range(NSA_WINDOW // tk, -1, -1))
    win_tile = [jnp.maximum(qi - back, 0) for back in backs]
    win_bias = [_window_bias(rel, back, tk, qi - back, NSA_WINDOW) for back in backs]
    _, alpha_d = stage(cur=(m, jnp.where(rel >= 0, 0.0, NEG_INF)),
                       nxt=(kw_ref[0, key_rows(win_tile[0]), :], q_heads),
                       prv=(vs_ref[jnp.maximum(qi - 1, 0)], sel_acc, alpha))
    prv = (vs_ref[qi], sel_acc, alpha_d)
    m_w = row(NEG_INF)
    for n in range(len(backs)):
        nxt = (kw_ref[0, key_rows(win_tile[n + 1]), :], q_heads) if n + 1 < len(backs) else None
        m_w, alpha_w = stage(cur=(m_w, win_bias[n]), nxt=nxt, prv=prv)
        prv = (vw_ref[win_tile[n]], win_acc, alpha_w)
    stage(prv=prv)

    outs = [oc_ref[r * HEAD_DIM:(r + 1) * HEAD_DIM, :]
            + gate_ref[1, 0, r:r + 1, :] * _normalize(sel_acc[r])
            + gate_ref[2, 0, r:r + 1, :] * _normalize(win_acc[r]) for r in heads]
    o_ref[...] = jnp.concatenate(outs, axis=0).T.astype(o_ref.dtype)


def _nsa(qn_t, mb, oc_t, gates, ksel, expand, vsel, kwin, vwin, b, g_kv, s_len):
    tk = NSA_TILE
    tq = min(NSA_STEP, s_len)
    nq = s_len // tq
    rep = NSA_REP
    n_sel = mb.shape[1]
    keys = pl.BlockSpec((1, s_len, HEAD_DIM), lambda bi, gi, qi: (gi * b + bi, 0, 0))
    vals = pl.BlockSpec((s_len // tk, VAL_ROWS, tk), lambda bi, gi, qi: (bi, gi, 0))
    return pl.pallas_call(
        _nsa_kernel,
        grid=(b, g_kv, nq),
        in_specs=[
            pl.BlockSpec((rep * HEAD_DIM, tq), lambda bi, gi, qi: (gi, bi * nq + qi)),
            pl.BlockSpec((1, n_sel, tq), lambda bi, gi, qi: (bi * g_kv + gi, 0, qi)),
            pl.BlockSpec((rep * HEAD_DIM, tq), lambda bi, gi, qi: (gi, bi * nq + qi)),
            pl.BlockSpec((3, 1, rep, tq), lambda bi, gi, qi: (0, gi, 0, bi * nq + qi)),
            keys, _resident(expand.shape), vals, keys, vals,
        ],
        out_specs=pl.BlockSpec((tq, rep * HEAD_DIM), lambda bi, gi, qi: (bi * nq + qi, gi)),
        out_shape=jax.ShapeDtypeStruct((b * s_len, NSA_Q_W), BF16),
        scratch_shapes=[pltpu.VMEM((rep, n_sel + HEAD_DIM, tk), BF16), pltpu.VMEM((rep, tk, tk), F32),
                        pltpu.VMEM((rep, tk, tk), BF16), pltpu.VMEM((rep, VAL_ROWS, tk), F32),
                        pltpu.VMEM((rep, VAL_ROWS, tk), F32)] * (tq // tk),
        compiler_params=_params(3),
        name="nsa",
    )(qn_t, mb, oc_t, gates, ksel.reshape(g_kv * b, s_len, HEAD_DIM), expand, vsel,
      kwin.reshape(g_kv * b, s_len, HEAD_DIM), vwin)


def _swa_kernel(q_ref, sink_ref, k_ref, v_ref, o_ref, s_buf, p_buf, bias_buf, o_buf):
    tq = q_ref.shape[1]
    tv = v_ref.shape[2]
    nk = s_buf.shape[0]
    heads = SWA_HEADS
    qi = pl.program_id(1)
    q0 = qi * tq
    k0 = jnp.maximum(q0 - SWA_WINDOW, 0)

    keys = k_ref[0, pl.ds(pl.multiple_of(k0, tv), nk), :]
    vals = jnp.concatenate([v_ref[k0 // tv + j] for j in range(nk // tv)], axis=1)
    dist = _rel_pos(nk, tq) + (q0 - k0)
    bias_buf[...] = jnp.where((dist >= 0) & (dist < SWA_WINDOW), 0.0, NEG_INF)
    q_head = lambda h: q_ref[h * HEAD_DIM:(h + 1) * HEAD_DIM, :]

    def finish(h, m_all, sink):
        acc = _dot(vals, p_buf[...])
        o_buf[h * HEAD_DIM:(h + 1) * HEAD_DIM, :] = (
            acc[:HEAD_DIM] / (acc[HEAD_DIM:HEAD_DIM + 1] + jnp.exp2(sink - m_all)))

    s_buf[...] = _dot(keys, q_head(0))
    prev = None
    for h in range(heads):
        if h + 1 < heads:
            s_next = _dot(keys, q_head(h + 1))
        if prev is not None:
            finish(*prev)
        sink = sink_ref[:, h * tq:(h + 1) * tq]
        s = s_buf[...] + bias_buf[...]
        m_all = jnp.maximum(jnp.max(s, axis=0, keepdims=True), sink)
        p_buf[...] = jnp.exp2(s - m_all).astype(BF16)
        prev = (h, m_all, sink)
        if h + 1 < heads:
            s_buf[...] = s_next
    finish(*prev)
    o_ref[...] = o_buf[...].T.astype(o_ref.dtype)


def _swa(sq_t, sink_row, k, v, b, s_len):
    tq = min(SWA_TILE, s_len)
    nq = s_len // tq
    tv = v.shape[2]
    nk = tq + SWA_WINDOW
    return pl.pallas_call(
        _swa_kernel,
        grid=(b, nq),
        in_specs=[pl.BlockSpec((SWA_Q_W, tq), lambda bi, qi: (0, bi * nq + qi)),
                  _resident(sink_row.shape),
                  pl.BlockSpec((1, s_len, HEAD_DIM), lambda bi, qi: (bi, 0, 0)),
                  pl.BlockSpec((s_len // tv, VAL_ROWS, tv), lambda bi, qi: (bi, 0, 0))],
        out_specs=pl.BlockSpec((tq, SWA_Q_W), lambda bi, qi: (bi * nq + qi, 0)),
        out_shape=jax.ShapeDtypeStruct((b * s_len, SWA_Q_W), BF16),
        scratch_shapes=[pltpu.VMEM((nk, tq), F32), pltpu.VMEM((nk, tq), BF16), pltpu.VMEM((nk, tq), F32),
                        pltpu.VMEM((SWA_Q_W, tq), F32)],
        compiler_params=_params(2),
        name="swa",
    )(sq_t, sink_row, k.reshape(b, s_len, HEAD_DIM), v)


def _overlap_matrix(n_sel, n_cmp, n_cmp_pad):
    cs = np.arange(n_cmp) * CMP_STRIDE
    ss = np.arange(n_sel) * SEL_LEN
    ov = np.clip(np.minimum(cs[None, :] + CMP_LEN, ss[:, None] + SEL_LEN)
                 - np.maximum(cs[None, :], ss[:, None]), 0, None).astype(np.float32) / CMP_LEN
    return np.pad(ov, ((0, 0), (0, n_cmp_pad - n_cmp)))


def _layer(h, positions, w, b, s_len, norm_final, last):
    n, d = h.shape
    g_kv = NSA_KV
    n_sel = s_len // SEL_LEN
    n_top = min(SEL_TOPN, n_sel)
    nc = s_len // CMP_STRIDE
    bf = lambda a: a.astype(BF16)

    h = _ffn(h, w['norm_ffn1'][None], bf(w['ffn1_gate']), bf(w['ffn1_up']), bf(w['ffn1_down']),
             w['norm_ffn1'][None], False)

    pts = np.cumsum((NSA_Q_W,) + (NSA_KV_W,) * 6 + (NSA_GATE_W, SWA_Q_W, SWA_KV_W, SWA_KV_W, d, d))[:-1]
    (w_nq, w_kc, w_vc, w_ksl, w_vsl, w_kwn, w_vwn, w_ng, w_sq, w_sk, w_sv, w_ga, w_gb) = jnp.split(
        bf(w['w_in']), pts, axis=1)
    parts = dict(qn=w_nq, sq=w_sq, ksel=w_ksl, kwin=w_kwn, kswa=w_sk, vsel=w_vsl, vwin=w_vwn, vswa=w_sv,
                 ng=w_ng, kc=w_kc, vc=w_vc)
    w_t = jnp.concatenate([jnp.pad(parts[name], ((0, 0), (0, width - parts[name].shape[1])))
                           for name, width in PROJ_ROWS], axis=1).T
    w_n = jnp.concatenate([w_ga, w_gb], axis=1)

    freq_col = (ROPE_THETA ** (-jnp.arange(HALF_DIM, dtype=F32) / HALF_DIM))[:, None]
    (qn_t, sq_t, ksel, kwin, kswa, vsel, vwin, vswa, ng_t, kcv, gab) = _proj(
        h, w['norm_mix'][None], w_t, w_n, positions.reshape(1, n), freq_col)

    pos_c = jnp.pad(positions[:, CMP_LEN - 1::CMP_STRIDE], ((0, 0), (0, 1)))[:, None, :]
    w1 = lambda a: bf(a.reshape(CMP_LEN, HEAD_DIM, a.shape[-1]))
    kc, vc_t = _compress(kcv, w['cmp_pe_k'], w['cmp_pe_v'], w1(w['cmp_k_w1']), w1(w['cmp_v_w1']),
                         bf(w['cmp_k_w2'].T), bf(w['cmp_v_w2'].T), pos_c, freq_col, b, g_kv, s_len)

    gates = ng_t.reshape(3, g_kv, NSA_REP, n)
    ov = jnp.asarray(_overlap_matrix(n_sel, nc - 1, nc), BF16)
    mb, oc_t = _cmp_topk(qn_t, kc, vc_t, ov, gates, b, g_kv, s_len, n_top)

    expand = jnp.asarray(np.arange(s_len)[:, None] // SEL_LEN == np.arange(n_sel)[None, :], BF16)
    o_a = _nsa(qn_t, mb, oc_t, gates, ksel, expand, vsel, kwin, vwin, b, g_kv, s_len)

    sink_row = jnp.repeat(w['swa_sinks'].astype(F32) * LOG2E, min(SWA_TILE, s_len))[None]
    o_b = _swa(sq_t, sink_row, kswa, vswa, b, s_len)

    branches = (o_a, o_b, gab, bf(w['w_branch_a']), bf(w['w_branch_b']), bf(w['w_out']))
    return _ffn(h, w['norm_ffn2'][None], bf(w['ffn2_gate']), bf(w['ffn2_up']), bf(w['ffn2_down']),
                norm_final[None], last, branches)


def kernel(x, positions, norm_ffn1, ffn1_gate, ffn1_up, ffn1_down, norm_mix, w_in, cmp_pe_k, cmp_k_w1, cmp_k_w2, cmp_pe_v, cmp_v_w1, cmp_v_w2, swa_sinks, w_branch_a, w_branch_b, w_out, norm_ffn2, ffn2_gate, ffn2_up, ffn2_down, norm_final):
    b, s_len, d = x.shape
    stacked = dict(norm_ffn1=norm_ffn1, ffn1_gate=ffn1_gate, ffn1_up=ffn1_up, ffn1_down=ffn1_down,
                   norm_mix=norm_mix, w_in=w_in, cmp_pe_k=cmp_pe_k, cmp_k_w1=cmp_k_w1, cmp_k_w2=cmp_k_w2,
                   cmp_pe_v=cmp_pe_v, cmp_v_w1=cmp_v_w1, cmp_v_w2=cmp_v_w2, swa_sinks=swa_sinks,
                   w_branch_a=w_branch_a, w_branch_b=w_branch_b, w_out=w_out,
                   norm_ffn2=norm_ffn2, ffn2_gate=ffn2_gate, ffn2_up=ffn2_up, ffn2_down=ffn2_down)
    depth = norm_ffn1.shape[0]
    h = x.reshape(b * s_len, d)
    for i in range(depth):
        w = {k: v[i] for k, v in stacked.items()}
        h = _layer(h, positions, w, b, s_len, norm_final, i == depth - 1)
    return h.reshape(b, s_len, d)
```

```python
import functools

import numpy as np
import jax
import jax.numpy as jnp
from jax import lax
from jax.experimental import pallas as pl
from jax.experimental.pallas import tpu as pltpu

HEAD_DIM = 64
HALF_DIM = HEAD_DIM // 2
NSA_HEADS = 8
NSA_KV = 2
NSA_REP = NSA_HEADS // NSA_KV
SWA_HEADS = 8
CMP_STRIDE = 16
CMP_SHIFT = 4
CMP_LEN = 2 * CMP_STRIDE
SEL_LEN = 64
SEL_SHIFT = 6
SEL_TOPN = 16
SEL_LOCAL = 2
NSA_WINDOW = 512
SWA_WINDOW = 128
ROPE_THETA = 10000.0
RMS_EPS = 1e-6
FFN_HALF = 0.5
NEG_INF = -1e30
FORCE = 1e9
LOG2E = 1.4426950408889634
Q_SCALE = HEAD_DIM ** -0.5 * LOG2E
BF16_ROWS = 16
VAL_ROWS = HEAD_DIM + BF16_ROWS
LANES = 128

NSA_Q_W = NSA_HEADS * HEAD_DIM
NSA_KV_W = NSA_KV * HEAD_DIM
NSA_GATE_W = 3 * NSA_HEADS
SWA_Q_W = SWA_HEADS * HEAD_DIM
SWA_KV_W = HEAD_DIM

VMEM_LIMIT_BYTES = 56 * 1024 * 1024

BF16 = jnp.bfloat16
F32 = jnp.float32

ROW_TILE = 512
FFN_CHUNK = 256
PROJ_TILE = 1024
PROJ_CHUNK = 512
CMP_TILE = 1024
CMP_SUBTILE = 256
NSA_TILE = 256
NSA_STEP = 1024
SWA_TILE = 256
SWA_VTILE = 128


def _params(n_axes):
    return pltpu.CompilerParams(dimension_semantics=("arbitrary",) * n_axes,
                                vmem_limit_bytes=VMEM_LIMIT_BYTES)


def _resident(shape):
    zeros = (0,) * len(shape)
    return pl.BlockSpec(shape, lambda *_: zeros, pipeline_mode=pl.Buffered(1))


def _rms(x, g):
    y = x * lax.rsqrt(jnp.mean(x * x, axis=-1, keepdims=True) + RMS_EPS)
    return y * g


def _dot(a, b):
    return jnp.dot(a, b, preferred_element_type=F32)


def _dot_nt(a, b):
    return lax.dot_general(a, b, (((1,), (1,)), ((), ())), preferred_element_type=F32)


def _rope_angles(pos_row, freq_col):
    ang = pos_row.astype(F32) * freq_col
    return jnp.cos(ang), jnp.sin(ang)


def _rope_rows(block, cos_t, sin_t):
    out = []
    for hd in range(block.shape[0] // HEAD_DIM):
        x1 = block[hd * HEAD_DIM:hd * HEAD_DIM + HALF_DIM]
        x2 = block[hd * HEAD_DIM + HALF_DIM:(hd + 1) * HEAD_DIM]
        out += [x1 * cos_t - x2 * sin_t, x2 * cos_t + x1 * sin_t]
    return out


def _ones_rows(tk):
    return jnp.where(lax.broadcasted_iota(jnp.int32, (BF16_ROWS, tk), 0) == 0, 1.0, 0.0).astype(BF16)


def _merge_branches(h_ref, oa_ref, ob_ref, gab_ref, wa_ref, wb_ref, wo_ref):
    d = h_ref.shape[1]
    gab = gab_ref[...]
    merged = gab[:, :d] * _dot(oa_ref[...], wa_ref[...]) + gab[:, d:] * _dot(ob_ref[...], wb_ref[...])
    return h_ref[...] + _dot(merged.astype(BF16), wo_ref[...])


def _ffn_kernel(*refs, final_norm, merge):
    x_ref, g_ref, wg_ref, wu_ref, wd_ref, gf_ref, o_ref = refs[-7:]
    chunk = x_ref.shape[0] if merge else FFN_CHUNK
    for part in range(x_ref.shape[0] // chunk):
        at = pl.ds(part * chunk, chunk)
        x = _merge_branches(x_ref, *refs[:-7]) if merge else x_ref[at, :]
        xb = _rms(x, g_ref[...]).astype(BF16)
        a = _dot(xb, wg_ref[...])
        b = _dot(xb, wu_ref[...])
        t = (a * jax.nn.sigmoid(a)) * b
        h = x + FFN_HALF * _dot(t.astype(BF16), wd_ref[...])
        if final_norm:
            h = _rms(h, gf_ref[...])
        o_ref[at, :] = h


def _ffn(x2, g, wg, wu, wd, gf, final_norm, branches=None):
    n, d = x2.shape
    f = wg.shape[1]
    tm = ROW_TILE
    rows = lambda w: pl.BlockSpec((tm, w), lambda i: (i, 0))
    merge_specs, merge_args = [], ()
    if branches is not None:
        oa, ob, gab, wa, wb, wo = branches
        merge_specs = [rows(oa.shape[1]), rows(ob.shape[1]), rows(2 * d),
                       _resident(wa.shape), _resident(wb.shape), _resident(wo.shape)]
        merge_args = branches
    return pl.pallas_call(
        functools.partial(_ffn_kernel, final_norm=final_norm, merge=branches is not None),
        grid=(n // tm,),
        in_specs=merge_specs + [rows(d), _resident((1, d)), _resident((d, f)), _resident((d, f)),
                                _resident((f, d)), _resident((1, d))],
        out_specs=rows(d),
        out_shape=jax.ShapeDtypeStruct((n, d), F32),
        compiler_params=_params(1),
        name="ffn_final" if final_norm else "ffn",
    )(*merge_args, x2, g, wg, wu, wd, gf)


PROJ_ROWS = (('qn', NSA_Q_W), ('sq', SWA_Q_W), ('ksel', NSA_KV_W), ('kwin', NSA_KV_W), ('kswa', LANES),
             ('vsel', NSA_KV_W), ('vwin', NSA_KV_W), ('vswa', SWA_KV_W), ('ng', 32),
             ('kc', NSA_KV_W), ('vc', NSA_KV_W))


def _proj_kernel(h_ref, g_ref, wt_ref, wn_ref, pos_ref, freq_ref,
                 qn_ref, sq_ref, ksel_ref, kwin_ref, kswa_ref, vsel_ref, vwin_ref, vswa_ref,
                 ng_ref, kcv_ref, gab_ref):
    for part in range(h_ref.shape[0] // PROJ_CHUNK):
        at = pl.ds(part * PROJ_CHUNK, PROJ_CHUNK)
        tiles = lambda ref: ref.at[pl.ds(part * (PROJ_CHUNK // ref.shape[2]), PROJ_CHUNK // ref.shape[2])]
        _proj_chunk(h_ref.at[at, :], g_ref, wt_ref, wn_ref, pos_ref.at[:, at], freq_ref,
                    qn_ref.at[:, at], sq_ref.at[:, at], ksel_ref.at[:, at, :], kwin_ref.at[:, at, :],
                    kswa_ref.at[at, :], tiles(vsel_ref), tiles(vwin_ref), tiles(vswa_ref),
                    ng_ref.at[:, at], kcv_ref.at[:, at, :], gab_ref.at[at, :])


def _proj_chunk(h_ref, g_ref, wt_ref, wn_ref, pos_ref, freq_ref,
                qn_ref, sq_ref, ksel_ref, kwin_ref, kswa_ref, vsel_ref, vwin_ref, vswa_ref,
                ng_ref, kcv_ref, gab_ref):
    ub = _rms(h_ref[...], g_ref[...]).astype(BF16)
    yt = _dot_nt(wt_ref[...], ub)
    rows, o = {}, 0
    for name, width in PROJ_ROWS:
        rows[name] = yt[o:o + width]
        o += width
    cos_t, sin_t = _rope_angles(pos_ref[...], freq_ref[...])

    for name, ref in (('qn', qn_ref), ('sq', sq_ref)):
        for i, piece in enumerate(_rope_rows(rows[name], cos_t, sin_t)):
            ref[i * HALF_DIM:(i + 1) * HALF_DIM, :] = (piece * Q_SCALE).astype(BF16)

    for name, ref in (('ksel', ksel_ref), ('kwin', kwin_ref)):
        k_nat = jnp.concatenate(_rope_rows(rows[name], cos_t, sin_t), axis=0).T
        for g in range(NSA_KV):
            ref[g] = k_nat[:, g * HEAD_DIM:(g + 1) * HEAD_DIM].astype(BF16)
    kswa = _rope_rows(rows['kswa'][:HEAD_DIM], cos_t, sin_t) + [rows['kswa'][HEAD_DIM:]]
    kswa_ref[...] = jnp.concatenate(kswa, axis=0).T[:, :HEAD_DIM].astype(BF16)

    for name, ref, groups in (('vsel', vsel_ref, NSA_KV), ('vwin', vwin_ref, NSA_KV), ('vswa', vswa_ref, 1)):
        tk = ref.shape[2]
        for j in range(ref.shape[0]):
            for g in range(groups):
                ref[j, g * VAL_ROWS:g * VAL_ROWS + HEAD_DIM, :] = (
                    rows[name][g * HEAD_DIM:(g + 1) * HEAD_DIM, j * tk:(j + 1) * tk].astype(BF16))
                ref[j, g * VAL_ROWS + HEAD_DIM:(g + 1) * VAL_ROWS, :] = _ones_rows(tk)

    ng_ref[...] = jax.nn.sigmoid(rows['ng'][:NSA_GATE_W])

    for i, name in enumerate(('kc', 'vc')):
        nat = rows[name].T
        for g in range(NSA_KV):
            kcv_ref[i * NSA_KV + g] = nat[:, g * HEAD_DIM:(g + 1) * HEAD_DIM]

    gab_ref[...] = jax.nn.sigmoid(_dot(ub, wn_ref[...]))


def _proj(h2, g, wt, wn, pos_row, freq_col):
    n, d = h2.shape
    tm = PROJ_TILE
    rows = lambda w: pl.BlockSpec((tm, w), lambda i: (i, 0))
    cols = lambda w: pl.BlockSpec((w, tm), lambda i: (0, i))
    grouped = lambda k: pl.BlockSpec((k, tm, HEAD_DIM), lambda i: (0, i, 0))
    tiles = lambda groups, tk: pl.BlockSpec((tm // tk, groups * VAL_ROWS, tk), lambda i: (i, 0, 0))
    val_shape = lambda groups, tk: jax.ShapeDtypeStruct((n // tk, groups * VAL_ROWS, tk), BF16)
    out_shape = [
        jax.ShapeDtypeStruct((NSA_Q_W, n), BF16), jax.ShapeDtypeStruct((SWA_Q_W, n), BF16),
        jax.ShapeDtypeStruct((NSA_KV, n, HEAD_DIM), BF16), jax.ShapeDtypeStruct((NSA_KV, n, HEAD_DIM), BF16),
        jax.ShapeDtypeStruct((n, HEAD_DIM), BF16),
        val_shape(NSA_KV, NSA_TILE), val_shape(NSA_KV, NSA_TILE), val_shape(1, SWA_VTILE),
        jax.ShapeDtypeStruct((NSA_GATE_W, n), F32), jax.ShapeDtypeStruct((2 * NSA_KV, n, HEAD_DIM), F32),
        jax.ShapeDtypeStruct((n, 2 * d), F32),
    ]
    return pl.pallas_call(
        _proj_kernel,
        grid=(n // tm,),
        in_specs=[rows(d), _resident((1, d)), _resident(wt.shape), _resident(wn.shape),
                  cols(1), _resident(freq_col.shape)],
        out_specs=[cols(NSA_Q_W), cols(SWA_Q_W), grouped(NSA_KV), grouped(NSA_KV), rows(HEAD_DIM),
                   tiles(NSA_KV, NSA_TILE), tiles(NSA_KV, NSA_TILE), tiles(1, SWA_VTILE),
                   cols(NSA_GATE_W), grouped(2 * NSA_KV), rows(2 * d)],
        out_shape=out_shape,
        compiler_params=_params(1),
        name="proj",
    )(h2, g, wt, wn, pos_row, freq_col)


def _gelu(x):
    return jax.nn.gelu(x, approximate=True)


def _compress_kernel(kc_ref, vc_ref, pek_ref, pev_ref, w1k_ref, w1v_ref, w2kt_ref, w2vt_ref,
                     pos_ref, freq_ref, kc_out, vct_out):
    nc = kc_out.shape[1]

    def hidden(x_ref, pe_ref, w1_ref):
        top = bot = None
        for j in range(CMP_STRIDE):
            x = x_ref[0, pl.ds(j, nc, stride=CMP_STRIDE), :]
            t = _dot((x + pe_ref[j:j + 1, :]).astype(BF16), w1_ref[j])
            b = _dot((x + pe_ref[CMP_STRIDE + j:CMP_STRIDE + j + 1, :]).astype(BF16), w1_ref[CMP_STRIDE + j])
            top, bot = (t, b) if top is None else (top + t, bot + b)
        return _gelu(top + pltpu.roll(bot, shift=nc - 1, axis=0)).astype(BF16)

    kt = _dot_nt(w2kt_ref[...], hidden(kc_ref, pek_ref, w1k_ref))
    cos_t, sin_t = _rope_angles(pos_ref[0], freq_ref[...])
    kt = jnp.concatenate(_rope_rows(kt, cos_t, sin_t) + [jnp.zeros((LANES - HEAD_DIM, nc), F32)], axis=0)
    kc_out[0] = kt.T[:, :HEAD_DIM].astype(BF16)
    vct_out[0] = _dot_nt(w2vt_ref[...], hidden(vc_ref, pev_ref, w1v_ref)).astype(BF16)


def _compress(kcv, pek, pev, w1k, w1v, w2kt, w2vt, pos_c, freq_col, b, g_kv, s_len):
    nc = s_len // CMP_STRIDE
    hid = w1k.shape[-1]
    kcv = kcv.reshape(2 * g_kv * b, s_len, HEAD_DIM)
    src = lambda kind: pl.BlockSpec((1, s_len, HEAD_DIM),
                                    lambda i: ((kind * g_kv + i % g_kv) * b + i // g_kv, 0, 0))
    per = lambda *s: pl.BlockSpec((1,) + s, lambda i: (i,) + (0,) * len(s))
    return pl.pallas_call(
        _compress_kernel,
        grid=(b * g_kv,),
        in_specs=[src(0), src(1), _resident(pek.shape), _resident(pev.shape),
                  _resident(w1k.shape), _resident(w1v.shape), _resident((HEAD_DIM, hid)),
                  _resident((HEAD_DIM, hid)),
                  pl.BlockSpec((1, 1, nc), lambda i: (i // g_kv, 0, 0)), _resident(freq_col.shape)],
        out_specs=[per(nc, HEAD_DIM), per(HEAD_DIM, nc)],
        out_shape=[jax.ShapeDtypeStruct((b * g_kv, nc, HEAD_DIM), BF16),
                   jax.ShapeDtypeStruct((b * g_kv, HEAD_DIM, nc), BF16)],
        compiler_params=_params(1),
        name="compress",
    )(kcv, kcv, pek, pev, w1k, w1v, w2kt, w2vt, pos_c, freq_col)


def _cmp_topk_body(q_ref, kc_ref, vct_ref, ov_ref, gate_ref, mb_ref, oc_ref,
                   s_buf, p_buf, bias_buf, psum_buf, x_buf, *, n_top, nc, n_sel):
    tq = q_ref.shape[1]
    sub = s_buf.shape[1]
    rep = NSA_REP
    qi = pl.program_id(2)
    q0 = qi * tq
    stages = [(u, r) for u in range(tq // sub) for r in range(rep)]
    q_cols = lambda u, r: q_ref[r * HEAD_DIM:(r + 1) * HEAD_DIM, u * sub:(u + 1) * sub]
    keys = kc_ref[0, 0:nc, :]
    vals = vct_ref[0, :, 0:nc]

    s_buf[0:nc] = _dot(keys, q_cols(*stages[0]))
    inv_prev = None
    for n, (u, r) in enumerate(stages):
        cols = slice(u * sub, (u + 1) * sub)
        if n + 1 < len(stages):
            s_next = _dot(keys, q_cols(*stages[n + 1]))
        if n > 0:
            pu, pr = stages[n - 1]
            oc_ref[pr * HEAD_DIM:(pr + 1) * HEAD_DIM, pu * sub:(pu + 1) * sub] = (
                _dot(vals, p_buf[0:nc]) * (inv_prev * gate_ref[0, 0, pr:pr + 1, pu * sub:(pu + 1) * sub]))
        last = ((q0 + u * sub + lax.broadcasted_iota(jnp.int32, (1, sub), 1)) - (CMP_LEN - 1)) >> CMP_SHIFT
        if r == 0:
            bias_buf[0:nc] = jnp.where(lax.broadcasted_iota(jnp.int32, (nc, sub), 0) <= last, 0.0, NEG_INF)
        s = s_buf[0:nc] + bias_buf[0:nc]
        m = jnp.max(s, axis=0, keepdims=True)
        e = jnp.exp2(s - m)
        inv_prev = jnp.where(last >= 0, 1.0 / jnp.sum(e, axis=0, keepdims=True), 0.0)
        p_buf[0:nc] = e.astype(BF16)
        if r == 0:
            psum_buf[0:nc, cols] = e * inv_prev
        else:
            psum_buf[0:nc, cols] += e * inv_prev
        if n + 1 < len(stages):
            s_buf[0:nc] = s_next
    pu, pr = stages[-1]
    oc_ref[pr * HEAD_DIM:(pr + 1) * HEAD_DIM, pu * sub:(pu + 1) * sub] = (
        _dot(vals, p_buf[0:nc]) * (inv_prev * gate_ref[0, 0, pr:pr + 1, pu * sub:(pu + 1) * sub]))

    psum = psum_buf[0:nc, :]
    p_hi = psum.astype(BF16)
    p_lo = (psum - p_hi.astype(F32)).astype(BF16)
    pool = ov_ref[0:n_sel, 0:nc]
    imp = _dot(pool, p_hi) + _dot(pool, p_lo)

    blk = lax.broadcasted_iota(jnp.int32, (n_sel, tq), 0)
    tb = (q0 + lax.broadcasted_iota(jnp.int32, (n_sel, tq), 1)) >> SEL_SHIFT
    forced = (blk == 0) | ((tb - blk >= 0) & (tb - blk < SEL_LOCAL))
    n_forced = 1 + SEL_LOCAL
    premark = forced & (jnp.full((n_sel, tq), qi, jnp.int32) > 0)
    x_buf[0:n_sel] = jnp.where(premark, -jnp.inf, jnp.where(forced, FORCE, jnp.where(blk > tb, -FORCE, imp)))

    blk_f = blk.astype(F32)

    def pick(_, carry):
        x = x_buf[0:n_sel]
        top = jnp.max(x, axis=0, keepdims=True)
        first = jnp.min(jnp.where(x == top, blk_f, float(n_sel)), axis=0, keepdims=True)
        x_buf[0:n_sel] = jnp.where(blk_f == first, -jnp.inf, x)
        return carry

    lax.fori_loop(0, jnp.where(qi > 0, n_top - n_forced, n_top), pick, 0)
    mb_ref[0, 0:n_sel, :] = jnp.where(x_buf[0:n_sel] == -jnp.inf, 0.0, NEG_INF).astype(BF16)
    if n_sel < mb_ref.shape[1]:
        mb_ref[0, n_sel:, :] = jnp.full((mb_ref.shape[1] - n_sel, tq), NEG_INF, BF16)


def _cmp_topk_kernel(*refs, n_top, n_tiles):
    nc = refs[1].shape[1]
    n_sel = refs[3].shape[0]
    qi = pl.program_id(2)
    for v in range(max(n_tiles // 2, 1)):
        frac = lambda total: min(total, (2 * v + 2) * total // n_tiles)
        pl.when((qi >> 1) == v)(functools.partial(_cmp_topk_body, *refs, n_top=n_top, nc=frac(nc), n_sel=frac(n_sel)))


def _cmp_topk(qn_t, kc, vct, ov, gates, b, g_kv, s_len, n_top):
    tq = min(CMP_TILE, s_len)
    sub = min(CMP_SUBTILE, tq)
    nq = s_len // tq
    nc = kc.shape[1]
    n_sel = ov.shape[0]
    rep = NSA_REP
    return pl.pallas_call(
        functools.partial(_cmp_topk_kernel, n_top=n_top, n_tiles=nq),
        grid=(b, g_kv, nq),
        in_specs=[
            pl.BlockSpec((rep * HEAD_DIM, tq), lambda bi, gi, qi: (gi, bi * nq + qi)),
            pl.BlockSpec((1, nc, HEAD_DIM), lambda bi, gi, qi: (bi * g_kv + gi, 0, 0)),
            pl.BlockSpec((1, HEAD_DIM, nc), lambda bi, gi, qi: (bi * g_kv + gi, 0, 0)),
            _resident(ov.shape),
            pl.BlockSpec((1, 1, rep, tq), lambda bi, gi, qi: (0, gi, 0, bi * nq + qi)),
        ],
        out_specs=[
            pl.BlockSpec((1, n_sel, tq), lambda bi, gi, qi: (bi * g_kv + gi, 0, qi)),
            pl.BlockSpec((rep * HEAD_DIM, tq), lambda bi, gi, qi: (gi, bi * nq + qi)),
        ],
        out_shape=[jax.ShapeDtypeStruct((b * g_kv, n_sel, s_len), BF16),
                   jax.ShapeDtypeStruct((NSA_Q_W, b * s_len), F32)],
        scratch_shapes=[pltpu.VMEM((nc, sub), F32), pltpu.VMEM((nc, sub), BF16), pltpu.VMEM((nc, sub), F32),
                        pltpu.VMEM((nc, tq), F32), pltpu.VMEM((n_sel, tq), F32)],
        compiler_params=_params(3),
        name="cmp_topk",
    )(qn_t, kc, vct, ov, gates)


def _softmax_tile(m, s):
    m_new = jnp.maximum(m, jnp.max(s, axis=0, keepdims=True))
    return m_new, jnp.exp2(m - m_new), jnp.exp2(s - m_new).astype(BF16)


def _normalize(acc):
    return acc[:HEAD_DIM] / acc[HEAD_DIM:HEAD_DIM + 1]


def _rel_pos(tk, tq):
    return lax.broadcasted_iota(jnp.int32, (tk, tq), 1) - lax.broadcasted_iota(jnp.int32, (tk, tq), 0)


def _window_bias(rel, back, tk, kj, window):
    dist = rel + (back * tk + jnp.where(kj >= 0, 0, window))
    return jnp.where((dist >= 0) & (dist < window), 0.0, NEG_INF)


def _pipe_stage(s_buf, p_buf, heads, cur=None, nxt=None, prv=None):
    if nxt is not None:
        s_next = [_dot(nxt[0], nxt[1][r][...]) for r in heads]
    if prv is not None:
        pv = [_dot(prv[0], p_buf[r]) for r in heads]
    out = None
    if cur is not None:
        m, bias = cur
        out = []
        for r in heads:
            s = s_buf[r] if bias is None else s_buf[r] + bias
            m_r, a_r, p_buf[r] = _softmax_tile(m[r], s)
            out.append((m_r, a_r))
        out = tuple(zip(*out))
    for r in heads:
        if prv is not None:
            prv[1][r] = prv[2][r] * prv[1][r] + pv[r]
        if nxt is not None:
            s_buf[r] = s_next[r]
    return out


def _nsa_kernel(q_ref, mb_ref, oc_ref, gate_ref, ks_ref, ex_ref, vs_ref, kw_ref, vw_ref, o_ref, *scratch):
    n_sub = q_ref.shape[1] // NSA_TILE
    per_tile = len(scratch) // n_sub
    for sub in range(n_sub):
        at = pl.ds(sub * NSA_TILE, NSA_TILE)
        _nsa_tile(pl.program_id(2) * n_sub + sub, q_ref.at[:, at], mb_ref.at[:, :, at], oc_ref.at[:, at],
                  gate_ref.at[:, :, :, at], ks_ref, ex_ref, vs_ref, kw_ref, vw_ref, o_ref.at[at, :],
                  *scratch[sub * per_tile:(sub + 1) * per_tile])


def _nsa_tile(qi, q_ref, mb_ref, oc_ref, gate_ref, ks_ref, ex_ref, vs_ref, kw_ref, vw_ref, o_ref,
              qa_buf, s_buf, p_buf, sel_acc, win_acc):
    tq = q_ref.shape[1]
    tk = tq
    n_sel = mb_ref.shape[1]
    rep = NSA_REP
    heads = range(rep)
    stage = functools.partial(_pipe_stage, s_buf, p_buf, heads)
    key_rows = lambda j: pl.ds(pl.multiple_of(j * tk, tk), tk)

    def sel_keys(j):
        return jnp.concatenate([ex_ref[key_rows(j), :], ks_ref[0, key_rows(j), :]], axis=1)

    q_heads = [q_ref.at[pl.ds(r * HEAD_DIM, HEAD_DIM), :] for r in heads]
    q_aug = [qa_buf.at[r] for r in heads]
    for r in heads:
        qa_buf[r, 0:n_sel, :] = mb_ref[0]
        qa_buf[r, n_sel:n_sel + HEAD_DIM, :] = q_heads[r][...]
        s_buf[r] = _dot(sel_keys(0), qa_buf[r])
        p_buf[r] = jnp.zeros((tk, tq), BF16)
        sel_acc[r] = jnp.zeros((VAL_ROWS, tq), F32)
        win_acc[r] = jnp.zeros((VAL_ROWS, tq), F32)
    row = lambda v: tuple(jnp.full((1, tq), v, F32) for _ in heads)

    def sel_stage(i, carry):
        m, alpha = carry
        return stage(cur=(m, None), nxt=(sel_keys(i + 1), q_aug),
                     prv=(vs_ref[jnp.maximum(i - 1, 0)], sel_acc, alpha))

    carry = lax.fori_loop(0, qi >> 1, lambda j, c: sel_stage(2 * j + 1, sel_stage(2 * j, c)),
                          (row(NEG_INF), row(1.0)))
    m, alpha = lax.fori_loop(0, qi & 1, lambda _, c: sel_stage(qi - 1, c), carry)

    rel = _rel_pos(tk, tq)
    backs = list(range(NSA_WINDOW // tk, -1, -1))
    win_tile = [jnp.maximum(qi - back, 0) for back in backs]
    win_bias = [_window_bias(rel, back, tk, qi - back, NSA_WINDOW) for back in backs]
    _, alpha_d = stage(cur=(m, jnp.where(rel >= 0, 0.0, NEG_INF)),
                       nxt=(kw_ref[0, key_rows(win_tile[0]), :], q_heads),
                       prv=(vs_ref[jnp.maximum(qi - 1, 0)], sel_acc, alpha))
    prv = (vs_ref[qi], sel_acc, alpha_d)
    m_w = row(NEG_INF)
    for n in range(len(backs)):
        nxt = (kw_ref[0, key_rows(win_tile[n + 1]), :], q_heads) if n + 1 < len(backs) else None
        m_w, alpha_w = stage(cur=(m_w, win_bias[n]), nxt=nxt, prv=prv)
        prv = (vw_ref[win_tile[n]], win_acc, alpha_w)
    stage(prv=prv)

    outs = [oc_ref[r * HEAD_DIM:(r + 1) * HEAD_DIM, :]
            + gate_ref[1, 0, r:r + 1, :] * _normalize(sel_acc[r])
            + gate_ref[2, 0, r:r + 1, :] * _normalize(win_acc[r]) for r in heads]
    o_ref[...] = jnp.concatenate(outs, axis=0).T.astype(o_ref.dtype)


def _nsa(qn_t, mb, oc_t, gates, ksel, expand, vsel, kwin, vwin, b, g_kv, s_len):
    tk = NSA_TILE
    tq = min(NSA_STEP, s_len)
    nq = s_len // tq
    rep = NSA_REP
    n_sel = mb.shape[1]
    keys = pl.BlockSpec((1, s_len, HEAD_DIM), lambda bi, gi, qi: (gi * b + bi, 0, 0))
    vals = pl.BlockSpec((s_len // tk, VAL_ROWS, tk), lambda bi, gi, qi: (bi, gi, 0))
    return pl.pallas_call(
        _nsa_kernel,
        grid=(b, g_kv, nq),
        in_specs=[
            pl.BlockSpec((rep * HEAD_DIM, tq), lambda bi, gi, qi: (gi, bi * nq + qi)),
            pl.BlockSpec((1, n_sel, tq), lambda bi, gi, qi: (bi * g_kv + gi, 0, qi)),
            pl.BlockSpec((rep * HEAD_DIM, tq), lambda bi, gi, qi: (gi, bi * nq + qi)),
            pl.BlockSpec((3, 1, rep, tq), lambda bi, gi, qi: (0, gi, 0, bi * nq + qi)),
            keys, _resident(expand.shape), vals, keys, vals,
        ],
        out_specs=pl.BlockSpec((tq, rep * HEAD_DIM), lambda bi, gi, qi: (bi * nq + qi, gi)),
        out_shape=jax.ShapeDtypeStruct((b * s_len, NSA_Q_W), BF16),
        scratch_shapes=[pltpu.VMEM((rep, n_sel + HEAD_DIM, tk), BF16), pltpu.VMEM((rep, tk, tk), F32),
                        pltpu.VMEM((rep, tk, tk), BF16), pltpu.VMEM((rep, VAL_ROWS, tk), F32),
                        pltpu.VMEM((rep, VAL_ROWS, tk), F32)] * (tq // tk),
        compiler_params=_params(3),
        name="nsa",
    )(qn_t, mb, oc_t, gates, ksel.reshape(g_kv * b, s_len, HEAD_DIM), expand, vsel,
      kwin.reshape(g_kv * b, s_len, HEAD_DIM), vwin)


def _swa_kernel(q_ref, sink_ref, k_ref, v_ref, o_ref, s_buf, p_buf, bias_buf, o_buf):
    tq = q_ref.shape[1]
    tv = v_ref.shape[2]
    nk = s_buf.shape[0]
    heads = SWA_HEADS
    qi = pl.program_id(1)
    q0 = qi * tq
    k0 = jnp.maximum(q0 - SWA_WINDOW, 0)

    keys = k_ref[0, pl.ds(pl.multiple_of(k0, tv), nk), :]
    vals = jnp.concatenate([v_ref[k0 // tv + j] for j in range(nk // tv)], axis=1)
    dist = _rel_pos(nk, tq) + (q0 - k0)
    bias_buf[...] = jnp.where((dist >= 0) & (dist < SWA_WINDOW), 0.0, NEG_INF)
    q_head = lambda h: q_ref[h * HEAD_DIM:(h + 1) * HEAD_DIM, :]

    def finish(h, m_all, sink):
        acc = _dot(vals, p_buf[...])
        o_buf[h * HEAD_DIM:(h + 1) * HEAD_DIM, :] = (
            acc[:HEAD_DIM] / (acc[HEAD_DIM:HEAD_DIM + 1] + jnp.exp2(sink - m_all)))

    s_buf[...] = _dot(keys, q_head(0))
    prev = None
    for h in range(heads):
        if h + 1 < heads:
            s_next = _dot(keys, q_head(h + 1))
        if prev is not None:
            finish(*prev)
        sink = sink_ref[:, h * tq:(h + 1) * tq]
        s = s_buf[...] + bias_buf[...]
        m_all = jnp.maximum(jnp.max(s, axis=0, keepdims=True), sink)
        p_buf[...] = jnp.exp2(s - m_all).astype(BF16)
        prev = (h, m_all, sink)
        if h + 1 < heads:
            s_buf[...] = s_next
    finish(*prev)
    o_ref[...] = o_buf[...].T.astype(o_ref.dtype)


def _swa(sq_t, sink_row, k, v, b, s_len):
    tq = min(SWA_TILE, s_len)
    nq = s_len // tq
    tv = v.shape[2]
    nk = tq + SWA_WINDOW
    return pl.pallas_call(
        _swa_kernel,
        grid=(b, nq),
        in_specs=[pl.BlockSpec((SWA_Q_W, tq), lambda bi, qi: (0, bi * nq + qi)),
                  _resident(sink_row.shape),
                  pl.BlockSpec((1, s_len, HEAD_DIM), lambda bi, qi: (bi, 0, 0)),
                  pl.BlockSpec((s_len // tv, VAL_ROWS, tv), lambda bi, qi: (bi, 0, 0))],
        out_specs=pl.BlockSpec((tq, SWA_Q_W), lambda bi, qi: (bi * nq + qi, 0)),
        out_shape=jax.ShapeDtypeStruct((b * s_len, SWA_Q_W), BF16),
        scratch_shapes=[pltpu.VMEM((nk, tq), F32), pltpu.VMEM((nk, tq), BF16), pltpu.VMEM((nk, tq), F32),
                        pltpu.VMEM((SWA_Q_W, tq), F32)],
        compiler_params=_params(2),
        name="swa",
    )(sq_t, sink_row, k.reshape(b, s_len, HEAD_DIM), v)


def _overlap_matrix(n_sel, n_cmp, n_cmp_pad):
    cs = np.arange(n_cmp) * CMP_STRIDE
    ss = np.arange(n_sel) * SEL_LEN
    ov = np.clip(np.minimum(cs[None, :] + CMP_LEN, ss[:, None] + SEL_LEN)
                 - np.maximum(cs[None, :], ss[:, None]), 0, None).astype(np.float32) / CMP_LEN
    return np.pad(ov, ((0, 0), (0, n_cmp_pad - n_cmp)))


def _layer(h, positions, w, b, s_len, norm_final, last):
    n, d = h.shape
    g_kv = NSA_KV
    n_sel = s_len // SEL_LEN
    n_top = min(SEL_TOPN, n_sel)
    nc = s_len // CMP_STRIDE
    bf = lambda a: a.astype(BF16)

    h = _ffn(h, w['norm_ffn1'][None], bf(w['ffn1_gate']), bf(w['ffn1_up']), bf(w['ffn1_down']),
             w['norm_ffn1'][None], False)

    pts = np.cumsum((NSA_Q_W,) + (NSA_KV_W,) * 6 + (NSA_GATE_W, SWA_Q_W, SWA_KV_W, SWA_KV_W, d, d))[:-1]
    (w_nq, w_kc, w_vc, w_ksl, w_vsl, w_kwn, w_vwn, w_ng, w_sq, w_sk, w_sv, w_ga, w_gb) = jnp.split(
        bf(w['w_in']), pts, axis=1)
    parts = dict(qn=w_nq, sq=w_sq, ksel=w_ksl, kwin=w_kwn, kswa=w_sk, vsel=w_vsl, vwin=w_vwn, vswa=w_sv,
                 ng=w_ng, kc=w_kc, vc=w_vc)
    w_t = jnp.concatenate([jnp.pad(parts[name], ((0, 0), (0, width - parts[name].shape[1])))
                           for name, width in PROJ_ROWS], axis=1).T
    w_n = jnp.concatenate([w_ga, w_gb], axis=1)

    freq_col = (ROPE_THETA ** (-jnp.arange(HALF_DIM, dtype=F32) / HALF_DIM))[:, None]
    (qn_t, sq_t, ksel, kwin, kswa, vsel, vwin, vswa, ng_t, kcv, gab) = _proj(
        h, w['norm_mix'][None], w_t, w_n, positions.reshape(1, n), freq_col)

    pos_c = jnp.pad(positions[:, CMP_LEN - 1::CMP_STRIDE], ((0, 0), (0, 1)))[:, None, :]
    w1 = lambda a: bf(a.reshape(CMP_LEN, HEAD_DIM, a.shape[-1]))
    kc, vc_t = _compress(kcv, w['cmp_pe_k'], w['cmp_pe_v'], w1(w['cmp_k_w1']), w1(w['cmp_v_w1']),
                         bf(w['cmp_k_w2'].T), bf(w['cmp_v_w2'].T), pos_c, freq_col, b, g_kv, s_len)

    gates = ng_t.reshape(3, g_kv, NSA_REP, n)
    ov = jnp.asarray(_overlap_matrix(n_sel, nc - 1, nc), BF16)
    mb, oc_t = _cmp_topk(qn_t, kc, vc_t, ov, gates, b, g_kv, s_len, n_top)

    expand = jnp.asarray(np.arange(s_len)[:, None] // SEL_LEN == np.arange(n_sel)[None, :], BF16)
    o_a = _nsa(qn_t, mb, oc_t, gates, ksel, expand, vsel, kwin, vwin, b, g_kv, s_len)

    sink_row = jnp.repeat(w['swa_sinks'].astype(F32) * LOG2E, min(SWA_TILE, s_len))[None]
    o_b = _swa(sq_t, sink_row, kswa, vswa, b, s_len)

    branches = (o_a, o_b, gab, bf(w['w_branch_a']), bf(w['w_branch_b']), bf(w['w_out']))
    return _ffn(h, w['norm_ffn2'][None], bf(w['ffn2_gate']), bf(w['ffn2_up']), bf(w['ffn2_down']),
                norm_final[None], last, branches)


def kernel(x, positions, norm_ffn1, ffn1_gate, ffn1_up, ffn1_down, norm_mix, w_in, cmp_pe_k, cmp_k_w1, cmp_k_w2, cmp_pe_v, cmp_v_w1, cmp_v_w2, swa_sinks, w_branch_a, w_branch_b, w_out, norm_ffn2, ffn2_gate, ffn2_up, ffn2_down, norm_final):
    b, s_len, d = x.shape
    stacked = dict(norm_ffn1=norm_ffn1, ffn1_gate=ffn1_gate, ffn1_up=ffn1_up, ffn1_down=ffn1_down,
                   norm_mix=norm_mix, w_in=w_in, cmp_pe_k=cmp_pe_k, cmp_k_w1=cmp_k_w1, cmp_k_w2=cmp_k_w2,
                   cmp_pe_v=cmp_pe_v, cmp_v_w1=cmp_v_w1, cmp_v_w2=cmp_v_w2, swa_sinks=swa_sinks,
                   w_branch_a=w_branch_a, w_branch_b=w_branch_b, w_out=w_out,
                   norm_ffn2=norm_ffn2, ffn2_gate=ffn2_gate, ffn2_up=ffn2_up, ffn2_down=ffn2_down)
    depth = norm_ffn1.shape[0]
    h = x.reshape(b * s_len, d)
    for i in range(depth):
        w = {k: v[i] for k, v in stacked.items()}
        h = _layer(h, positions, w, b, s_len, norm_final, i == depth - 1)
    return h.reshape(b, s_len, d)
```

```python
import functools

import numpy as np
import jax
import jax.numpy as jnp
from jax import lax
from jax.experimental import pallas as pl
from jax.experimental.pallas import tpu as pltpu

HEAD_DIM = 64
HALF_DIM = HEAD_DIM // 2
NSA_HEADS = 8
NSA_KV = 2
NSA_REP = NSA_HEADS // NSA_KV
SWA_HEADS = 8
CMP_STRIDE = 16
CMP_SHIFT = 4
CMP_LEN = 2 * CMP_STRIDE
SEL_LEN = 64
SEL_SHIFT = 6
SEL_TOPN = 16
SEL_LOCAL = 2
NSA_WINDOW = 512
SWA_WINDOW = 128
ROPE_THETA = 10000.0
RMS_EPS = 1e-6
FFN_HALF = 0.5
NEG_INF = -1e30
FORCE = 1e9
LOG2E = 1.4426950408889634
Q_SCALE = HEAD_DIM ** -0.5 * LOG2E
BF16_ROWS = 16
VAL_ROWS = HEAD_DIM + BF16_ROWS
LANES = 128

NSA_Q_W = NSA_HEADS * HEAD_DIM
NSA_KV_W = NSA_KV * HEAD_DIM
NSA_GATE_W = 3 * NSA_HEADS
SWA_Q_W = SWA_HEADS * HEAD_DIM
SWA_KV_W = HEAD_DIM

VMEM_LIMIT_BYTES = 56 * 1024 * 1024

BF16 = jnp.bfloat16
F32 = jnp.float32

ROW_TILE = 512
CMP_TILE = 1024
CMP_SUBTILE = 256
NSA_TILE = 256
NSA_STEP = 1024
SWA_TILE = 256
SWA_STEP = 1024
SWA_VTILE = 128


def _params(n_axes):
    return pltpu.CompilerParams(dimension_semantics=("arbitrary",) * n_axes,
                                vmem_limit_bytes=VMEM_LIMIT_BYTES)


def _resident(shape):
    zeros = (0,) * len(shape)
    return pl.BlockSpec(shape, lambda *_: zeros, pipeline_mode=pl.Buffered(1))


def _rms(x, g):
    y = x * lax.rsqrt(jnp.mean(x * x, axis=-1, keepdims=True) + RMS_EPS)
    return y * g


def _dot(a, b):
    return jnp.dot(a, b, preferred_element_type=F32)


def _dot_nt(a, b):
    return lax.dot_general(a, b, (((1,), (1,)), ((), ())), preferred_element_type=F32)


def _rope_angles(pos_row, freq_col):
    ang = pos_row.astype(F32) * freq_col
    return jnp.cos(ang), jnp.sin(ang)


def _rope_rows(block, cos_t, sin_t):
    out = []
    for hd in range(block.shape[0] // HEAD_DIM):
        x1 = block[hd * HEAD_DIM:hd * HEAD_DIM + HALF_DIM]
        x2 = block[hd * HEAD_DIM + HALF_DIM:(hd + 1) * HEAD_DIM]
        out += [x1 * cos_t - x2 * sin_t, x2 * cos_t + x1 * sin_t]
    return out


def _ones_rows(tk):
    return jnp.where(lax.broadcasted_iota(jnp.int32, (BF16_ROWS, tk), 0) == 0, 1.0, 0.0).astype(BF16)


def _merge_branches(h_ref, oa_ref, ob_ref, gab_ref, wa_ref, wb_ref, wo_ref):
    d = h_ref.shape[1]
    gab = gab_ref[...]
    merged = gab[:, :d] * _dot(oa_ref[...], wa_ref[...]) + gab[:, d:] * _dot(ob_ref[...], wb_ref[...])
    return h_ref[...] + _dot(merged.astype(BF16), wo_ref[...])


def _ffn_kernel(*refs, final_norm, merge):
    x_ref, g_ref, wg_ref, wu_ref, wd_ref, gf_ref, o_ref = refs[-7:]
    x = _merge_branches(x_ref, *refs[:-7]) if merge else x_ref[...]
    xb = _rms(x, g_ref[...]).astype(BF16)
    a = _dot(xb, wg_ref[...])
    b = _dot(xb, wu_ref[...])
    t = (a * jax.nn.sigmoid(a)) * b
    h = x + FFN_HALF * _dot(t.astype(BF16), wd_ref[...])
    if final_norm:
        h = _rms(h, gf_ref[...])
    o_ref[...] = h


def _ffn(x2, g, wg, wu, wd, gf, final_norm, branches=None):
    n, d = x2.shape
    f = wg.shape[1]
    tm = ROW_TILE
    rows = lambda w: pl.BlockSpec((tm, w), lambda i: (i, 0))
    merge_specs, merge_args = [], ()
    if branches is not None:
        oa, ob, gab, wa, wb, wo = branches
        merge_specs = [rows(oa.shape[1]), rows(ob.shape[1]), rows(2 * d),
                       _resident(wa.shape), _resident(wb.shape), _resident(wo.shape)]
        merge_args = branches
    return pl.pallas_call(
        functools.partial(_ffn_kernel, final_norm=final_norm, merge=branches is not None),
        grid=(n // tm,),
        in_specs=merge_specs + [rows(d), _resident((1, d)), _resident((d, f)), _resident((d, f)),
                                _resident((f, d)), _resident((1, d))],
        out_specs=rows(d),
        out_shape=jax.ShapeDtypeStruct((n, d), F32),
        compiler_params=_params(1),
        name="ffn_final" if final_norm else "ffn",
    )(*merge_args, x2, g, wg, wu, wd, gf)


PROJ_ROWS = (('qn', NSA_Q_W), ('sq', SWA_Q_W), ('ksel', NSA_KV_W), ('kwin', NSA_KV_W), ('kswa', LANES),
             ('vsel', NSA_KV_W), ('vwin', NSA_KV_W), ('vswa', SWA_KV_W), ('ng', 32),
             ('kc', NSA_KV_W), ('vc', NSA_KV_W))


def _proj_kernel(h_ref, g_ref, wt_ref, wn_ref, pos_ref, freq_ref,
                 qn_ref, sq_ref, ksel_ref, kwin_ref, kswa_ref, vsel_ref, vwin_ref, vswa_ref,
                 ng_ref, kcv_ref, gab_ref):
    ub = _rms(h_ref[...], g_ref[...]).astype(BF16)
    yt = _dot_nt(wt_ref[...], ub)
    rows, o = {}, 0
    for name, width in PROJ_ROWS:
        rows[name] = yt[o:o + width]
        o += width
    cos_t, sin_t = _rope_angles(pos_ref[...], freq_ref[...])

    for name, ref in (('qn', qn_ref), ('sq', sq_ref)):
        for i, piece in enumerate(_rope_rows(rows[name], cos_t, sin_t)):
            ref[i * HALF_DIM:(i + 1) * HALF_DIM, :] = (piece * Q_SCALE).astype(BF16)

    for name, ref in (('ksel', ksel_ref), ('kwin', kwin_ref)):
        k_nat = jnp.concatenate(_rope_rows(rows[name], cos_t, sin_t), axis=0).T
        for g in range(NSA_KV):
            ref[g] = k_nat[:, g * HEAD_DIM:(g + 1) * HEAD_DIM].astype(BF16)
    kswa = _rope_rows(rows['kswa'][:HEAD_DIM], cos_t, sin_t) + [rows['kswa'][HEAD_DIM:]]
    kswa_ref[...] = jnp.concatenate(kswa, axis=0).T[:, :HEAD_DIM].astype(BF16)

    for name, ref, groups in (('vsel', vsel_ref, NSA_KV), ('vwin', vwin_ref, NSA_KV), ('vswa', vswa_ref, 1)):
        tk = ref.shape[2]
        for j in range(ref.shape[0]):
            for g in range(groups):
                ref[j, g * VAL_ROWS:g * VAL_ROWS + HEAD_DIM, :] = (
                    rows[name][g * HEAD_DIM:(g + 1) * HEAD_DIM, j * tk:(j + 1) * tk].astype(BF16))
                ref[j, g * VAL_ROWS + HEAD_DIM:(g + 1) * VAL_ROWS, :] = _ones_rows(tk)

    ng_ref[...] = jax.nn.sigmoid(rows['ng'][:NSA_GATE_W])

    for i, name in enumerate(('kc', 'vc')):
        nat = rows[name].T
        for g in range(NSA_KV):
            kcv_ref[i * NSA_KV + g] = nat[:, g * HEAD_DIM:(g + 1) * HEAD_DIM]

    gab_ref[...] = jax.nn.sigmoid(_dot(ub, wn_ref[...]))


def _proj(h2, g, wt, wn, pos_row, freq_col):
    n, d = h2.shape
    tm = ROW_TILE
    rows = lambda w: pl.BlockSpec((tm, w), lambda i: (i, 0))
    cols = lambda w: pl.BlockSpec((w, tm), lambda i: (0, i))
    grouped = lambda k: pl.BlockSpec((k, tm, HEAD_DIM), lambda i: (0, i, 0))
    tiles = lambda groups, tk: pl.BlockSpec((tm // tk, groups * VAL_ROWS, tk), lambda i: (i, 0, 0))
    val_shape = lambda groups, tk: jax.ShapeDtypeStruct((n // tk, groups * VAL_ROWS, tk), BF16)
    out_shape = [
        jax.ShapeDtypeStruct((NSA_Q_W, n), BF16), jax.ShapeDtypeStruct((SWA_Q_W, n), BF16),
        jax.ShapeDtypeStruct((NSA_KV, n, HEAD_DIM), BF16), jax.ShapeDtypeStruct((NSA_KV, n, HEAD_DIM), BF16),
        jax.ShapeDtypeStruct((n, HEAD_DIM), BF16),
        val_shape(NSA_KV, NSA_TILE), val_shape(NSA_KV, NSA_TILE), val_shape(1, SWA_VTILE),
        jax.ShapeDtypeStruct((NSA_GATE_W, n), F32), jax.ShapeDtypeStruct((2 * NSA_KV, n, HEAD_DIM), F32),
        jax.ShapeDtypeStruct((n, 2 * d), F32),
    ]
    return pl.pallas_call(
        _proj_kernel,
        grid=(n // tm,),
        in_specs=[rows(d), _resident((1, d)), _resident(wt.shape), _resident(wn.shape),
                  cols(1), _resident(freq_col.shape)],
        out_specs=[cols(NSA_Q_W), cols(SWA_Q_W), grouped(NSA_KV), grouped(NSA_KV), rows(HEAD_DIM),
                   tiles(NSA_KV, NSA_TILE), tiles(NSA_KV, NSA_TILE), tiles(1, SWA_VTILE),
                   cols(NSA_GATE_W), grouped(2 * NSA_KV), rows(2 * d)],
        out_shape=out_shape,
        compiler_params=_params(1),
        name="proj",
    )(h2, g, wt, wn, pos_row, freq_col)


def _gelu(x):
    return jax.nn.gelu(x, approximate=True)


def _compress_kernel(kc_ref, vc_ref, pek_ref, pev_ref, w1k_ref, w1v_ref, w2kt_ref, w2vt_ref,
                     pos_ref, freq_ref, kc_out, vct_out):
    nc = kc_out.shape[1]

    def hidden(x_ref, pe_ref, w1_ref):
        top = bot = None
        for j in range(CMP_STRIDE):
            x = x_ref[0, pl.ds(j, nc, stride=CMP_STRIDE), :]
            t = _dot((x + pe_ref[j:j + 1, :]).astype(BF16), w1_ref[j])
            b = _dot((x + pe_ref[CMP_STRIDE + j:CMP_STRIDE + j + 1, :]).astype(BF16), w1_ref[CMP_STRIDE + j])
            top, bot = (t, b) if top is None else (top + t, bot + b)
        return _gelu(top + pltpu.roll(bot, shift=nc - 1, axis=0)).astype(BF16)

    kt = _dot_nt(w2kt_ref[...], hidden(kc_ref, pek_ref, w1k_ref))
    cos_t, sin_t = _rope_angles(pos_ref[0], freq_ref[...])
    kt = jnp.concatenate(_rope_rows(kt, cos_t, sin_t) + [jnp.zeros((LANES - HEAD_DIM, nc), F32)], axis=0)
    kc_out[0] = kt.T[:, :HEAD_DIM].astype(BF16)
    vct_out[0] = _dot_nt(w2vt_ref[...], hidden(vc_ref, pev_ref, w1v_ref)).astype(BF16)


def _compress(kcv, pek, pev, w1k, w1v, w2kt, w2vt, pos_c, freq_col, b, g_kv, s_len):
    nc = s_len // CMP_STRIDE
    hid = w1k.shape[-1]
    kcv = kcv.reshape(2 * g_kv * b, s_len, HEAD_DIM)
    src = lambda kind: pl.BlockSpec((1, s_len, HEAD_DIM),
                                    lambda i: ((kind * g_kv + i % g_kv) * b + i // g_kv, 0, 0))
    per = lambda *s: pl.BlockSpec((1,) + s, lambda i: (i,) + (0,) * len(s))
    return pl.pallas_call(
        _compress_kernel,
        grid=(b * g_kv,),
        in_specs=[src(0), src(1), _resident(pek.shape), _resident(pev.shape),
                  _resident(w1k.shape), _resident(w1v.shape), _resident((HEAD_DIM, hid)),
                  _resident((HEAD_DIM, hid)),
                  pl.BlockSpec((1, 1, nc), lambda i: (i // g_kv, 0, 0)), _resident(freq_col.shape)],
        out_specs=[per(nc, HEAD_DIM), per(HEAD_DIM, nc)],
        out_shape=[jax.ShapeDtypeStruct((b * g_kv, nc, HEAD_DIM), BF16),
                   jax.ShapeDtypeStruct((b * g_kv, HEAD_DIM, nc), BF16)],
        compiler_params=_params(1),
        name="compress",
    )(kcv, kcv, pek, pev, w1k, w1v, w2kt, w2vt, pos_c, freq_col)


def _cmp_topk_body(q_ref, kc_ref, vct_ref, ov_ref, gate_ref, mb_ref, oc_ref,
                   s_buf, p_buf, bias_buf, psum_buf, x_buf, *, n_top, nc, n_sel):
    tq = q_ref.shape[1]
    sub = s_buf.shape[1]
    rep = NSA_REP
    qi = pl.program_id(2)
    q0 = qi * tq
    stages = [(u, r) for u in range(tq // sub) for r in range(rep)]
    q_cols = lambda u, r: q_ref[r * HEAD_DIM:(r + 1) * HEAD_DIM, u * sub:(u + 1) * sub]
    keys = kc_ref[0, 0:nc, :]
    vals = vct_ref[0, :, 0:nc]

    s_buf[0:nc] = _dot(keys, q_cols(*stages[0]))
    inv_prev = None
    for n, (u, r) in enumerate(stages):
        cols = slice(u * sub, (u + 1) * sub)
        if n + 1 < len(stages):
            s_next = _dot(keys, q_cols(*stages[n + 1]))
        if n > 0:
            pu, pr = stages[n - 1]
            oc_ref[pr * HEAD_DIM:(pr + 1) * HEAD_DIM, pu * sub:(pu + 1) * sub] = (
                _dot(vals, p_buf[0:nc]) * (inv_prev * gate_ref[0, 0, pr:pr + 1, pu * sub:(pu + 1) * sub]))
        last = ((q0 + u * sub + lax.broadcasted_iota(jnp.int32, (1, sub), 1)) - (CMP_LEN - 1)) >> CMP_SHIFT
        if r == 0:
            bias_buf[0:nc] = jnp.where(lax.broadcasted_iota(jnp.int32, (nc, sub), 0) <= last, 0.0, NEG_INF)
        s = s_buf[0:nc] + bias_buf[0:nc]
        m = jnp.max(s, axis=0, keepdims=True)
        e = jnp.exp2(s - m)
        inv_prev = jnp.where(last >= 0, 1.0 / jnp.sum(e, axis=0, keepdims=True), 0.0)
        p_buf[0:nc] = e.astype(BF16)
        if r == 0:
            psum_buf[0:nc, cols] = e * inv_prev
        else:
            psum_buf[0:nc, cols] += e * inv_prev
        if n + 1 < len(stages):
            s_buf[0:nc] = s_next
    pu, pr = stages[-1]
    oc_ref[pr * HEAD_DIM:(pr + 1) * HEAD_DIM, pu * sub:(pu + 1) * sub] = (
        _dot(vals, p_buf[0:nc]) * (inv_prev * gate_ref[0, 0, pr:pr + 1, pu * sub:(pu + 1) * sub]))

    psum = psum_buf[0:nc, :]
    p_hi = psum.astype(BF16)
    p_lo = (psum - p_hi.astype(F32)).astype(BF16)
    pool = ov_ref[0:n_sel, 0:nc]
    imp = _dot(pool, p_hi) + _dot(pool, p_lo)

    blk = lax.broadcasted_iota(jnp.int32, (n_sel, tq), 0)
    tb = (q0 + lax.broadcasted_iota(jnp.int32, (n_sel, tq), 1)) >> SEL_SHIFT
    forced = (blk == 0) | ((tb - blk >= 0) & (tb - blk < SEL_LOCAL))
    n_forced = 1 + SEL_LOCAL
    premark = forced & (jnp.full((n_sel, tq), qi, jnp.int32) > 0)
    x_buf[0:n_sel] = jnp.where(premark, -jnp.inf, jnp.where(forced, FORCE, jnp.where(blk > tb, -FORCE, imp)))

    blk_f = blk.astype(F32)

    def pick(_, carry):
        x = x_buf[0:n_sel]
        top = jnp.max(x, axis=0, keepdims=True)
        first = jnp.min(jnp.where(x == top, blk_f, float(n_sel)), axis=0, keepdims=True)
        x_buf[0:n_sel] = jnp.where(blk_f == first, -jnp.inf, x)
        return carry

    lax.fori_loop(0, jnp.where(qi > 0, n_top - n_forced, n_top), pick, 0)
    mb_ref[0, 0:n_sel, :] = jnp.where(x_buf[0:n_sel] == -jnp.inf, 0.0, NEG_INF).astype(BF16)
    if n_sel < mb_ref.shape[1]:
        mb_ref[0, n_sel:, :] = jnp.full((mb_ref.shape[1] - n_sel, tq), NEG_INF, BF16)


def _cmp_topk_kernel(*refs, n_top, n_tiles):
    nc = refs[1].shape[1]
    n_sel = refs[3].shape[0]
    qi = pl.program_id(2)
    for v in range(max(n_tiles // 2, 1)):
        frac = lambda total: min(total, (2 * v + 2) * total // n_tiles)
        pl.when((qi >> 1) == v)(functools.partial(_cmp_topk_body, *refs, n_top=n_top, nc=frac(nc), n_sel=frac(n_sel)))


def _cmp_topk(qn_t, kc, vct, ov, gates, b, g_kv, s_len, n_top):
    tq = min(CMP_TILE, s_len)
    sub = min(CMP_SUBTILE, tq)
    nq = s_len // tq
    nc = kc.shape[1]
    n_sel = ov.shape[0]
    rep = NSA_REP
    return pl.pallas_call(
        functools.partial(_cmp_topk_kernel, n_top=n_top, n_tiles=nq),
        grid=(b, g_kv, nq),
        in_specs=[
            pl.BlockSpec((rep * HEAD_DIM, tq), lambda bi, gi, qi: (gi, bi * nq + qi)),
            pl.BlockSpec((1, nc, HEAD_DIM), lambda bi, gi, qi: (bi * g_kv + gi, 0, 0)),
            pl.BlockSpec((1, HEAD_DIM, nc), lambda bi, gi, qi: (bi * g_kv + gi, 0, 0)),
            _resident(ov.shape),
            pl.BlockSpec((1, 1, rep, tq), lambda bi, gi, qi: (0, gi, 0, bi * nq + qi)),
        ],
        out_specs=[
            pl.BlockSpec((1, n_sel, tq), lambda bi, gi, qi: (bi * g_kv + gi, 0, qi)),
            pl.BlockSpec((rep * HEAD_DIM, tq), lambda bi, gi, qi: (gi, bi * nq + qi)),
        ],
        out_shape=[jax.ShapeDtypeStruct((b * g_kv, n_sel, s_len), BF16),
                   jax.ShapeDtypeStruct((NSA_Q_W, b * s_len), F32)],
        scratch_shapes=[pltpu.VMEM((nc, sub), F32), pltpu.VMEM((nc, sub), BF16), pltpu.VMEM((nc, sub), F32),
                        pltpu.VMEM((nc, tq), F32), pltpu.VMEM((n_sel, tq), F32)],
        compiler_params=_params(3),
        name="cmp_topk",
    )(qn_t, kc, vct, ov, gates)


def _softmax_tile(m, s):
    m_new = jnp.maximum(m, jnp.max(s, axis=0, keepdims=True))
    return m_new, jnp.exp2(m - m_new), jnp.exp2(s - m_new).astype(BF16)


def _normalize(acc):
    return acc[:HEAD_DIM] / acc[HEAD_DIM:HEAD_DIM + 1]


def _rel_pos(tk, tq):
    return lax.broadcasted_iota(jnp.int32, (tk, tq), 1) - lax.broadcasted_iota(jnp.int32, (tk, tq), 0)


def _window_bias(rel, back, tk, kj, window):
    dist = rel + (back * tk + jnp.where(kj >= 0, 0, window))
    return jnp.where((dist >= 0) & (dist < window), 0.0, NEG_INF)


def _pipe_stage(s_buf, p_buf, heads, cur=None, nxt=None, prv=None):
    if nxt is not None:
        s_next = [_dot(nxt[0], nxt[1][r][...]) for r in heads]
    if prv is not None:
        pv = [_dot(prv[0], p_buf[r]) for r in heads]
    out = None
    if cur is not None:
        m, bias = cur
        out = []
        for r in heads:
            s = s_buf[r] if bias is None else s_buf[r] + bias
            m_r, a_r, p_buf[r] = _softmax_tile(m[r], s)
            out.append((m_r, a_r))
        out = tuple(zip(*out))
    for r in heads:
        if prv is not None:
            prv[1][r] = prv[2][r] * prv[1][r] + pv[r]
        if nxt is not None:
            s_buf[r] = s_next[r]
    return out


def _nsa_kernel(q_ref, mb_ref, oc_ref, gate_ref, ks_ref, ex_ref, vs_ref, kw_ref, vw_ref, o_ref, *scratch):
    n_sub = q_ref.shape[1] // NSA_TILE
    per_tile = len(scratch) // n_sub
    for sub in range(n_sub):
        at = pl.ds(sub * NSA_TILE, NSA_TILE)
        _nsa_tile(pl.program_id(2) * n_sub + sub, q_ref.at[:, at], mb_ref.at[:, :, at], oc_ref.at[:, at],
                  gate_ref.at[:, :, :, at], ks_ref, ex_ref, vs_ref, kw_ref, vw_ref, o_ref.at[at, :],
                  *scratch[sub * per_tile:(sub + 1) * per_tile])


def _nsa_tile(qi, q_ref, mb_ref, oc_ref, gate_ref, ks_ref, ex_ref, vs_ref, kw_ref, vw_ref, o_ref,
              qa_buf, s_buf, p_buf, sel_acc, win_acc):
    tq = q_ref.shape[1]
    tk = tq
    n_sel = mb_ref.shape[1]
    rep = NSA_REP
    heads = range(rep)
    stage = functools.partial(_pipe_stage, s_buf, p_buf, heads)
    key_rows = lambda j: pl.ds(pl.multiple_of(j * tk, tk), tk)

    def sel_keys(j):
        return jnp.concatenate([ex_ref[key_rows(j), :], ks_ref[0, key_rows(j), :]], axis=1)

    q_heads = [q_ref.at[pl.ds(r * HEAD_DIM, HEAD_DIM), :] for r in heads]
    q_aug = [qa_buf.at[r] for r in heads]
    for r in heads:
        qa_buf[r, 0:n_sel, :] = mb_ref[0]
        qa_buf[r, n_sel:n_sel + HEAD_DIM, :] = q_heads[r][...]
        s_buf[r] = _dot(sel_keys(0), qa_buf[r])
        p_buf[r] = jnp.zeros((tk, tq), BF16)
        sel_acc[r] = jnp.zeros((VAL_ROWS, tq), F32)
        win_acc[r] = jnp.zeros((VAL_ROWS, tq), F32)
    row = lambda v: tuple(jnp.full((1, tq), v, F32) for _ in heads)

    def sel_stage(i, carry):
        m, alpha = carry
        return stage(cur=(m, None), nxt=(sel_keys(i + 1), q_aug),
                     prv=(vs_ref[jnp.maximum(i - 1, 0)], sel_acc, alpha))

    carry = lax.fori_loop(0, qi >> 1, lambda j, c: sel_stage(2 * j + 1, sel_stage(2 * j, c)),
                          (row(NEG_INF), row(1.0)))
    m, alpha = lax.fori_loop(0, qi & 1, lambda _, c: sel_stage(qi - 1, c), carry)

    rel = _rel_pos(tk, tq)
    backs = list(range(NSA_WINDOW // tk, -1, -1))
    win_tile = [jnp.maximum(qi - back, 0) for back in backs]
    win_bias = [_window_bias(rel, back, tk, qi - back, NSA_WINDOW) for back in backs]
    _, alpha_d = stage(cur=(m, jnp.where(rel >= 0, 0.0, NEG_INF)),
                       nxt=(kw_ref[0, key_rows(win_tile[0]), :], q_heads),
                       prv=(vs_ref[jnp.maximum(qi - 1, 0)], sel_acc, alpha))
    prv = (vs_ref[qi], sel_acc, alpha_d)
    m_w = row(NEG_INF)
    for n in range(len(backs)):
        nxt = (kw_ref[0, key_rows(win_tile[n + 1]), :], q_heads) if n + 1 < len(backs) else None
        m_w, alpha_w = stage(cur=(m_w, win_bias[n]), nxt=nxt, prv=prv)
        prv = (vw_ref[win_tile[n]], win_acc, alpha_w)
    stage(prv=prv)

    outs = [oc_ref[r * HEAD_DIM:(r + 1) * HEAD_DIM, :]
            + gate_ref[1, 0, r:r + 1, :] * _normalize(sel_acc[r])
            + gate_ref[2, 0, r:r + 1, :] * _normalize(win_acc[r]) for r in heads]
    o_ref[...] = jnp.concatenate(outs, axis=0).T.astype(o_ref.dtype)


def _nsa(qn_t, mb, oc_t, gates, ksel, expand, vsel, kwin, vwin, b, g_kv, s_len):
    tk = NSA_TILE
    tq = min(NSA_STEP, s_len)
    nq = s_len // tq
    rep = NSA_REP
    n_sel = mb.shape[1]
    keys = pl.BlockSpec((1, s_len, HEAD_DIM), lambda bi, gi, qi: (gi * b + bi, 0, 0))
    vals = pl.BlockSpec((s_len // tk, VAL_ROWS, tk), lambda bi, gi, qi: (bi, gi, 0))
    return pl.pallas_call(
        _nsa_kernel,
        grid=(b, g_kv, nq),
        in_specs=[
            pl.BlockSpec((rep * HEAD_DIM, tq), lambda bi, gi, qi: (gi, bi * nq + qi)),
            pl.BlockSpec((1, n_sel, tq), lambda bi, gi, qi: (bi * g_kv + gi, 0, qi)),
            pl.BlockSpec((rep * HEAD_DIM, tq), lambda bi, gi, qi: (gi, bi * nq + qi)),
            pl.BlockSpec((3, 1, rep, tq), lambda bi, gi, qi: (0, gi, 0, bi * nq + qi)),
            keys, _resident(expand.shape), vals, keys, vals,
        ],
        out_specs=pl.BlockSpec((tq, rep * HEAD_DIM), lambda bi, gi, qi: (bi * nq + qi, gi)),
        out_shape=jax.ShapeDtypeStruct((b * s_len, NSA_Q_W), BF16),
        scratch_shapes=[pltpu.VMEM((rep, n_sel + HEAD_DIM, tk), BF16), pltpu.VMEM((rep, tk, tk), F32),
                        pltpu.VMEM((rep, tk, tk), BF16), pltpu.VMEM((rep, VAL_ROWS, tk), F32),
                        pltpu.VMEM((rep, VAL_ROWS, tk), F32)] * (tq // tk),
        compiler_params=_params(3),
        name="nsa",
    )(qn_t, mb, oc_t, gates, ksel.reshape(g_kv * b, s_len, HEAD_DIM), expand, vsel,
      kwin.reshape(g_kv * b, s_len, HEAD_DIM), vwin)


def _swa_kernel(q_ref, sink_ref, k_ref, v_ref, o_ref, *scratch):
    n_sub = q_ref.shape[1] // SWA_TILE
    per_tile = len(scratch) // n_sub
    for sub in range(n_sub):
        at = pl.ds(sub * SWA_TILE, SWA_TILE)
        _swa_tile(pl.program_id(1) * n_sub + sub, q_ref.at[:, at], sink_ref, k_ref, v_ref, o_ref.at[at, :],
                  *scratch[sub * per_tile:(sub + 1) * per_tile])


def _swa_tile(qi, q_ref, sink_ref, k_ref, v_ref, o_ref, s_buf, p_buf, bias_buf, o_buf):
    tq = q_ref.shape[1]
    tv = v_ref.shape[2]
    nk = s_buf.shape[0]
    heads = SWA_HEADS
    q0 = qi * tq
    k0 = jnp.maximum(q0 - SWA_WINDOW, 0)

    keys = k_ref[0, pl.ds(pl.multiple_of(k0, tv), nk), :]
    vals = jnp.concatenate([v_ref[k0 // tv + j] for j in range(nk // tv)], axis=1)
    dist = _rel_pos(nk, tq) + (q0 - k0)
    bias_buf[...] = jnp.where((dist >= 0) & (dist < SWA_WINDOW), 0.0, NEG_INF)
    q_head = lambda h: q_ref[h * HEAD_DIM:(h + 1) * HEAD_DIM, :]

    def finish(h, m_all, sink):
        acc = _dot(vals, p_buf[...])
        o_buf[h * HEAD_DIM:(h + 1) * HEAD_DIM, :] = (
            acc[:HEAD_DIM] / (acc[HEAD_DIM:HEAD_DIM + 1] + jnp.exp2(sink - m_all)))

    s_buf[...] = _dot(keys, q_head(0))
    prev = None
    for h in range(heads):
        if h + 1 < heads:
            s_next = _dot(keys, q_head(h + 1))
        if prev is not None:
            finish(*prev)
        sink = sink_ref[:, h * tq:(h + 1) * tq]
        s = s_buf[...] + bias_buf[...]
        m_all = jnp.maximum(jnp.max(s, axis=0, keepdims=True), sink)
        p_buf[...] = jnp.exp2(s - m_all).astype(BF16)
        prev = (h, m_all, sink)
        if h + 1 < heads:
            s_buf[...] = s_next
    finish(*prev)
    o_ref[...] = o_buf[...].T.astype(o_ref.dtype)


def _swa(sq_t, sink_row, k, v, b, s_len):
    tile = min(SWA_TILE, s_len)
    tq = min(SWA_STEP, s_len)
    nq = s_len // tq
    tv = v.shape[2]
    nk = tile + SWA_WINDOW
    return pl.pallas_call(
        _swa_kernel,
        grid=(b, nq),
        in_specs=[pl.BlockSpec((SWA_Q_W, tq), lambda bi, qi: (0, bi * nq + qi)),
                  _resident(sink_row.shape),
                  pl.BlockSpec((1, s_len, HEAD_DIM), lambda bi, qi: (bi, 0, 0)),
                  pl.BlockSpec((s_len // tv, VAL_ROWS, tv), lambda bi, qi: (bi, 0, 0))],
        out_specs=pl.BlockSpec((tq, SWA_Q_W), lambda bi, qi: (bi * nq + qi, 0)),
        out_shape=jax.ShapeDtypeStruct((b * s_len, SWA_Q_W), BF16),
        scratch_shapes=[pltpu.VMEM((nk, tile), F32), pltpu.VMEM((nk, tile), BF16), pltpu.VMEM((nk, tile), F32),
                        pltpu.VMEM((SWA_Q_W, tile), F32)] * (tq // tile),
        compiler_params=_params(2),
        name="swa",
    )(sq_t, sink_row, k.reshape(b, s_len, HEAD_DIM), v)


def _overlap_matrix(n_sel, n_cmp, n_cmp_pad):
    cs = np.arange(n_cmp) * CMP_STRIDE
    ss = np.arange(n_sel) * SEL_LEN
    ov = np.clip(np.minimum(cs[None, :] + CMP_LEN, ss[:, None] + SEL_LEN)
                 - np.maximum(cs[None, :], ss[:, None]), 0, None).astype(np.float32) / CMP_LEN
    return np.pad(ov, ((0, 0), (0, n_cmp_pad - n_cmp)))


def _layer(h, positions, w, b, s_len, norm_final, last):
    n, d = h.shape
    g_kv = NSA_KV
    n_sel = s_len // SEL_LEN
    n_top = min(SEL_TOPN, n_sel)
    nc = s_len // CMP_STRIDE
    bf = lambda a: a.astype(BF16)

    h = _ffn(h, w['norm_ffn1'][None], bf(w['ffn1_gate']), bf(w['ffn1_up']), bf(w['ffn1_down']),
             w['norm_ffn1'][None], False)

    pts = np.cumsum((NSA_Q_W,) + (NSA_KV_W,) * 6 + (NSA_GATE_W, SWA_Q_W, SWA_KV_W, SWA_KV_W, d, d))[:-1]
    (w_nq, w_kc, w_vc, w_ksl, w_vsl, w_kwn, w_vwn, w_ng, w_sq, w_sk, w_sv, w_ga, w_gb) = jnp.split(
        bf(w['w_in']), pts, axis=1)
    parts = dict(qn=w_nq, sq=w_sq, ksel=w_ksl, kwin=w_kwn, kswa=w_sk, vsel=w_vsl, vwin=w_vwn, vswa=w_sv,
                 ng=w_ng, kc=w_kc, vc=w_vc)
    w_t = jnp.concatenate([jnp.pad(parts[name], ((0, 0), (0, width - parts[name].shape[1])))
                           for name, width in PROJ_ROWS], axis=1).T
    w_n = jnp.concatenate([w_ga, w_gb], axis=1)

    freq_col = (ROPE_THETA ** (-jnp.arange(HALF_DIM, dtype=F32) / HALF_DIM))[:, None]
    (qn_t, sq_t, ksel, kwin, kswa, vsel, vwin, vswa, ng_t, kcv, gab) = _proj(
        h, w['norm_mix'][None], w_t, w_n, positions.reshape(1, n), freq_col)

    pos_c = jnp.pad(positions[:, CMP_LEN - 1::CMP_STRIDE], ((0, 0), (0, 1)))[:, None, :]
    w1 = lambda a: bf(a.reshape(CMP_LEN, HEAD_DIM, a.shape[-1]))
    kc, vc_t = _compress(kcv, w['cmp_pe_k'], w['cmp_pe_v'], w1(w['cmp_k_w1']), w1(w['cmp_v_w1']),
                         bf(w['cmp_k_w2'].T), bf(w['cmp_v_w2'].T), pos_c, freq_col, b, g_kv, s_len)

    gates = ng_t.reshape(3, g_kv, NSA_REP, n)
    ov = jnp.asarray(_overlap_matrix(n_sel, nc - 1, nc), BF16)
    mb, oc_t = _cmp_topk(qn_t, kc, vc_t, ov, gates, b, g_kv, s_len, n_top)

    expand = jnp.asarray(np.arange(s_len)[:, None] // SEL_LEN == np.arange(n_sel)[None, :], BF16)
    o_a = _nsa(qn_t, mb, oc_t, gates, ksel, expand, vsel, kwin, vwin, b, g_kv, s_len)

    sink_row = jnp.repeat(w['swa_sinks'].astype(F32) * LOG2E, min(SWA_TILE, s_len))[None]
    o_b = _swa(sq_t, sink_row, kswa, vswa, b, s_len)

    branches = (o_a, o_b, gab, bf(w['w_branch_a']), bf(w['w_branch_b']), bf(w['w_out']))
    return _ffn(h, w['norm_ffn2'][None], bf(w['ffn2_gate']), bf(w['ffn2_up']), bf(w['ffn2_down']),
                norm_final[None], last, branches)


def kernel(x, positions, norm_ffn1, ffn1_gate, ffn1_up, ffn1_down, norm_mix, w_in, cmp_pe_k, cmp_k_w1, cmp_k_w2, cmp_pe_v, cmp_v_w1, cmp_v_w2, swa_sinks, w_branch_a, w_branch_b, w_out, norm_ffn2, ffn2_gate, ffn2_up, ffn2_down, norm_final):
    b, s_len, d = x.shape
    stacked = dict(norm_ffn1=norm_ffn1, ffn1_gate=ffn1_gate, ffn1_up=ffn1_up, ffn1_down=ffn1_down,
                   norm_mix=norm_mix, w_in=w_in, cmp_pe_k=cmp_pe_k, cmp_k_w1=cmp_k_w1, cmp_k_w2=cmp_k_w2,
                   cmp_pe_v=cmp_pe_v, cmp_v_w1=cmp_v_w1, cmp_v_w2=cmp_v_w2, swa_sinks=swa_sinks,
                   w_branch_a=w_branch_a, w_branch_b=w_branch_b, w_out=w_out,
                   norm_ffn2=norm_ffn2, ffn2_gate=ffn2_gate, ffn2_up=ffn2_up, ffn2_down=ffn2_down)
    depth = norm_ffn1.shape[0]
    h = x.reshape(b * s_len, d)
    for i in range(depth):
        w = {k: v[i] for k, v in stacked.items()}
        h = _layer(h, positions, w, b, s_len, norm_final, i == depth - 1)
    return h.reshape(b, s_len, d)
```

```python
import functools

import numpy as np
import jax
import jax.numpy as jnp
from jax import lax
from jax.experimental import pallas as pl
from jax.experimental.pallas import tpu as pltpu

HEAD_DIM = 64
HALF_DIM = HEAD_DIM // 2
NSA_HEADS = 8
NSA_KV = 2
NSA_REP = NSA_HEADS // NSA_KV
SWA_HEADS = 8
CMP_STRIDE = 16
CMP_SHIFT = 4
CMP_LEN = 2 * CMP_STRIDE
SEL_LEN = 64
SEL_SHIFT = 6
SEL_TOPN = 16
SEL_LOCAL = 2
NSA_WINDOW = 512
SWA_WINDOW = 128
ROPE_THETA = 10000.0
RMS_EPS = 1e-6
FFN_HALF = 0.5
NEG_INF = -1e30
FORCE = 1e9
LOG2E = 1.4426950408889634
Q_SCALE = HEAD_DIM ** -0.5 * LOG2E
BF16_ROWS = 16
VAL_ROWS = HEAD_DIM + BF16_ROWS
LANES = 128

NSA_Q_W = NSA_HEADS * HEAD_DIM
NSA_KV_W = NSA_KV * HEAD_DIM
NSA_GATE_W = 3 * NSA_HEADS
SWA_Q_W = SWA_HEADS * HEAD_DIM
SWA_KV_W = HEAD_DIM

VMEM_LIMIT_BYTES = 56 * 1024 * 1024

BF16 = jnp.bfloat16
F32 = jnp.float32

ROW_TILE = 512
CMP_TILE = 1024
CMP_SUBTILE = 256
NSA_TILE = 256
NSA_STEP = 2048
SWA_TILE = 256
SWA_STEP = 1024
SWA_VTILE = 128


def _params(n_axes):
    return pltpu.CompilerParams(dimension_semantics=("arbitrary",) * n_axes,
                                vmem_limit_bytes=VMEM_LIMIT_BYTES)


def _resident(shape):
    zeros = (0,) * len(shape)
    return pl.BlockSpec(shape, lambda *_: zeros, pipeline_mode=pl.Buffered(1))


def _rms(x, g):
    y = x * lax.rsqrt(jnp.mean(x * x, axis=-1, keepdims=True) + RMS_EPS)
    return y * g


def _dot(a, b):
    return jnp.dot(a, b, preferred_element_type=F32)


def _dot_nt(a, b):
    return lax.dot_general(a, b, (((1,), (1,)), ((), ())), preferred_element_type=F32)


def _rope_angles(pos_row, freq_col):
    ang = pos_row.astype(F32) * freq_col
    return jnp.cos(ang), jnp.sin(ang)


def _rope_rows(block, cos_t, sin_t):
    out = []
    for hd in range(block.shape[0] // HEAD_DIM):
        x1 = block[hd * HEAD_DIM:hd * HEAD_DIM + HALF_DIM]
        x2 = block[hd * HEAD_DIM + HALF_DIM:(hd + 1) * HEAD_DIM]
        out += [x1 * cos_t - x2 * sin_t, x2 * cos_t + x1 * sin_t]
    return out


def _ones_rows(tk):
    return jnp.where(lax.broadcasted_iota(jnp.int32, (BF16_ROWS, tk), 0) == 0, 1.0, 0.0).astype(BF16)


def _merge_branches(h_ref, oa_ref, ob_ref, gab_ref, wa_ref, wb_ref, wo_ref):
    d = h_ref.shape[1]
    gab = gab_ref[...]
    merged = gab[:, :d] * _dot(oa_ref[...], wa_ref[...]) + gab[:, d:] * _dot(ob_ref[...], wb_ref[...])
    return h_ref[...] + _dot(merged.astype(BF16), wo_ref[...])


def _ffn_kernel(*refs, final_norm, merge):
    x_ref, g_ref, wg_ref, wu_ref, wd_ref, gf_ref, o_ref = refs[-7:]
    x = _merge_branches(x_ref, *refs[:-7]) if merge else x_ref[...]
    xb = _rms(x, g_ref[...]).astype(BF16)
    a = _dot(xb, wg_ref[...])
    b = _dot(xb, wu_ref[...])
    t = (a * jax.nn.sigmoid(a)) * b
    h = x + FFN_HALF * _dot(t.astype(BF16), wd_ref[...])
    if final_norm:
        h = _rms(h, gf_ref[...])
    o_ref[...] = h


def _ffn(x2, g, wg, wu, wd, gf, final_norm, branches=None):
    n, d = x2.shape
    f = wg.shape[1]
    tm = ROW_TILE
    rows = lambda w: pl.BlockSpec((tm, w), lambda i: (i, 0))
    merge_specs, merge_args = [], ()
    if branches is not None:
        oa, ob, gab, wa, wb, wo = branches
        merge_specs = [rows(oa.shape[1]), rows(ob.shape[1]), rows(2 * d),
                       _resident(wa.shape), _resident(wb.shape), _resident(wo.shape)]
        merge_args = branches
    return pl.pallas_call(
        functools.partial(_ffn_kernel, final_norm=final_norm, merge=branches is not None),
        grid=(n // tm,),
        in_specs=merge_specs + [rows(d), _resident((1, d)), _resident((d, f)), _resident((d, f)),
                                _resident((f, d)), _resident((1, d))],
        out_specs=rows(d),
        out_shape=jax.ShapeDtypeStruct((n, d), F32),
        compiler_params=_params(1),
        name="ffn_final" if final_norm else "ffn",
    )(*merge_args, x2, g, wg, wu, wd, gf)


PROJ_ROWS = (('qn', NSA_Q_W), ('sq', SWA_Q_W), ('ksel', NSA_KV_W), ('kwin', NSA_KV_W), ('kswa', LANES),
             ('vsel', NSA_KV_W), ('vwin', NSA_KV_W), ('vswa', SWA_KV_W), ('ng', 32),
             ('kc', NSA_KV_W), ('vc', NSA_KV_W))


def _proj_kernel(h_ref, g_ref, wt_ref, wn_ref, pos_ref, freq_ref,
                 qn_ref, sq_ref, ksel_ref, kwin_ref, kswa_ref, vsel_ref, vwin_ref, vswa_ref,
                 ng_ref, kcv_ref, gab_ref):
    ub = _rms(h_ref[...], g_ref[...]).astype(BF16)
    yt = _dot_nt(wt_ref[...], ub)
    rows, o = {}, 0
    for name, width in PROJ_ROWS:
        rows[name] = yt[o:o + width]
        o += width
    cos_t, sin_t = _rope_angles(pos_ref[...], freq_ref[...])

    for name, ref in (('qn', qn_ref), ('sq', sq_ref)):
        for i, piece in enumerate(_rope_rows(rows[name], cos_t, sin_t)):
            ref[i * HALF_DIM:(i + 1) * HALF_DIM, :] = (piece * Q_SCALE).astype(BF16)

    for name, ref in (('ksel', ksel_ref), ('kwin', kwin_ref)):
        k_nat = jnp.concatenate(_rope_rows(rows[name], cos_t, sin_t), axis=0).T
        for g in range(NSA_KV):
            ref[g] = k_nat[:, g * HEAD_DIM:(g + 1) * HEAD_DIM].astype(BF16)
    kswa = _rope_rows(rows['kswa'][:HEAD_DIM], cos_t, sin_t) + [rows['kswa'][HEAD_DIM:]]
    kswa_ref[...] = jnp.concatenate(kswa, axis=0).T[:, :HEAD_DIM].astype(BF16)

    for name, ref, groups in (('vsel', vsel_ref, NSA_KV), ('vwin', vwin_ref, NSA_KV), ('vswa', vswa_ref, 1)):
        tk = ref.shape[2]
        for j in range(ref.shape[0]):
            for g in range(groups):
                ref[j, g * VAL_ROWS:g * VAL_ROWS + HEAD_DIM, :] = (
                    rows[name][g * HEAD_DIM:(g + 1) * HEAD_DIM, j * tk:(j + 1) * tk].astype(BF16))
                ref[j, g * VAL_ROWS + HEAD_DIM:(g + 1) * VAL_ROWS, :] = _ones_rows(tk)

    ng_ref[...] = jax.nn.sigmoid(rows['ng'][:NSA_GATE_W])

    for i, name in enumerate(('kc', 'vc')):
        nat = rows[name].T
        for g in range(NSA_KV):
            kcv_ref[i * NSA_KV + g] = nat[:, g * HEAD_DIM:(g + 1) * HEAD_DIM]

    gab_ref[...] = jax.nn.sigmoid(_dot(ub, wn_ref[...]))


def _proj(h2, g, wt, wn, pos_row, freq_col):
    n, d = h2.shape
    tm = ROW_TILE
    rows = lambda w: pl.BlockSpec((tm, w), lambda i: (i, 0))
    cols = lambda w: pl.BlockSpec((w, tm), lambda i: (0, i))
    grouped = lambda k: pl.BlockSpec((k, tm, HEAD_DIM), lambda i: (0, i, 0))
    tiles = lambda groups, tk: pl.BlockSpec((tm // tk, groups * VAL_ROWS, tk), lambda i: (i, 0, 0))
    val_shape = lambda groups, tk: jax.ShapeDtypeStruct((n // tk, groups * VAL_ROWS, tk), BF16)
    out_shape = [
        jax.ShapeDtypeStruct((NSA_Q_W, n), BF16), jax.ShapeDtypeStruct((SWA_Q_W, n), BF16),
        jax.ShapeDtypeStruct((NSA_KV, n, HEAD_DIM), BF16), jax.ShapeDtypeStruct((NSA_KV, n, HEAD_DIM), BF16),
        jax.ShapeDtypeStruct((n, HEAD_DIM), BF16),
        val_shape(NSA_KV, NSA_TILE), val_shape(NSA_KV, NSA_TILE), val_shape(1, SWA_VTILE),
        jax.ShapeDtypeStruct((NSA_GATE_W, n), F32), jax.ShapeDtypeStruct((2 * NSA_KV, n, HEAD_DIM), F32),
        jax.ShapeDtypeStruct((n, 2 * d), F32),
    ]
    return pl.pallas_call(
        _proj_kernel,
        grid=(n // tm,),
        in_specs=[rows(d), _resident((1, d)), _resident(wt.shape), _resident(wn.shape),
                  cols(1), _resident(freq_col.shape)],
        out_specs=[cols(NSA_Q_W), cols(SWA_Q_W), grouped(NSA_KV), grouped(NSA_KV), rows(HEAD_DIM),
                   tiles(NSA_KV, NSA_TILE), tiles(NSA_KV, NSA_TILE), tiles(1, SWA_VTILE),
                   cols(NSA_GATE_W), grouped(2 * NSA_KV), rows(2 * d)],
        out_shape=out_shape,
        compiler_params=_params(1),
        name="proj",
    )(h2, g, wt, wn, pos_row, freq_col)


def _gelu(x):
    return jax.nn.gelu(x, approximate=True)


def _compress_kernel(kc_ref, vc_ref, pek_ref, pev_ref, w1k_ref, w1v_ref, w2kt_ref, w2vt_ref,
                     pos_ref, freq_ref, kc_out, vct_out):
    nc = kc_out.shape[1]

    def hidden(x_ref, pe_ref, w1_ref):
        top = bot = None
        for j in range(CMP_STRIDE):
            x = x_ref[0, pl.ds(j, nc, stride=CMP_STRIDE), :]
            t = _dot((x + pe_ref[j:j + 1, :]).astype(BF16), w1_ref[j])
            b = _dot((x + pe_ref[CMP_STRIDE + j:CMP_STRIDE + j + 1, :]).astype(BF16), w1_ref[CMP_STRIDE + j])
            top, bot = (t, b) if top is None else (top + t, bot + b)
        return _gelu(top + pltpu.roll(bot, shift=nc - 1, axis=0)).astype(BF16)

    kt = _dot_nt(w2kt_ref[...], hidden(kc_ref, pek_ref, w1k_ref))
    cos_t, sin_t = _rope_angles(pos_ref[0], freq_ref[...])
    kt = jnp.concatenate(_rope_rows(kt, cos_t, sin_t) + [jnp.zeros((LANES - HEAD_DIM, nc), F32)], axis=0)
    kc_out[0] = kt.T[:, :HEAD_DIM].astype(BF16)
    vct_out[0] = _dot_nt(w2vt_ref[...], hidden(vc_ref, pev_ref, w1v_ref)).astype(BF16)


def _compress(kcv, pek, pev, w1k, w1v, w2kt, w2vt, pos_c, freq_col, b, g_kv, s_len):
    nc = s_len // CMP_STRIDE
    hid = w1k.shape[-1]
    kcv = kcv.reshape(2 * g_kv * b, s_len, HEAD_DIM)
    src = lambda kind: pl.BlockSpec((1, s_len, HEAD_DIM),
                                    lambda i: ((kind * g_kv + i % g_kv) * b + i // g_kv, 0, 0))
    per = lambda *s: pl.BlockSpec((1,) + s, lambda i: (i,) + (0,) * len(s))
    return pl.pallas_call(
        _compress_kernel,
        grid=(b * g_kv,),
        in_specs=[src(0), src(1), _resident(pek.shape), _resident(pev.shape),
                  _resident(w1k.shape), _resident(w1v.shape), _resident((HEAD_DIM, hid)),
                  _resident((HEAD_DIM, hid)),
                  pl.BlockSpec((1, 1, nc), lambda i: (i // g_kv, 0, 0)), _resident(freq_col.shape)],
        out_specs=[per(nc, HEAD_DIM), per(HEAD_DIM, nc)],
        out_shape=[jax.ShapeDtypeStruct((b * g_kv, nc, HEAD_DIM), BF16),
                   jax.ShapeDtypeStruct((b * g_kv, HEAD_DIM, nc), BF16)],
        compiler_params=_params(1),
        name="compress",
    )(kcv, kcv, pek, pev, w1k, w1v, w2kt, w2vt, pos_c, freq_col)


def _cmp_topk_body(q_ref, kc_ref, vct_ref, ov_ref, gate_ref, mb_ref, oc_ref,
                   s_buf, p_buf, bias_buf, psum_buf, x_buf, *, n_top, nc, n_sel):
    tq = q_ref.shape[1]
    sub = s_buf.shape[1]
    rep = NSA_REP
    qi = pl.program_id(2)
    q0 = qi * tq
    stages = [(u, r) for u in range(tq // sub) for r in range(rep)]
    q_cols = lambda u, r: q_ref[r * HEAD_DIM:(r + 1) * HEAD_DIM, u * sub:(u + 1) * sub]
    keys = kc_ref[0, 0:nc, :]
    vals = vct_ref[0, :, 0:nc]

    s_buf[0:nc] = _dot(keys, q_cols(*stages[0]))
    inv_prev = None
    for n, (u, r) in enumerate(stages):
        cols = slice(u * sub, (u + 1) * sub)
        if n + 1 < len(stages):
            s_next = _dot(keys, q_cols(*stages[n + 1]))
        if n > 0:
            pu, pr = stages[n - 1]
            oc_ref[pr * HEAD_DIM:(pr + 1) * HEAD_DIM, pu * sub:(pu + 1) * sub] = (
                _dot(vals, p_buf[0:nc]) * (inv_prev * gate_ref[0, 0, pr:pr + 1, pu * sub:(pu + 1) * sub]))
        last = ((q0 + u * sub + lax.broadcasted_iota(jnp.int32, (1, sub), 1)) - (CMP_LEN - 1)) >> CMP_SHIFT
        if r == 0:
            bias_buf[0:nc] = jnp.where(lax.broadcasted_iota(jnp.int32, (nc, sub), 0) <= last, 0.0, NEG_INF)
        s = s_buf[0:nc] + bias_buf[0:nc]
        m = jnp.max(s, axis=0, keepdims=True)
        e = jnp.exp2(s - m)
        inv_prev = jnp.where(last >= 0, 1.0 / jnp.sum(e, axis=0, keepdims=True), 0.0)
        p_buf[0:nc] = e.astype(BF16)
        if r == 0:
            psum_buf[0:nc, cols] = e * inv_prev
        else:
            psum_buf[0:nc, cols] += e * inv_prev
        if n + 1 < len(stages):
            s_buf[0:nc] = s_next
    pu, pr = stages[-1]
    oc_ref[pr * HEAD_DIM:(pr + 1) * HEAD_DIM, pu * sub:(pu + 1) * sub] = (
        _dot(vals, p_buf[0:nc]) * (inv_prev * gate_ref[0, 0, pr:pr + 1, pu * sub:(pu + 1) * sub]))

    psum = psum_buf[0:nc, :]
    p_hi = psum.astype(BF16)
    p_lo = (psum - p_hi.astype(F32)).astype(BF16)
    pool = ov_ref[0:n_sel, 0:nc]
    imp = _dot(pool, p_hi) + _dot(pool, p_lo)

    blk = lax.broadcasted_iota(jnp.int32, (n_sel, tq), 0)
    tb = (q0 + lax.broadcasted_iota(jnp.int32, (n_sel, tq), 1)) >> SEL_SHIFT
    forced = (blk == 0) | ((tb - blk >= 0) & (tb - blk < SEL_LOCAL))
    n_forced = 1 + SEL_LOCAL
    premark = forced & (jnp.full((n_sel, tq), qi, jnp.int32) > 0)
    x_buf[0:n_sel] = jnp.where(premark, -jnp.inf, jnp.where(forced, FORCE, jnp.where(blk > tb, -FORCE, imp)))

    blk_f = blk.astype(F32)

    def pick(_, carry):
        x = x_buf[0:n_sel]
        top = jnp.max(x, axis=0, keepdims=True)
        first = jnp.min(jnp.where(x == top, blk_f, float(n_sel)), axis=0, keepdims=True)
        x_buf[0:n_sel] = jnp.where(blk_f == first, -jnp.inf, x)
        return carry

    lax.fori_loop(0, jnp.where(qi > 0, n_top - n_forced, n_top), pick, 0)
    mb_ref[0, 0:n_sel, :] = jnp.where(x_buf[0:n_sel] == -jnp.inf, 0.0, NEG_INF).astype(BF16)
    if n_sel < mb_ref.shape[1]:
        mb_ref[0, n_sel:, :] = jnp.full((mb_ref.shape[1] - n_sel, tq), NEG_INF, BF16)


def _cmp_topk_kernel(*refs, n_top, n_tiles):
    nc = refs[1].shape[1]
    n_sel = refs[3].shape[0]
    qi = pl.program_id(2)
    for v in range(max(n_tiles // 2, 1)):
        frac = lambda total: min(total, (2 * v + 2) * total // n_tiles)
        pl.when((qi >> 1) == v)(functools.partial(_cmp_topk_body, *refs, n_top=n_top, nc=frac(nc), n_sel=frac(n_sel)))


def _cmp_topk(qn_t, kc, vct, ov, gates, b, g_kv, s_len, n_top):
    tq = min(CMP_TILE, s_len)
    sub = min(CMP_SUBTILE, tq)
    nq = s_len // tq
    nc = kc.shape[1]
    n_sel = ov.shape[0]
    rep = NSA_REP
    return pl.pallas_call(
        functools.partial(_cmp_topk_kernel, n_top=n_top, n_tiles=nq),
        grid=(b, g_kv, nq),
        in_specs=[
            pl.BlockSpec((rep * HEAD_DIM, tq), lambda bi, gi, qi: (gi, bi * nq + qi)),
            pl.BlockSpec((1, nc, HEAD_DIM), lambda bi, gi, qi: (bi * g_kv + gi, 0, 0)),
            pl.BlockSpec((1, HEAD_DIM, nc), lambda bi, gi, qi: (bi * g_kv + gi, 0, 0)),
            _resident(ov.shape),
            pl.BlockSpec((1, 1, rep, tq), lambda bi, gi, qi: (0, gi, 0, bi * nq + qi)),
        ],
        out_specs=[
            pl.BlockSpec((1, n_sel, tq), lambda bi, gi, qi: (bi * g_kv + gi, 0, qi)),
            pl.BlockSpec((rep * HEAD_DIM, tq), lambda bi, gi, qi: (gi, bi * nq + qi)),
        ],
        out_shape=[jax.ShapeDtypeStruct((b * g_kv, n_sel, s_len), BF16),
                   jax.ShapeDtypeStruct((NSA_Q_W, b * s_len), F32)],
        scratch_shapes=[pltpu.VMEM((nc, sub), F32), pltpu.VMEM((nc, sub), BF16), pltpu.VMEM((nc, sub), F32),
                        pltpu.VMEM((nc, tq), F32), pltpu.VMEM((n_sel, tq), F32)],
        compiler_params=_params(3),
        name="cmp_topk",
    )(qn_t, kc, vct, ov, gates)


def _softmax_tile(m, s):
    m_new = jnp.maximum(m, jnp.max(s, axis=0, keepdims=True))
    return m_new, jnp.exp2(m - m_new), jnp.exp2(s - m_new).astype(BF16)


def _normalize(acc):
    return acc[:HEAD_DIM] / acc[HEAD_DIM:HEAD_DIM + 1]


def _rel_pos(tk, tq):
    return lax.broadcasted_iota(jnp.int32, (tk, tq), 1) - lax.broadcasted_iota(jnp.int32, (tk, tq), 0)


def _window_bias(rel, back, tk, kj, window):
    dist = rel + (back * tk + jnp.where(kj >= 0, 0, window))
    return jnp.where((dist >= 0) & (dist < window), 0.0, NEG_INF)


def _pipe_stage(s_buf, p_buf, heads, cur=None, nxt=None, prv=None):
    if nxt is not None:
        s_next = [_dot(nxt[0], nxt[1][r][...]) for r in heads]
    if prv is not None:
        pv = [_dot(prv[0], p_buf[r]) for r in heads]
    out = None
    if cur is not None:
        m, bias = cur
        out = []
        for r in heads:
            s = s_buf[r] if bias is None else s_buf[r] + bias
            m_r, a_r, p_buf[r] = _softmax_tile(m[r], s)
            out.append((m_r, a_r))
        out = tuple(zip(*out))
    for r in heads:
        if prv is not None:
            prv[1][r] = prv[2][r] * prv[1][r] + pv[r]
        if nxt is not None:
            s_buf[r] = s_next[r]
    return out


def _nsa_kernel(q_ref, mb_ref, oc_ref, gate_ref, ks_ref, ex_ref, vs_ref, kw_ref, vw_ref, o_ref, *scratch):
    n_sub = q_ref.shape[1] // NSA_TILE
    per_tile = len(scratch) // n_sub
    for sub in range(n_sub):
        at = pl.ds(sub * NSA_TILE, NSA_TILE)
        _nsa_tile(pl.program_id(2) * n_sub + sub, q_ref.at[:, at], mb_ref.at[:, :, at], oc_ref.at[:, at],
                  gate_ref.at[:, :, :, at], ks_ref, ex_ref, vs_ref, kw_ref, vw_ref, o_ref.at[at, :],
                  *scratch[sub * per_tile:(sub + 1) * per_tile])


def _nsa_tile(qi, q_ref, mb_ref, oc_ref, gate_ref, ks_ref, ex_ref, vs_ref, kw_ref, vw_ref, o_ref,
              qa_buf, s_buf, p_buf, sel_acc, win_acc):
    tq = q_ref.shape[1]
    tk = tq
    n_sel = mb_ref.shape[1]
    rep = NSA_REP
    heads = range(rep)
    stage = functools.partial(_pipe_stage, s_buf, p_buf, heads)
    key_rows = lambda j: pl.ds(pl.multiple_of(j * tk, tk), tk)

    def sel_keys(j):
        return jnp.concatenate([ex_ref[key_rows(j), :], ks_ref[0, key_rows(j), :]], axis=1)

    q_heads = [q_ref.at[pl.ds(r * HEAD_DIM, HEAD_DIM), :] for r in heads]
    q_aug = [qa_buf.at[r] for r in heads]
    for r in heads:
        qa_buf[r, 0:n_sel, :] = mb_ref[0]
        qa_buf[r, n_sel:n_sel + HEAD_DIM, :] = q_heads[r][...]
        s_buf[r] = _dot(sel_keys(0), qa_buf[r])
        p_buf[r] = jnp.zeros((tk, tq), BF16)
        sel_acc[r] = jnp.zeros((VAL_ROWS, tq), F32)
        win_acc[r] = jnp.zeros((VAL_ROWS, tq), F32)
    row = lambda v: tuple(jnp.full((1, tq), v, F32) for _ in heads)

    def sel_stage(i, carry):
        m, alpha = carry
        return stage(cur=(m, None), nxt=(sel_keys(i + 1), q_aug),
                     prv=(vs_ref[jnp.maximum(i - 1, 0)], sel_acc, alpha))

    carry = lax.fori_loop(0, qi >> 1, lambda j, c: sel_stage(2 * j + 1, sel_stage(2 * j, c)),
                          (row(NEG_INF), row(1.0)))
    m, alpha = lax.fori_loop(0, qi & 1, lambda _, c: sel_stage(qi - 1, c), carry)

    rel = _rel_pos(tk, tq)
    backs = list(range(NSA_WINDOW // tk, -1, -1))
    win_tile = [jnp.maximum(qi - back, 0) for back in backs]
    win_bias = [_window_bias(rel, back, tk, qi - back, NSA_WINDOW) for back in backs]
    _, alpha_d = stage(cur=(m, jnp.where(rel >= 0, 0.0, NEG_INF)),
                       nxt=(kw_ref[0, key_rows(win_tile[0]), :], q_heads),
                       prv=(vs_ref[jnp.maximum(qi - 1, 0)], sel_acc, alpha))
    prv = (vs_ref[qi], sel_acc, alpha_d)
    m_w = row(NEG_INF)
    for n in range(len(backs)):
        nxt = (kw_ref[0, key_rows(win_tile[n + 1]), :], q_heads) if n + 1 < len(backs) else None
        m_w, alpha_w = stage(cur=(m_w, win_bias[n]), nxt=nxt, prv=prv)
        prv = (vw_ref[win_tile[n]], win_acc, alpha_w)
    stage(prv=prv)

    outs = [oc_ref[r * HEAD_DIM:(r + 1) * HEAD_DIM, :]
            + gate_ref[1, 0, r:r + 1, :] * _normalize(sel_acc[r])
            + gate_ref[2, 0, r:r + 1, :] * _normalize(win_acc[r]) for r in heads]
    o_ref[...] = jnp.concatenate(outs, axis=0).T.astype(o_ref.dtype)


def _nsa(qn_t, mb, oc_t, gates, ksel, expand, vsel, kwin, vwin, b, g_kv, s_len):
    tk = NSA_TILE
    tq = min(NSA_STEP, s_len)
    nq = s_len // tq
    rep = NSA_REP
    n_sel = mb.shape[1]
    keys = pl.BlockSpec((1, s_len, HEAD_DIM), lambda bi, gi, qi: (gi * b + bi, 0, 0))
    vals = pl.BlockSpec((s_len // tk, VAL_ROWS, tk), lambda bi, gi, qi: (bi, gi, 0))
    return pl.pallas_call(
        _nsa_kernel,
        grid=(b, g_kv, nq),
        in_specs=[
            pl.BlockSpec((rep * HEAD_DIM, tq), lambda bi, gi, qi: (gi, bi * nq + qi)),
            pl.BlockSpec((1, n_sel, tq), lambda bi, gi, qi: (bi * g_kv + gi, 0, qi)),
            pl.BlockSpec((rep * HEAD_DIM, tq), lambda bi, gi, qi: (gi, bi * nq + qi)),
            pl.BlockSpec((3, 1, rep, tq), lambda bi, gi, qi: (0, gi, 0, bi * nq + qi)),
            keys, _resident(expand.shape), vals, keys, vals,
        ],
        out_specs=pl.BlockSpec((tq, rep * HEAD_DIM), lambda bi, gi, qi: (bi * nq + qi, gi)),
        out_shape=jax.ShapeDtypeStruct((b * s_len, NSA_Q_W), BF16),
        scratch_shapes=[pltpu.VMEM((rep, n_sel + HEAD_DIM, tk), BF16), pltpu.VMEM((rep, tk, tk), F32),
                        pltpu.VMEM((rep, tk, tk), BF16), pltpu.VMEM((rep, VAL_ROWS, tk), F32),
                        pltpu.VMEM((rep, VAL_ROWS, tk), F32)] * (tq // tk),
        compiler_params=_params(3),
        name="nsa",
    )(qn_t, mb, oc_t, gates, ksel.reshape(g_kv * b, s_len, HEAD_DIM), expand, vsel,
      kwin.reshape(g_kv * b, s_len, HEAD_DIM), vwin)


def _swa_kernel(q_ref, sink_ref, k_ref, v_ref, o_ref, *scratch):
    n_sub = q_ref.shape[1] // SWA_TILE
    per_tile = len(scratch) // n_sub
    for sub in range(n_sub):
        at = pl.ds(sub * SWA_TILE, SWA_TILE)
        _swa_tile(pl.program_id(1) * n_sub + sub, q_ref.at[:, at], sink_ref, k_ref, v_ref, o_ref.at[at, :],
                  *scratch[sub * per_tile:(sub + 1) * per_tile])


def _swa_tile(qi, q_ref, sink_ref, k_ref, v_ref, o_ref, s_buf, p_buf, bias_buf, o_buf):
    tq = q_ref.shape[1]
    tv = v_ref.shape[2]
    nk = s_buf.shape[0]
    heads = SWA_HEADS
    q0 = qi * tq
    k0 = jnp.maximum(q0 - SWA_WINDOW, 0)

    keys = k_ref[0, pl.ds(pl.multiple_of(k0, tv), nk), :]
    vals = jnp.concatenate([v_ref[k0 // tv + j] for j in range(nk // tv)], axis=1)
    dist = _rel_pos(nk, tq) + (q0 - k0)
    bias_buf[...] = jnp.where((dist >= 0) & (dist < SWA_WINDOW), 0.0, NEG_INF)
    q_head = lambda h: q_ref[h * HEAD_DIM:(h + 1) * HEAD_DIM, :]

    def finish(h, m_all, sink):
        acc = _dot(vals, p_buf[...])
        o_buf[h * HEAD_DIM:(h + 1) * HEAD_DIM, :] = (
            acc[:HEAD_DIM] / (acc[HEAD_DIM:HEAD_DIM + 1] + jnp.exp2(sink - m_all)))

    s_buf[...] = _dot(keys, q_head(0))
    prev = None
    for h in range(heads):
        if h + 1 < heads:
            s_next = _dot(keys, q_head(h + 1))
        if prev is not None:
            finish(*prev)
        sink = sink_ref[:, h * tq:(h + 1) * tq]
        s = s_buf[...] + bias_buf[...]
        m_all = jnp.maximum(jnp.max(s, axis=0, keepdims=True), sink)
        p_buf[...] = jnp.exp2(s - m_all).astype(BF16)
        prev = (h, m_all, sink)
        if h + 1 < heads:
            s_buf[...] = s_next
    finish(*prev)
    o_ref[...] = o_buf[...].T.astype(o_ref.dtype)


def _swa(sq_t, sink_row, k, v, b, s_len):
    tile = min(SWA_TILE, s_len)
    tq = min(SWA_STEP, s_len)
    nq = s_len // tq
    tv = v.shape[2]
    nk = tile + SWA_WINDOW
    return pl.pallas_call(
        _swa_kernel,
        grid=(b, nq),
        in_specs=[pl.BlockSpec((SWA_Q_W, tq), lambda bi, qi: (0, bi * nq + qi)),
                  _resident(sink_row.shape),
                  pl.BlockSpec((1, s_len, HEAD_DIM), lambda bi, qi: (bi, 0, 0)),
                  pl.BlockSpec((s_len // tv, VAL_ROWS, tv), lambda bi, qi: (bi, 0, 0))],
        out_specs=pl.BlockSpec((tq, SWA_Q_W), lambda bi, qi: (bi * nq + qi, 0)),
        out_shape=jax.ShapeDtypeStruct((b * s_len, SWA_Q_W), BF16),
        scratch_shapes=[pltpu.VMEM((nk, tile), F32), pltpu.VMEM((nk, tile), BF16), pltpu.VMEM((nk, tile), F32),
                        pltpu.VMEM((SWA_Q_W, tile), F32)] * (tq // tile),
        compiler_params=_params(2),
        name="swa",
    )(sq_t, sink_row, k.reshape(b, s_len, HEAD_DIM), v)


def _overlap_matrix(n_sel, n_cmp, n_cmp_pad):
    cs = np.arange(n_cmp) * CMP_STRIDE
    ss = np.arange(n_sel) * SEL_LEN
    ov = np.clip(np.minimum(cs[None, :] + CMP_LEN, ss[:, None] + SEL_LEN)
                 - np.maximum(cs[None, :], ss[:, None]), 0, None).astype(np.float32) / CMP_LEN
    return np.pad(ov, ((0, 0), (0, n_cmp_pad - n_cmp)))


def _layer(h, positions, w, b, s_len, norm_final, last):
    n, d = h.shape
    g_kv = NSA_KV
    n_sel = s_len // SEL_LEN
    n_top = min(SEL_TOPN, n_sel)
    nc = s_len // CMP_STRIDE
    bf = lambda a: a.astype(BF16)

    h = _ffn(h, w['norm_ffn1'][None], bf(w['ffn1_gate']), bf(w['ffn1_up']), bf(w['ffn1_down']),
             w['norm_ffn1'][None], False)

    pts = np.cumsum((NSA_Q_W,) + (NSA_KV_W,) * 6 + (NSA_GATE_W, SWA_Q_W, SWA_KV_W, SWA_KV_W, d, d))[:-1]
    (w_nq, w_kc, w_vc, w_ksl, w_vsl, w_kwn, w_vwn, w_ng, w_sq, w_sk, w_sv, w_ga, w_gb) = jnp.split(
        bf(w['w_in']), pts, axis=1)
    parts = dict(qn=w_nq, sq=w_sq, ksel=w_ksl, kwin=w_kwn, kswa=w_sk, vsel=w_vsl, vwin=w_vwn, vswa=w_sv,
                 ng=w_ng, kc=w_kc, vc=w_vc)
    w_t = jnp.concatenate([jnp.pad(parts[name], ((0, 0), (0, width - parts[name].shape[1])))
                           for name, width in PROJ_ROWS], axis=1).T
    w_n = jnp.concatenate([w_ga, w_gb], axis=1)

    freq_col = (ROPE_THETA ** (-jnp.arange(HALF_DIM, dtype=F32) / HALF_DIM))[:, None]
    (qn_t, sq_t, ksel, kwin, kswa, vsel, vwin, vswa, ng_t, kcv, gab) = _proj(
        h, w['norm_mix'][None], w_t, w_n, positions.reshape(1, n), freq_col)

    pos_c = jnp.pad(positions[:, CMP_LEN - 1::CMP_STRIDE], ((0, 0), (0, 1)))[:, None, :]
    w1 = lambda a: bf(a.reshape(CMP_LEN, HEAD_DIM, a.shape[-1]))
    kc, vc_t = _compress(kcv, w['cmp_pe_k'], w['cmp_pe_v'], w1(w['cmp_k_w1']), w1(w['cmp_v_w1']),
                         bf(w['cmp_k_w2'].T), bf(w['cmp_v_w2'].T), pos_c, freq_col, b, g_kv, s_len)

    gates = ng_t.reshape(3, g_kv, NSA_REP, n)
    ov = jnp.asarray(_overlap_matrix(n_sel, nc - 1, nc), BF16)
    mb, oc_t = _cmp_topk(qn_t, kc, vc_t, ov, gates, b, g_kv, s_len, n_top)

    expand = jnp.asarray(np.arange(s_len)[:, None] // SEL_LEN == np.arange(n_sel)[None, :], BF16)
    o_a = _nsa(qn_t, mb, oc_t, gates, ksel, expand, vsel, kwin, vwin, b, g_kv, s_len)

    sink_row = jnp.repeat(w['swa_sinks'].astype(F32) * LOG2E, min(SWA_TILE, s_len))[None]
    o_b = _swa(sq_t, sink_row, kswa, vswa, b, s_len)

    branches = (o_a, o_b, gab, bf(w['w_branch_a']), bf(w['w_branch_b']), bf(w['w_out']))
    return _ffn(h, w['norm_ffn2'][None], bf(w['ffn2_gate']), bf(w['ffn2_up']), bf(w['ffn2_down']),
                norm_final[None], last, branches)


def kernel(x, positions, norm_ffn1, ffn1_gate, ffn1_up, ffn1_down, norm_mix, w_in, cmp_pe_k, cmp_k_w1, cmp_k_w2, cmp_pe_v, cmp_v_w1, cmp_v_w2, swa_sinks, w_branch_a, w_branch_b, w_out, norm_ffn2, ffn2_gate, ffn2_up, ffn2_down, norm_final):
    b, s_len, d = x.shape
    stacked = dict(norm_ffn1=norm_ffn1, ffn1_gate=ffn1_gate, ffn1_up=ffn1_up, ffn1_down=ffn1_down,
                   norm_mix=norm_mix, w_in=w_in, cmp_pe_k=cmp_pe_k, cmp_k_w1=cmp_k_w1, cmp_k_w2=cmp_k_w2,
                   cmp_pe_v=cmp_pe_v, cmp_v_w1=cmp_v_w1, cmp_v_w2=cmp_v_w2, swa_sinks=swa_sinks,
                   w_branch_a=w_branch_a, w_branch_b=w_branch_b, w_out=w_out,
                   norm_ffn2=norm_ffn2, ffn2_gate=ffn2_gate, ffn2_up=ffn2_up, ffn2_down=ffn2_down)
    depth = norm_ffn1.shape[0]
    h = x.reshape(b * s_len, d)
    for i in range(depth):
        w = {k: v[i] for k, v in stacked.items()}
        h = _layer(h, positions, w, b, s_len, norm_final, i == depth - 1)
    return h.reshape(b, s_len, d)
```

```python
import functools

import numpy as np
import jax
import jax.numpy as jnp
from jax import lax
from jax.experimental import pallas as pl
from jax.experimental.pallas import tpu as pltpu

HEAD_DIM = 64
HALF_DIM = HEAD_DIM // 2
NSA_HEADS = 8
NSA_KV = 2
NSA_REP = NSA_HEADS // NSA_KV
SWA_HEADS = 8
CMP_STRIDE = 16
CMP_SHIFT = 4
CMP_LEN = 2 * CMP_STRIDE
SEL_LEN = 64
SEL_SHIFT = 6
SEL_TOPN = 16
SEL_LOCAL = 2
NSA_WINDOW = 512
SWA_WINDOW = 128
ROPE_THETA = 10000.0
RMS_EPS = 1e-6
FFN_HALF = 0.5
NEG_INF = -1e30
FORCE = 1e9
LOG2E = 1.4426950408889634
Q_SCALE = HEAD_DIM ** -0.5 * LOG2E
BF16_ROWS = 16
VAL_ROWS = HEAD_DIM + BF16_ROWS
LANES = 128

NSA_Q_W = NSA_HEADS * HEAD_DIM
NSA_KV_W = NSA_KV * HEAD_DIM
NSA_GATE_W = 3 * NSA_HEADS
SWA_Q_W = SWA_HEADS * HEAD_DIM
SWA_KV_W = HEAD_DIM

VMEM_LIMIT_BYTES = 56 * 1024 * 1024

BF16 = jnp.bfloat16
F32 = jnp.float32

ROW_TILE = 512
FFN_CHUNK = 256
PROJ_TILE = 1024
PROJ_CHUNK = 512
CMP_TILE = 1024
CMP_SUBTILE = 256
NSA_TILE = 256
NSA_STEP = 1024
SWA_TILE = 256
SWA_STEP = 1024
SWA_VTILE = 128


def _params(n_axes):
    return pltpu.CompilerParams(dimension_semantics=("arbitrary",) * n_axes,
                                vmem_limit_bytes=VMEM_LIMIT_BYTES)


def _resident(shape):
    zeros = (0,) * len(shape)
    return pl.BlockSpec(shape, lambda *_: zeros, pipeline_mode=pl.Buffered(1))


def _rms(x, g):
    y = x * lax.rsqrt(jnp.mean(x * x, axis=-1, keepdims=True) + RMS_EPS)
    return y * g


def _dot(a, b):
    return jnp.dot(a, b, preferred_element_type=F32)


def _dot_nt(a, b):
    return lax.dot_general(a, b, (((1,), (1,)), ((), ())), preferred_element_type=F32)


def _rope_angles(pos_row, freq_col):
    ang = pos_row.astype(F32) * freq_col
    return jnp.cos(ang), jnp.sin(ang)


def _rope_rows(block, cos_t, sin_t):
    out = []
    for hd in range(block.shape[0] // HEAD_DIM):
        x1 = block[hd * HEAD_DIM:hd * HEAD_DIM + HALF_DIM]
        x2 = block[hd * HEAD_DIM + HALF_DIM:(hd + 1) * HEAD_DIM]
        out += [x1 * cos_t - x2 * sin_t, x2 * cos_t + x1 * sin_t]
    return out


def _ones_rows(tk):
    return jnp.where(lax.broadcasted_iota(jnp.int32, (BF16_ROWS, tk), 0) == 0, 1.0, 0.0).astype(BF16)


def _merge_branches(h_ref, oa_ref, ob_ref, gab_ref, wa_ref, wb_ref, wo_ref):
    d = h_ref.shape[1]
    gab = gab_ref[...]
    merged = gab[:, :d] * _dot(oa_ref[...], wa_ref[...]) + gab[:, d:] * _dot(ob_ref[...], wb_ref[...])
    return h_ref[...] + _dot(merged.astype(BF16), wo_ref[...])


def _ffn_kernel(*refs, final_norm, merge):
    x_ref, g_ref, wg_ref, wu_ref, wd_ref, gf_ref, o_ref = refs[-7:]
    chunk = x_ref.shape[0] if merge else FFN_CHUNK
    for part in range(x_ref.shape[0] // chunk):
        at = pl.ds(part * chunk, chunk)
        x = _merge_branches(x_ref, *refs[:-7]) if merge else x_ref[at, :]
        xb = _rms(x, g_ref[...]).astype(BF16)
        a = _dot(xb, wg_ref[...])
        b = _dot(xb, wu_ref[...])
        t = (a * jax.nn.sigmoid(a)) * b
        h = x + FFN_HALF * _dot(t.astype(BF16), wd_ref[...])
        if final_norm:
            h = _rms(h, gf_ref[...])
        o_ref[at, :] = h


def _ffn(x2, g, wg, wu, wd, gf, final_norm, branches=None):
    n, d = x2.shape
    f = wg.shape[1]
    tm = ROW_TILE
    rows = lambda w: pl.BlockSpec((tm, w), lambda i: (i, 0))
    merge_specs, merge_args = [], ()
    if branches is not None:
        oa, ob, gab, wa, wb, wo = branches
        merge_specs = [rows(oa.shape[1]), rows(ob.shape[1]), rows(2 * d),
                       _resident(wa.shape), _resident(wb.shape), _resident(wo.shape)]
        merge_args = branches
    return pl.pallas_call(
        functools.partial(_ffn_kernel, final_norm=final_norm, merge=branches is not None),
        grid=(n // tm,),
        in_specs=merge_specs + [rows(d), _resident((1, d)), _resident((d, f)), _resident((d, f)),
                                _resident((f, d)), _resident((1, d))],
        out_specs=rows(d),
        out_shape=jax.ShapeDtypeStruct((n, d), F32),
        compiler_params=_params(1),
        name="ffn_final" if final_norm else "ffn",
    )(*merge_args, x2, g, wg, wu, wd, gf)


PROJ_ROWS = (('qn', NSA_Q_W), ('sq', SWA_Q_W), ('ksel', NSA_KV_W), ('kwin', NSA_KV_W), ('kswa', LANES),
             ('vsel', NSA_KV_W), ('vwin', NSA_KV_W), ('vswa', SWA_KV_W), ('ng', 32),
             ('kc', NSA_KV_W), ('vc', NSA_KV_W))


def _proj_kernel(h_ref, g_ref, wt_ref, wn_ref, pos_ref, freq_ref,
                 qn_ref, sq_ref, ksel_ref, kwin_ref, kswa_ref, vsel_ref, vwin_ref, vswa_ref,
                 ng_ref, kcv_ref, gab_ref):
    for part in range(h_ref.shape[0] // PROJ_CHUNK):
        at = pl.ds(part * PROJ_CHUNK, PROJ_CHUNK)
        tiles = lambda ref: ref.at[pl.ds(part * (PROJ_CHUNK // ref.shape[2]), PROJ_CHUNK // ref.shape[2])]
        _proj_chunk(h_ref.at[at, :], g_ref, wt_ref, wn_ref, pos_ref.at[:, at], freq_ref,
                    qn_ref.at[:, at], sq_ref.at[:, at], ksel_ref.at[:, at, :], kwin_ref.at[:, at, :],
                    kswa_ref.at[at, :], tiles(vsel_ref), tiles(vwin_ref), tiles(vswa_ref),
                    ng_ref.at[:, at], kcv_ref.at[:, at, :], gab_ref.at[at, :])


def _proj_chunk(h_ref, g_ref, wt_ref, wn_ref, pos_ref, freq_ref,
                qn_ref, sq_ref, ksel_ref, kwin_ref, kswa_ref, vsel_ref, vwin_ref, vswa_ref,
                ng_ref, kcv_ref, gab_ref):
    ub = _rms(h_ref[...], g_ref[...]).astype(BF16)
    yt = _dot_nt(wt_ref[...], ub)
    rows, o = {}, 0
    for name, width in PROJ_ROWS:
        rows[name] = yt[o:o + width]
        o += width
    cos_t, sin_t = _rope_angles(pos_ref[...], freq_ref[...])

    for name, ref in (('qn', qn_ref), ('sq', sq_ref)):
        for i, piece in enumerate(_rope_rows(rows[name], cos_t, sin_t)):
            ref[i * HALF_DIM:(i + 1) * HALF_DIM, :] = (piece * Q_SCALE).astype(BF16)

    for name, ref in (('ksel', ksel_ref), ('kwin', kwin_ref)):
        k_nat = jnp.concatenate(_rope_rows(rows[name], cos_t, sin_t), axis=0).T
        for g in range(NSA_KV):
            ref[g] = k_nat[:, g * HEAD_DIM:(g + 1) * HEAD_DIM].astype(BF16)
    kswa = _rope_rows(rows['kswa'][:HEAD_DIM], cos_t, sin_t) + [rows['kswa'][HEAD_DIM:]]
    kswa_ref[...] = jnp.concatenate(kswa, axis=0).T[:, :HEAD_DIM].astype(BF16)

    for name, ref, groups in (('vsel', vsel_ref, NSA_KV), ('vwin', vwin_ref, NSA_KV), ('vswa', vswa_ref, 1)):
        tk = ref.shape[2]
        for j in range(ref.shape[0]):
            for g in range(groups):
                ref[j, g * VAL_ROWS:g * VAL_ROWS + HEAD_DIM, :] = (
                    rows[name][g * HEAD_DIM:(g + 1) * HEAD_DIM, j * tk:(j + 1) * tk].astype(BF16))
                ref[j, g * VAL_ROWS + HEAD_DIM:(g + 1) * VAL_ROWS, :] = _ones_rows(tk)

    ng_ref[...] = jax.nn.sigmoid(rows['ng'][:NSA_GATE_W])

    for i, name in enumerate(('kc', 'vc')):
        nat = rows[name].T
        for g in range(NSA_KV):
            kcv_ref[i * NSA_KV + g] = nat[:, g * HEAD_DIM:(g + 1) * HEAD_DIM]

    gab_ref[...] = jax.nn.sigmoid(_dot(ub, wn_ref[...]))


def _proj(h2, g, wt, wn, pos_row, freq_col):
    n, d = h2.shape
    tm = PROJ_TILE
    rows = lambda w: pl.BlockSpec((tm, w), lambda i: (i, 0))
    cols = lambda w: pl.BlockSpec((w, tm), lambda i: (0, i))
    grouped = lambda k: pl.BlockSpec((k, tm, HEAD_DIM), lambda i: (0, i, 0))
    tiles = lambda groups, tk: pl.BlockSpec((tm // tk, groups * VAL_ROWS, tk), lambda i: (i, 0, 0))
    val_shape = lambda groups, tk: jax.ShapeDtypeStruct((n // tk, groups * VAL_ROWS, tk), BF16)
    out_shape = [
        jax.ShapeDtypeStruct((NSA_Q_W, n), BF16), jax.ShapeDtypeStruct((SWA_Q_W, n), BF16),
        jax.ShapeDtypeStruct((NSA_KV, n, HEAD_DIM), BF16), jax.ShapeDtypeStruct((NSA_KV, n, HEAD_DIM), BF16),
        jax.ShapeDtypeStruct((n, HEAD_DIM), BF16),
        val_shape(NSA_KV, NSA_TILE), val_shape(NSA_KV, NSA_TILE), val_shape(1, SWA_VTILE),
        jax.ShapeDtypeStruct((NSA_GATE_W, n), F32), jax.ShapeDtypeStruct((2 * NSA_KV, n, HEAD_DIM), F32),
        jax.ShapeDtypeStruct((n, 2 * d), F32),
    ]
    return pl.pallas_call(
        _proj_kernel,
        grid=(n // tm,),
        in_specs=[rows(d), _resident((1, d)), _resident(wt.shape), _resident(wn.shape),
                  cols(1), _resident(freq_col.shape)],
        out_specs=[cols(NSA_Q_W), cols(SWA_Q_W), grouped(NSA_KV), grouped(NSA_KV), rows(HEAD_DIM),
                   tiles(NSA_KV, NSA_TILE), tiles(NSA_KV, NSA_TILE), tiles(1, SWA_VTILE),
                   cols(NSA_GATE_W), grouped(2 * NSA_KV), rows(2 * d)],
        out_shape=out_shape,
        compiler_params=_params(1),
        name="proj",
    )(h2, g, wt, wn, pos_row, freq_col)


def _gelu(x):
    return jax.nn.gelu(x, approximate=True)


def _compress_kernel(kc_ref, vc_ref, pek_ref, pev_ref, w1k_ref, w1v_ref, w2kt_ref, w2vt_ref,
                     pos_ref, freq_ref, kc_out, vct_out):
    nc = kc_out.shape[1]

    def hidden(x_ref, pe_ref, w1_ref):
        top = bot = None
        for j in range(CMP_STRIDE):
            x = x_ref[0, pl.ds(j, nc, stride=CMP_STRIDE), :]
            t = _dot((x + pe_ref[j:j + 1, :]).astype(BF16), w1_ref[j])
            b = _dot((x + pe_ref[CMP_STRIDE + j:CMP_STRIDE + j + 1, :]).astype(BF16), w1_ref[CMP_STRIDE + j])
            top, bot = (t, b) if top is None else (top + t, bot + b)
        return _gelu(top + pltpu.roll(bot, shift=nc - 1, axis=0)).astype(BF16)

    kt = _dot_nt(w2kt_ref[...], hidden(kc_ref, pek_ref, w1k_ref))
    cos_t, sin_t = _rope_angles(pos_ref[0], freq_ref[...])
    kt = jnp.concatenate(_rope_rows(kt, cos_t, sin_t) + [jnp.zeros((LANES - HEAD_DIM, nc), F32)], axis=0)
    kc_out[0] = kt.T[:, :HEAD_DIM].astype(BF16)
    vct_out[0] = _dot_nt(w2vt_ref[...], hidden(vc_ref, pev_ref, w1v_ref)).astype(BF16)


def _compress(kcv, pek, pev, w1k, w1v, w2kt, w2vt, pos_c, freq_col, b, g_kv, s_len):
    nc = s_len // CMP_STRIDE
    hid = w1k.shape[-1]
    kcv = kcv.reshape(2 * g_kv * b, s_len, HEAD_DIM)
    src = lambda kind: pl.BlockSpec((1, s_len, HEAD_DIM),
                                    lambda i: ((kind * g_kv + i % g_kv) * b + i // g_kv, 0, 0))
    per = lambda *s: pl.BlockSpec((1,) + s, lambda i: (i,) + (0,) * len(s))
    return pl.pallas_call(
        _compress_kernel,
        grid=(b * g_kv,),
        in_specs=[src(0), src(1), _resident(pek.shape), _resident(pev.shape),
                  _resident(w1k.shape), _resident(w1v.shape), _resident((HEAD_DIM, hid)),
                  _resident((HEAD_DIM, hid)),
                  pl.BlockSpec((1, 1, nc), lambda i: (i // g_kv, 0, 0)), _resident(freq_col.shape)],
        out_specs=[per(nc, HEAD_DIM), per(HEAD_DIM, nc)],
        out_shape=[jax.ShapeDtypeStruct((b * g_kv, nc, HEAD_DIM), BF16),
                   jax.ShapeDtypeStruct((b * g_kv, HEAD_DIM, nc), BF16)],
        compiler_params=_params(1),
        name="compress",
    )(kcv, kcv, pek, pev, w1k, w1v, w2kt, w2vt, pos_c, freq_col)


def _cmp_topk_body(q_ref, kc_ref, vct_ref, ov_ref, gate_ref, mb_ref, oc_ref,
                   s_buf, p_buf, bias_buf, psum_buf, x_buf, *, n_top, nc, n_sel):
    tq = q_ref.shape[1]
    sub = s_buf.shape[1]
    rep = NSA_REP
    qi = pl.program_id(2)
    q0 = qi * tq
    stages = [(u, r) for u in range(tq // sub) for r in range(rep)]
    q_cols = lambda u, r: q_ref[r * HEAD_DIM:(r + 1) * HEAD_DIM, u * sub:(u + 1) * sub]
    keys = kc_ref[0, 0:nc, :]
    vals = vct_ref[0, :, 0:nc]

    s_buf[0:nc] = _dot(keys, q_cols(*stages[0]))
    inv_prev = None
    for n, (u, r) in enumerate(stages):
        cols = slice(u * sub, (u + 1) * sub)
        if n + 1 < len(stages):
            s_next = _dot(keys, q_cols(*stages[n + 1]))
        if n > 0:
            pu, pr = stages[n - 1]
            oc_ref[pr * HEAD_DIM:(pr + 1) * HEAD_DIM, pu * sub:(pu + 1) * sub] = (
                _dot(vals, p_buf[0:nc]) * (inv_prev * gate_ref[0, 0, pr:pr + 1, pu * sub:(pu + 1) * sub]))
        last = ((q0 + u * sub + lax.broadcasted_iota(jnp.int32, (1, sub), 1)) - (CMP_LEN - 1)) >> CMP_SHIFT
        if r == 0:
            bias_buf[0:nc] = jnp.where(lax.broadcasted_iota(jnp.int32, (nc, sub), 0) <= last, 0.0, NEG_INF)
        s = s_buf[0:nc] + bias_buf[0:nc]
        m = jnp.max(s, axis=0, keepdims=True)
        e = jnp.exp2(s - m)
        inv_prev = jnp.where(last >= 0, 1.0 / jnp.sum(e, axis=0, keepdims=True), 0.0)
        p_buf[0:nc] = e.astype(BF16)
        if r == 0:
            psum_buf[0:nc, cols] = e * inv_prev
        else:
            psum_buf[0:nc, cols] += e * inv_prev
        if n + 1 < len(stages):
            s_buf[0:nc] = s_next
    pu, pr = stages[-1]
    oc_ref[pr * HEAD_DIM:(pr + 1) * HEAD_DIM, pu * sub:(pu + 1) * sub] = (
        _dot(vals, p_buf[0:nc]) * (inv_prev * gate_ref[0, 0, pr:pr + 1, pu * sub:(pu + 1) * sub]))

    psum = psum_buf[0:nc, :]
    p_hi = psum.astype(BF16)
    p_lo = (psum - p_hi.astype(F32)).astype(BF16)
    pool = ov_ref[0:n_sel, 0:nc]
    imp = _dot(pool, p_hi) + _dot(pool, p_lo)

    blk = lax.broadcasted_iota(jnp.int32, (n_sel, tq), 0)
    tb = (q0 + lax.broadcasted_iota(jnp.int32, (n_sel, tq), 1)) >> SEL_SHIFT
    forced = (blk == 0) | ((tb - blk >= 0) & (tb - blk < SEL_LOCAL))
    n_forced = 1 + SEL_LOCAL
    premark = forced & (jnp.full((n_sel, tq), qi, jnp.int32) > 0)
    x_buf[0:n_sel] = jnp.where(premark, -jnp.inf, jnp.where(forced, FORCE, jnp.where(blk > tb, -FORCE, imp)))

    blk_f = blk.astype(F32)

    def pick(_, carry):
        x = x_buf[0:n_sel]
        top = jnp.max(x, axis=0, keepdims=True)
        first = jnp.min(jnp.where(x == top, blk_f, float(n_sel)), axis=0, keepdims=True)
        x_buf[0:n_sel] = jnp.where(blk_f == first, -jnp.inf, x)
        return carry

    lax.fori_loop(0, jnp.where(qi > 0, n_top - n_forced, n_top), pick, 0)
    mb_ref[0, 0:n_sel, :] = jnp.where(x_buf[0:n_sel] == -jnp.inf, 0.0, NEG_INF).astype(BF16)
    if n_sel < mb_ref.shape[1]:
        mb_ref[0, n_sel:, :] = jnp.full((mb_ref.shape[1] - n_sel, tq), NEG_INF, BF16)


def _cmp_topk_kernel(*refs, n_top, n_tiles):
    nc = refs[1].shape[1]
    n_sel = refs[3].shape[0]
    qi = pl.program_id(2)
    for v in range(max(n_tiles // 2, 1)):
        frac = lambda total: min(total, (2 * v + 2) * total // n_tiles)
        pl.when((qi >> 1) == v)(functools.partial(_cmp_topk_body, *refs, n_top=n_top, nc=frac(nc), n_sel=frac(n_sel)))


def _cmp_topk(qn_t, kc, vct, ov, gates, b, g_kv, s_len, n_top):
    tq = min(CMP_TILE, s_len)
    sub = min(CMP_SUBTILE, tq)
    nq = s_len // tq
    nc = kc.shape[1]
    n_sel = ov.shape[0]
    rep = NSA_REP
    return pl.pallas_call(
        functools.partial(_cmp_topk_kernel, n_top=n_top, n_tiles=nq),
        grid=(b, g_kv, nq),
        in_specs=[
            pl.BlockSpec((rep * HEAD_DIM, tq), lambda bi, gi, qi: (gi, bi * nq + qi)),
            pl.BlockSpec((1, nc, HEAD_DIM), lambda bi, gi, qi: (bi * g_kv + gi, 0, 0)),
            pl.BlockSpec((1, HEAD_DIM, nc), lambda bi, gi, qi: (bi * g_kv + gi, 0, 0)),
            _resident(ov.shape),
            pl.BlockSpec((1, 1, rep, tq), lambda bi, gi, qi: (0, gi, 0, bi * nq + qi)),
        ],
        out_specs=[
            pl.BlockSpec((1, n_sel, tq), lambda bi, gi, qi: (bi * g_kv + gi, 0, qi)),
            pl.BlockSpec((rep * HEAD_DIM, tq), lambda bi, gi, qi: (gi, bi * nq + qi)),
        ],
        out_shape=[jax.ShapeDtypeStruct((b * g_kv, n_sel, s_len), BF16),
                   jax.ShapeDtypeStruct((NSA_Q_W, b * s_len), F32)],
        scratch_shapes=[pltpu.VMEM((nc, sub), F32), pltpu.VMEM((nc, sub), BF16), pltpu.VMEM((nc, sub), F32),
                        pltpu.VMEM((nc, tq), F32), pltpu.VMEM((n_sel, tq), F32)],
        compiler_params=_params(3),
        name="cmp_topk",
    )(qn_t, kc, vct, ov, gates)


def _softmax_tile(m, s):
    m_new = jnp.maximum(m, jnp.max(s, axis=0, keepdims=True))
    return m_new, jnp.exp2(m - m_new), jnp.exp2(s - m_new).astype(BF16)


def _normalize(acc):
    return acc[:HEAD_DIM] / acc[HEAD_DIM:HEAD_DIM + 1]


def _rel_pos(tk, tq):
    return lax.broadcasted_iota(jnp.int32, (tk, tq), 1) - lax.broadcasted_iota(jnp.int32, (tk, tq), 0)


def _window_bias(rel, back, tk, kj, window):
    dist = rel + (back * tk + jnp.where(kj >= 0, 0, window))
    return jnp.where((dist >= 0) & (dist < window), 0.0, NEG_INF)


def _pipe_stage(s_buf, p_buf, heads, cur=None, nxt=None, prv=None):
    if nxt is not None:
        s_next = [_dot(nxt[0], nxt[1][r][...]) for r in heads]
    if prv is not None:
        pv = [_dot(prv[0], p_buf[r]) for r in heads]
    out = None
    if cur is not None:
        m, bias = cur
        out = []
        for r in heads:
            s = s_buf[r] if bias is None else s_buf[r] + bias
            m_r, a_r, p_buf[r] = _softmax_tile(m[r], s)
            out.append((m_r, a_r))
        out = tuple(zip(*out))
    for r in heads:
        if prv is not None:
            prv[1][r] = prv[2][r] * prv[1][r] + pv[r]
        if nxt is not None:
            s_buf[r] = s_next[r]
    return out


def _nsa_kernel(q_ref, mb_ref, oc_ref, gate_ref, ks_ref, ex_ref, vs_ref, kw_ref, vw_ref, o_ref, *scratch):
    n_sub = q_ref.shape[1] // NSA_TILE
    per_tile = len(scratch) // n_sub
    for sub in range(n_sub):
        at = pl.ds(sub * NSA_TILE, NSA_TILE)
        _nsa_tile(pl.program_id(2) * n_sub + sub, q_ref.at[:, at], mb_ref.at[:, :, at], oc_ref.at[:, at],
                  gate_ref.at[:, :, :, at], ks_ref, ex_ref, vs_ref, kw_ref, vw_ref, o_ref.at[at, :],
                  *scratch[sub * per_tile:(sub + 1) * per_tile])


def _nsa_tile(qi, q_ref, mb_ref, oc_ref, gate_ref, ks_ref, ex_ref, vs_ref, kw_ref, vw_ref, o_ref,
              qa_buf, s_buf, p_buf, sel_acc, win_acc):
    tq = q_ref.shape[1]
    tk = tq
    n_sel = mb_ref.shape[1]
    rep = NSA_REP
    heads = range(rep)
    stage = functools.partial(_pipe_stage, s_buf, p_buf, heads)
    key_rows = lambda j: pl.ds(pl.multiple_of(j * tk, tk), tk)

    def sel_keys(j):
        return jnp.concatenate([ex_ref[key_rows(j), :], ks_ref[0, key_rows(j), :]], axis=1)

    q_heads = [q_ref.at[pl.ds(r * HEAD_DIM, HEAD_DIM), :] for r in heads]
    q_aug = [qa_buf.at[r] for r in heads]
    for r in heads:
        qa_buf[r, 0:n_sel, :] = mb_ref[0]
        qa_buf[r, n_sel:n_sel + HEAD_DIM, :] = q_heads[r][...]
        s_buf[r] = _dot(sel_keys(0), qa_buf[r])
        p_buf[r] = jnp.zeros((tk, tq), BF16)
        sel_acc[r] = jnp.zeros((VAL_ROWS, tq), F32)
        win_acc[r] = jnp.zeros((VAL_ROWS, tq), F32)
    row = lambda v: tuple(jnp.full((1, tq), v, F32) for _ in heads)

    def sel_stage(i, carry):
        m, alpha = carry
        return stage(cur=(m, None), nxt=(sel_keys(i + 1), q_aug),
                     prv=(vs_ref[jnp.maximum(i - 1, 0)], sel_acc, alpha))

    carry = lax.fori_loop(0, qi >> 1, lambda j, c: sel_stage(2 * j + 1, sel_stage(2 * j, c)),
                          (row(NEG_INF), row(1.0)))
    m, alpha = lax.fori_loop(0, qi & 1, lambda _, c: sel_stage(qi - 1, c), carry)

    rel = _rel_pos(tk, tq)
    backs = list(range(NSA_WINDOW // tk, -1, -1))
    win_tile = [jnp.maximum(qi - back, 0) for back in backs]
    win_bias = [_window_bias(rel, back, tk, qi - back, NSA_WINDOW) for back in backs]
    _, alpha_d = stage(cur=(m, jnp.where(rel >= 0, 0.0, NEG_INF)),
                       nxt=(kw_ref[0, key_rows(win_tile[0]), :], q_heads),
                       prv=(vs_ref[jnp.maximum(qi - 1, 0)], sel_acc, alpha))
    prv = (vs_ref[qi], sel_acc, alpha_d)
    m_w = row(NEG_INF)
    for n in range(len(backs)):
        nxt = (kw_ref[0, key_rows(win_tile[n + 1]), :], q_heads) if n + 1 < len(backs) else None
        m_w, alpha_w = stage(cur=(m_w, win_bias[n]), nxt=nxt, prv=prv)
        prv = (vw_ref[win_tile[n]], win_acc, alpha_w)
    stage(prv=prv)

    outs = [oc_ref[r * HEAD_DIM:(r + 1) * HEAD_DIM, :]
            + gate_ref[1, 0, r:r + 1, :] * _normalize(sel_acc[r])
            + gate_ref[2, 0, r:r + 1, :] * _normalize(win_acc[r]) for r in heads]
    o_ref[...] = jnp.concatenate(outs, axis=0).T.astype(o_ref.dtype)


def _nsa(qn_t, mb, oc_t, gates, ksel, expand, vsel, kwin, vwin, b, g_kv, s_len):
    tk = NSA_TILE
    tq = min(NSA_STEP, s_len)
    nq = s_len // tq
    rep = NSA_REP
    n_sel = mb.shape[1]
    keys = pl.BlockSpec((1, s_len, HEAD_DIM), lambda bi, gi, qi: (gi * b + bi, 0, 0))
    vals = pl.BlockSpec((s_len // tk, VAL_ROWS, tk), lambda bi, gi, qi: (bi, gi, 0))
    return pl.pallas_call(
        _nsa_kernel,
        grid=(b, g_kv, nq),
        in_specs=[
            pl.BlockSpec((rep * HEAD_DIM, tq), lambda bi, gi, qi: (gi, bi * nq + qi)),
            pl.BlockSpec((1, n_sel, tq), lambda bi, gi, qi: (bi * g_kv + gi, 0, qi)),
            pl.BlockSpec((rep * HEAD_DIM, tq), lambda bi, gi, qi: (gi, bi * nq + qi)),
            pl.BlockSpec((3, 1, rep, tq), lambda bi, gi, qi: (0, gi, 0, bi * nq + qi)),
            keys, _resident(expand.shape), vals, keys, vals,
        ],
        out_specs=pl.BlockSpec((tq, rep * HEAD_DIM), lambda bi, gi, qi: (bi * nq + qi, gi)),
        out_shape=jax.ShapeDtypeStruct((b * s_len, NSA_Q_W), BF16),
        scratch_shapes=[pltpu.VMEM((rep, n_sel + HEAD_DIM, tk), BF16), pltpu.VMEM((rep, tk, tk), F32),
                        pltpu.VMEM((rep, tk, tk), BF16), pltpu.VMEM((rep, VAL_ROWS, tk), F32),
                        pltpu.VMEM((rep, VAL_ROWS, tk), F32)] * (tq // tk),
        compiler_params=_params(3),
        name="nsa",
    )(qn_t, mb, oc_t, gates, ksel.reshape(g_kv * b, s_len, HEAD_DIM), expand, vsel,
      kwin.reshape(g_kv * b, s_len, HEAD_DIM), vwin)


def _swa_kernel(q_ref, sink_ref, k_ref, v_ref, o_ref, *scratch):
    n_sub = q_ref.shape[1] // SWA_TILE
    per_tile = len(scratch) // n_sub
    for sub in range(n_sub):
        at = pl.ds(sub * SWA_TILE, SWA_TILE)
        _swa_tile(pl.program_id(1) * n_sub + sub, q_ref.at[:, at], sink_ref, k_ref, v_ref, o_ref.at[at, :],
                  *scratch[sub * per_tile:(sub + 1) * per_tile])


def _swa_tile(qi, q_ref, sink_ref, k_ref, v_ref, o_ref, s_buf, p_buf, bias_buf, o_buf):
    tq = q_ref.shape[1]
    tv = v_ref.shape[2]
    nk = s_buf.shape[0]
    heads = SWA_HEADS
    q0 = qi * tq
    k0 = jnp.maximum(q0 - SWA_WINDOW, 0)

    keys = k_ref[0, pl.ds(pl.multiple_of(k0, tv), nk), :]
    vals = jnp.concatenate([v_ref[k0 // tv + j] for j in range(nk // tv)], axis=1)
    dist = _rel_pos(nk, tq) + (q0 - k0)
    bias_buf[...] = jnp.where((dist >= 0) & (dist < SWA_WINDOW), 0.0, NEG_INF)
    q_head = lambda h: q_ref[h * HEAD_DIM:(h + 1) * HEAD_DIM, :]

    def finish(h, m_all, sink):
        acc = _dot(vals, p_buf[...])
        o_buf[h * HEAD_DIM:(h + 1) * HEAD_DIM, :] = (
            acc[:HEAD_DIM] / (acc[HEAD_DIM:HEAD_DIM + 1] + jnp.exp2(sink - m_all)))

    s_buf[...] = _dot(keys, q_head(0))
    prev = None
    for h in range(heads):
        if h + 1 < heads:
            s_next = _dot(keys, q_head(h + 1))
        if prev is not None:
            finish(*prev)
        sink = sink_ref[:, h * tq:(h + 1) * tq]
        s = s_buf[...] + bias_buf[...]
        m_all = jnp.maximum(jnp.max(s, axis=0, keepdims=True), sink)
        p_buf[...] = jnp.exp2(s - m_all).astype(BF16)
        prev = (h, m_all, sink)
        if h + 1 < heads:
            s_buf[...] = s_next
    finish(*prev)
    o_ref[...] = o_buf[...].T.astype(o_ref.dtype)


def _swa(sq_t, sink_row, k, v, b, s_len):
    tile = min(SWA_TILE, s_len)
    tq = min(SWA_STEP, s_len)
    nq = s_len // tq
    tv = v.shape[2]
    nk = tile + SWA_WINDOW
    return pl.pallas_call(
        _swa_kernel,
        grid=(b, nq),
        in_specs=[pl.BlockSpec((SWA_Q_W, tq), lambda bi, qi: (0, bi * nq + qi)),
                  _resident(sink_row.shape),
                  pl.BlockSpec((1, s_len, HEAD_DIM), lambda bi, qi: (bi, 0, 0)),
                  pl.BlockSpec((s_len // tv, VAL_ROWS, tv), lambda bi, qi: (bi, 0, 0))],
        out_specs=pl.BlockSpec((tq, SWA_Q_W), lambda bi, qi: (bi * nq + qi, 0)),
        out_shape=jax.ShapeDtypeStruct((b * s_len, SWA_Q_W), BF16),
        scratch_shapes=[pltpu.VMEM((nk, tile), F32), pltpu.VMEM((nk, tile), BF16), pltpu.VMEM((nk, tile), F32),
                        pltpu.VMEM((SWA_Q_W, tile), F32)] * (tq // tile),
        compiler_params=_params(2),
        name="swa",
    )(sq_t, sink_row, k.reshape(b, s_len, HEAD_DIM), v)


def _overlap_matrix(n_sel, n_cmp, n_cmp_pad):
    cs = np.arange(n_cmp) * CMP_STRIDE
    ss = np.arange(n_sel) * SEL_LEN
    ov = np.clip(np.minimum(cs[None, :] + CMP_LEN, ss[:, None] + SEL_LEN)
                 - np.maximum(cs[None, :], ss[:, None]), 0, None).astype(np.float32) / CMP_LEN
    return np.pad(ov, ((0, 0), (0, n_cmp_pad - n_cmp)))


def _layer(h, positions, w, b, s_len, norm_final, last):
    n, d = h.shape
    g_kv = NSA_KV
    n_sel = s_len // SEL_LEN
    n_top = min(SEL_TOPN, n_sel)
    nc = s_len // CMP_STRIDE
    bf = lambda a: a.astype(BF16)

    h = _ffn(h, w['norm_ffn1'][None], bf(w['ffn1_gate']), bf(w['ffn1_up']), bf(w['ffn1_down']),
             w['norm_ffn1'][None], False)

    pts = np.cumsum((NSA_Q_W,) + (NSA_KV_W,) * 6 + (NSA_GATE_W, SWA_Q_W, SWA_KV_W, SWA_KV_W, d, d))[:-1]
    (w_nq, w_kc, w_vc, w_ksl, w_vsl, w_kwn, w_vwn, w_ng, w_sq, w_sk, w_sv, w_ga, w_gb) = jnp.split(
        bf(w['w_in']), pts, axis=1)
    parts = dict(qn=w_nq, sq=w_sq, ksel=w_ksl, kwin=w_kwn, kswa=w_sk, vsel=w_vsl, vwin=w_vwn, vswa=w_sv,
                 ng=w_ng, kc=w_kc, vc=w_vc)
    w_t = jnp.concatenate([jnp.pad(parts[name], ((0, 0), (0, width - parts[name].shape[1])))
                           for name, width in PROJ_ROWS], axis=1).T
    w_n = jnp.concatenate([w_ga, w_gb], axis=1)

    freq_col = (ROPE_THETA ** (-jnp.arange(HALF_DIM, dtype=F32) / HALF_DIM))[:, None]
    (qn_t, sq_t, ksel, kwin, kswa, vsel, vwin, vswa, ng_t, kcv, gab) = _proj(
        h, w['norm_mix'][None], w_t, w_n, positions.reshape(1, n), freq_col)

    pos_c = jnp.pad(positions[:, CMP_LEN - 1::CMP_STRIDE], ((0, 0), (0, 1)))[:, None, :]
    w1 = lambda a: bf(a.reshape(CMP_LEN, HEAD_DIM, a.shape[-1]))
    kc, vc_t = _compress(kcv, w['cmp_pe_k'], w['cmp_pe_v'], w1(w['cmp_k_w1']), w1(w['cmp_v_w1']),
                         bf(w['cmp_k_w2'].T), bf(w['cmp_v_w2'].T), pos_c, freq_col, b, g_kv, s_len)

    gates = ng_t.reshape(3, g_kv, NSA_REP, n)
    ov = jnp.asarray(_overlap_matrix(n_sel, nc - 1, nc), BF16)
    mb, oc_t = _cmp_topk(qn_t, kc, vc_t, ov, gates, b, g_kv, s_len, n_top)

    expand = jnp.asarray(np.arange(s_len)[:, None] // SEL_LEN == np.arange(n_sel)[None, :], BF16)
    o_a = _nsa(qn_t, mb, oc_t, gates, ksel, expand, vsel, kwin, vwin, b, g_kv, s_len)

    sink_row = jnp.repeat(w['swa_sinks'].astype(F32) * LOG2E, min(SWA_TILE, s_len))[None]
    o_b = _swa(sq_t, sink_row, kswa, vswa, b, s_len)

    branches = (o_a, o_b, gab, bf(w['w_branch_a']), bf(w['w_branch_b']), bf(w['w_out']))
    return _ffn(h, w['norm_ffn2'][None], bf(w['ffn2_gate']), bf(w['ffn2_up']), bf(w['ffn2_down']),
                norm_final[None], last, branches)


def kernel(x, positions, norm_ffn1, ffn1_gate, ffn1_up, ffn1_down, norm_mix, w_in, cmp_pe_k, cmp_k_w1, cmp_k_w2, cmp_pe_v, cmp_v_w1, cmp_v_w2, swa_sinks, w_branch_a, w_branch_b, w_out, norm_ffn2, ffn2_gate, ffn2_up, ffn2_down, norm_final):
    b, s_len, d = x.shape
    stacked = dict(norm_ffn1=norm_ffn1, ffn1_gate=ffn1_gate, ffn1_up=ffn1_up, ffn1_down=ffn1_down,
                   norm_mix=norm_mix, w_in=w_in, cmp_pe_k=cmp_pe_k, cmp_k_w1=cmp_k_w1, cmp_k_w2=cmp_k_w2,
                   cmp_pe_v=cmp_pe_v, cmp_v_w1=cmp_v_w1, cmp_v_w2=cmp_v_w2, swa_sinks=swa_sinks,
                   w_branch_a=w_branch_a, w_branch_b=w_branch_b, w_out=w_out,
                   norm_ffn2=norm_ffn2, ffn2_gate=ffn2_gate, ffn2_up=ffn2_up, ffn2_down=ffn2_down)
    depth = norm_ffn1.shape[0]
    h = x.reshape(b * s_len, d)
    for i in range(depth):
        w = {k: v[i] for k, v in stacked.items()}
        h = _layer(h, positions, w, b, s_len, norm_final, i == depth - 1)
    return h.reshape(b, s_len, d)
```

```python
import functools

import numpy as np
import jax
import jax.numpy as jnp
from jax import lax
from jax.experimental import pallas as pl
from jax.experimental.pallas import tpu as pltpu

HEAD_DIM = 64
HALF_DIM = HEAD_DIM // 2
NSA_HEADS = 8
NSA_KV = 2
NSA_REP = NSA_HEADS // NSA_KV
SWA_HEADS = 8
CMP_STRIDE = 16
CMP_SHIFT = 4
CMP_LEN = 2 * CMP_STRIDE
SEL_LEN = 64
SEL_SHIFT = 6
SEL_TOPN = 16
SEL_LOCAL = 2
NSA_WINDOW = 512
SWA_WINDOW = 128
ROPE_THETA = 10000.0
RMS_EPS = 1e-6
FFN_HALF = 0.5
NEG_INF = -1e30
FORCE = 1e9
LOG2E = 1.4426950408889634
Q_SCALE = HEAD_DIM ** -0.5 * LOG2E
BF16_ROWS = 16
VAL_ROWS = HEAD_DIM + BF16_ROWS
LANES = 128

NSA_Q_W = NSA_HEADS * HEAD_DIM
NSA_KV_W = NSA_KV * HEAD_DIM
NSA_GATE_W = 3 * NSA_HEADS
SWA_Q_W = SWA_HEADS * HEAD_DIM
SWA_KV_W = HEAD_DIM

VMEM_LIMIT_BYTES = 56 * 1024 * 1024

BF16 = jnp.bfloat16
F32 = jnp.float32

ROW_TILE = 512
FFN_CHUNK = 256
PROJ_TILE = 1024
PROJ_CHUNK = 512
CMP_TILE = 1024
CMP_SUBTILE = 256
NSA_TILE = 256
NSA_STEP = 2048
SWA_TILE = 256
SWA_STEP = 1024
SWA_VTILE = 128


def _params(n_axes):
    return pltpu.CompilerParams(dimension_semantics=("arbitrary",) * n_axes,
                                vmem_limit_bytes=VMEM_LIMIT_BYTES)


def _resident(shape):
    zeros = (0,) * len(shape)
    return pl.BlockSpec(shape, lambda *_: zeros, pipeline_mode=pl.Buffered(1))


def _rms(x, g):
    y = x * lax.rsqrt(jnp.mean(x * x, axis=-1, keepdims=True) + RMS_EPS)
    return y * g


def _dot(a, b):
    return jnp.dot(a, b, preferred_element_type=F32)


def _dot_nt(a, b):
    return lax.dot_general(a, b, (((1,), (1,)), ((), ())), preferred_element_type=F32)


def _rope_angles(pos_row, freq_col):
    ang = pos_row.astype(F32) * freq_col
    return jnp.cos(ang), jnp.sin(ang)


def _rope_rows(block, cos_t, sin_t):
    out = []
    for hd in range(block.shape[0] // HEAD_DIM):
        x1 = block[hd * HEAD_DIM:hd * HEAD_DIM + HALF_DIM]
        x2 = block[hd * HEAD_DIM + HALF_DIM:(hd + 1) * HEAD_DIM]
        out += [x1 * cos_t - x2 * sin_t, x2 * cos_t + x1 * sin_t]
    return out


def _ones_rows(tk):
    return jnp.where(lax.broadcasted_iota(jnp.int32, (BF16_ROWS, tk), 0) == 0, 1.0, 0.0).astype(BF16)


def _merge_branches(h_ref, oa_ref, ob_ref, gab_ref, wa_ref, wb_ref, wo_ref):
    d = h_ref.shape[1]
    gab = gab_ref[...]
    merged = gab[:, :d] * _dot(oa_ref[...], wa_ref[...]) + gab[:, d:] * _dot(ob_ref[...], wb_ref[...])
    return h_ref[...] + _dot(merged.astype(BF16), wo_ref[...])


def _ffn_kernel(*refs, final_norm, merge):
    x_ref, g_ref, wg_ref, wu_ref, wd_ref, gf_ref, o_ref = refs[-7:]
    chunk = x_ref.shape[0] if merge else FFN_CHUNK
    for part in range(x_ref.shape[0] // chunk):
        at = pl.ds(part * chunk, chunk)
        x = _merge_branches(x_ref, *refs[:-7]) if merge else x_ref[at, :]
        xb = _rms(x, g_ref[...]).astype(BF16)
        a = _dot(xb, wg_ref[...])
        b = _dot(xb, wu_ref[...])
        t = (a * jax.nn.sigmoid(a)) * b
        h = x + FFN_HALF * _dot(t.astype(BF16), wd_ref[...])
        if final_norm:
            h = _rms(h, gf_ref[...])
        o_ref[at, :] = h


def _ffn(x2, g, wg, wu, wd, gf, final_norm, branches=None):
    n, d = x2.shape
    f = wg.shape[1]
    tm = ROW_TILE
    rows = lambda w: pl.BlockSpec((tm, w), lambda i: (i, 0))
    merge_specs, merge_args = [], ()
    if branches is not None:
        oa, ob, gab, wa, wb, wo = branches
        merge_specs = [rows(oa.shape[1]), rows(ob.shape[1]), rows(2 * d),
                       _resident(wa.shape), _resident(wb.shape), _resident(wo.shape)]
        merge_args = branches
    return pl.pallas_call(
        functools.partial(_ffn_kernel, final_norm=final_norm, merge=branches is not None),
        grid=(n // tm,),
        in_specs=merge_specs + [rows(d), _resident((1, d)), _resident((d, f)), _resident((d, f)),
                                _resident((f, d)), _resident((1, d))],
        out_specs=rows(d),
        out_shape=jax.ShapeDtypeStruct((n, d), F32),
        compiler_params=_params(1),
        name="ffn_final" if final_norm else "ffn",
    )(*merge_args, x2, g, wg, wu, wd, gf)


PROJ_ROWS = (('qn', NSA_Q_W), ('sq', SWA_Q_W), ('ksel', NSA_KV_W), ('kwin', NSA_KV_W), ('kswa', LANES),
             ('vsel', NSA_KV_W), ('vwin', NSA_KV_W), ('vswa', SWA_KV_W), ('ng', 32),
             ('kc', NSA_KV_W), ('vc', NSA_KV_W))


def _proj_kernel(h_ref, g_ref, wt_ref, wn_ref, pos_ref, freq_ref,
                 qn_ref, sq_ref, ksel_ref, kwin_ref, kswa_ref, vsel_ref, vwin_ref, vswa_ref,
                 ng_ref, kcv_ref, gab_ref):
    for part in range(h_ref.shape[0] // PROJ_CHUNK):
        at = pl.ds(part * PROJ_CHUNK, PROJ_CHUNK)
        tiles = lambda ref: ref.at[pl.ds(part * (PROJ_CHUNK // ref.shape[2]), PROJ_CHUNK // ref.shape[2])]
        _proj_chunk(h_ref.at[at, :], g_ref, wt_ref, wn_ref, pos_ref.at[:, at], freq_ref,
                    qn_ref.at[:, at], sq_ref.at[:, at], ksel_ref.at[:, at, :], kwin_ref.at[:, at, :],
                    kswa_ref.at[at, :], tiles(vsel_ref), tiles(vwin_ref), tiles(vswa_ref),
                    ng_ref.at[:, at], kcv_ref.at[:, at, :], gab_ref.at[at, :])


def _proj_chunk(h_ref, g_ref, wt_ref, wn_ref, pos_ref, freq_ref,
                qn_ref, sq_ref, ksel_ref, kwin_ref, kswa_ref, vsel_ref, vwin_ref, vswa_ref,
                ng_ref, kcv_ref, gab_ref):
    ub = _rms(h_ref[...], g_ref[...]).astype(BF16)
    yt = _dot_nt(wt_ref[...], ub)
    rows, o = {}, 0
    for name, width in PROJ_ROWS:
        rows[name] = yt[o:o + width]
        o += width
    cos_t, sin_t = _rope_angles(pos_ref[...], freq_ref[...])

    for name, ref in (('qn', qn_ref), ('sq', sq_ref)):
        for i, piece in enumerate(_rope_rows(rows[name], cos_t, sin_t)):
            ref[i * HALF_DIM:(i + 1) * HALF_DIM, :] = (piece * Q_SCALE).astype(BF16)

    for name, ref in (('ksel', ksel_ref), ('kwin', kwin_ref)):
        k_nat = jnp.concatenate(_rope_rows(rows[name], cos_t, sin_t), axis=0).T
        for g in range(NSA_KV):
            ref[g] = k_nat[:, g * HEAD_DIM:(g + 1) * HEAD_DIM].astype(BF16)
    kswa = _rope_rows(rows['kswa'][:HEAD_DIM], cos_t, sin_t) + [rows['kswa'][HEAD_DIM:]]
    kswa_ref[...] = jnp.concatenate(kswa, axis=0).T[:, :HEAD_DIM].astype(BF16)

    for name, ref, groups in (('vsel', vsel_ref, NSA_KV), ('vwin', vwin_ref, NSA_KV), ('vswa', vswa_ref, 1)):
        tk = ref.shape[2]
        for j in range(ref.shape[0]):
            for g in range(groups):
                ref[j, g * VAL_ROWS:g * VAL_ROWS + HEAD_DIM, :] = (
                    rows[name][g * HEAD_DIM:(g + 1) * HEAD_DIM, j * tk:(j + 1) * tk].astype(BF16))
                ref[j, g * VAL_ROWS + HEAD_DIM:(g + 1) * VAL_ROWS, :] = _ones_rows(tk)

    ng_ref[...] = jax.nn.sigmoid(rows['ng'][:NSA_GATE_W])

    for i, name in enumerate(('kc', 'vc')):
        nat = rows[name].T
        for g in range(NSA_KV):
            kcv_ref[i * NSA_KV + g] = nat[:, g * HEAD_DIM:(g + 1) * HEAD_DIM]

    gab_ref[...] = jax.nn.sigmoid(_dot(ub, wn_ref[...]))


def _proj(h2, g, wt, wn, pos_row, freq_col):
    n, d = h2.shape
    tm = PROJ_TILE
    rows = lambda w: pl.BlockSpec((tm, w), lambda i: (i, 0))
    cols = lambda w: pl.BlockSpec((w, tm), lambda i: (0, i))
    grouped = lambda k: pl.BlockSpec((k, tm, HEAD_DIM), lambda i: (0, i, 0))
    tiles = lambda groups, tk: pl.BlockSpec((tm // tk, groups * VAL_ROWS, tk), lambda i: (i, 0, 0))
    val_shape = lambda groups, tk: jax.ShapeDtypeStruct((n // tk, groups * VAL_ROWS, tk), BF16)
    out_shape = [
        jax.ShapeDtypeStruct((NSA_Q_W, n), BF16), jax.ShapeDtypeStruct((SWA_Q_W, n), BF16),
        jax.ShapeDtypeStruct((NSA_KV, n, HEAD_DIM), BF16), jax.ShapeDtypeStruct((NSA_KV, n, HEAD_DIM), BF16),
        jax.ShapeDtypeStruct((n, HEAD_DIM), BF16),
        val_shape(NSA_KV, NSA_TILE), val_shape(NSA_KV, NSA_TILE), val_shape(1, SWA_VTILE),
        jax.ShapeDtypeStruct((NSA_GATE_W, n), F32), jax.ShapeDtypeStruct((2 * NSA_KV, n, HEAD_DIM), F32),
        jax.ShapeDtypeStruct((n, 2 * d), F32),
    ]
    return pl.pallas_call(
        _proj_kernel,
        grid=(n // tm,),
        in_specs=[rows(d), _resident((1, d)), _resident(wt.shape), _resident(wn.shape),
                  cols(1), _resident(freq_col.shape)],
        out_specs=[cols(NSA_Q_W), cols(SWA_Q_W), grouped(NSA_KV), grouped(NSA_KV), rows(HEAD_DIM),
                   tiles(NSA_KV, NSA_TILE), tiles(NSA_KV, NSA_TILE), tiles(1, SWA_VTILE),
                   cols(NSA_GATE_W), grouped(2 * NSA_KV), rows(2 * d)],
        out_shape=out_shape,
        compiler_params=_params(1),
        name="proj",
    )(h2, g, wt, wn, pos_row, freq_col)


def _gelu(x):
    return jax.nn.gelu(x, approximate=True)


def _compress_kernel(kc_ref, vc_ref, pek_ref, pev_ref, w1k_ref, w1v_ref, w2kt_ref, w2vt_ref,
                     pos_ref, freq_ref, kc_out, vct_out):
    nc = kc_out.shape[1]

    def hidden(x_ref, pe_ref, w1_ref):
        top = bot = None
        for j in range(CMP_STRIDE):
            x = x_ref[0, pl.ds(j, nc, stride=CMP_STRIDE), :]
            t = _dot((x + pe_ref[j:j + 1, :]).astype(BF16), w1_ref[j])
            b = _dot((x + pe_ref[CMP_STRIDE + j:CMP_STRIDE + j + 1, :]).astype(BF16), w1_ref[CMP_STRIDE + j])
            top, bot = (t, b) if top is None else (top + t, bot + b)
        return _gelu(top + pltpu.roll(bot, shift=nc - 1, axis=0)).astype(BF16)

    kt = _dot_nt(w2kt_ref[...], hidden(kc_ref, pek_ref, w1k_ref))
    cos_t, sin_t = _rope_angles(pos_ref[0], freq_ref[...])
    kt = jnp.concatenate(_rope_rows(kt, cos_t, sin_t) + [jnp.zeros((LANES - HEAD_DIM, nc), F32)], axis=0)
    kc_out[0] = kt.T[:, :HEAD_DIM].astype(BF16)
    vct_out[0] = _dot_nt(w2vt_ref[...], hidden(vc_ref, pev_ref, w1v_ref)).astype(BF16)


def _compress(kcv, pek, pev, w1k, w1v, w2kt, w2vt, pos_c, freq_col, b, g_kv, s_len):
    nc = s_len // CMP_STRIDE
    hid = w1k.shape[-1]
    kcv = kcv.reshape(2 * g_kv * b, s_len, HEAD_DIM)
    src = lambda kind: pl.BlockSpec((1, s_len, HEAD_DIM),
                                    lambda i: ((kind * g_kv + i % g_kv) * b + i // g_kv, 0, 0))
    per = lambda *s: pl.BlockSpec((1,) + s, lambda i: (i,) + (0,) * len(s))
    return pl.pallas_call(
        _compress_kernel,
        grid=(b * g_kv,),
        in_specs=[src(0), src(1), _resident(pek.shape), _resident(pev.shape),
                  _resident(w1k.shape), _resident(w1v.shape), _resident((HEAD_DIM, hid)),
                  _resident((HEAD_DIM, hid)),
                  pl.BlockSpec((1, 1, nc), lambda i: (i // g_kv, 0, 0)), _resident(freq_col.shape)],
        out_specs=[per(nc, HEAD_DIM), per(HEAD_DIM, nc)],
        out_shape=[jax.ShapeDtypeStruct((b * g_kv, nc, HEAD_DIM), BF16),
                   jax.ShapeDtypeStruct((b * g_kv, HEAD_DIM, nc), BF16)],
        compiler_params=_params(1),
        name="compress",
    )(kcv, kcv, pek, pev, w1k, w1v, w2kt, w2vt, pos_c, freq_col)


def _cmp_topk_body(q_ref, kc_ref, vct_ref, ov_ref, gate_ref, mb_ref, oc_ref,
                   s_buf, p_buf, bias_buf, psum_buf, x_buf, *, n_top, nc, n_sel):
    tq = q_ref.shape[1]
    sub = s_buf.shape[1]
    rep = NSA_REP
    qi = pl.program_id(2)
    q0 = qi * tq
    stages = [(u, r) for u in range(tq // sub) for r in range(rep)]
    q_cols = lambda u, r: q_ref[r * HEAD_DIM:(r + 1) * HEAD_DIM, u * sub:(u + 1) * sub]
    keys = kc_ref[0, 0:nc, :]
    vals = vct_ref[0, :, 0:nc]

    s_buf[0:nc] = _dot(keys, q_cols(*stages[0]))
    inv_prev = None
    for n, (u, r) in enumerate(stages):
        cols = slice(u * sub, (u + 1) * sub)
        if n + 1 < len(stages):
            s_next = _dot(keys, q_cols(*stages[n + 1]))
        if n > 0:
            pu, pr = stages[n - 1]
            oc_ref[pr * HEAD_DIM:(pr + 1) * HEAD_DIM, pu * sub:(pu + 1) * sub] = (
                _dot(vals, p_buf[0:nc]) * (inv_prev * gate_ref[0, 0, pr:pr + 1, pu * sub:(pu + 1) * sub]))
        last = ((q0 + u * sub + lax.broadcasted_iota(jnp.int32, (1, sub), 1)) - (CMP_LEN - 1)) >> CMP_SHIFT
        if r == 0:
            bias_buf[0:nc] = jnp.where(lax.broadcasted_iota(jnp.int32, (nc, sub), 0) <= last, 0.0, NEG_INF)
        s = s_buf[0:nc] + bias_buf[0:nc]
        m = jnp.max(s, axis=0, keepdims=True)
        e = jnp.exp2(s - m)
        inv_prev = jnp.where(last >= 0, 1.0 / jnp.sum(e, axis=0, keepdims=True), 0.0)
        p_buf[0:nc] = e.astype(BF16)
        if r == 0:
            psum_buf[0:nc, cols] = e * inv_prev
        else:
            psum_buf[0:nc, cols] += e * inv_prev
        if n + 1 < len(stages):
            s_buf[0:nc] = s_next
    pu, pr = stages[-1]
    oc_ref[pr * HEAD_DIM:(pr + 1) * HEAD_DIM, pu * sub:(pu + 1) * sub] = (
        _dot(vals, p_buf[0:nc]) * (inv_prev * gate_ref[0, 0, pr:pr + 1, pu * sub:(pu + 1) * sub]))

    psum = psum_buf[0:nc, :]
    p_hi = psum.astype(BF16)
    p_lo = (psum - p_hi.astype(F32)).astype(BF16)
    pool = ov_ref[0:n_sel, 0:nc]
    imp = _dot(pool, p_hi) + _dot(pool, p_lo)

    blk = lax.broadcasted_iota(jnp.int32, (n_sel, tq), 0)
    tb = (q0 + lax.broadcasted_iota(jnp.int32, (n_sel, tq), 1)) >> SEL_SHIFT
    forced = (blk == 0) | ((tb - blk >= 0) & (tb - blk < SEL_LOCAL))
    n_forced = 1 + SEL_LOCAL
    premark = forced & (jnp.full((n_sel, tq), qi, jnp.int32) > 0)
    x_buf[0:n_sel] = jnp.where(premark, -jnp.inf, jnp.where(forced, FORCE, jnp.where(blk > tb, -FORCE, imp)))

    blk_f = blk.astype(F32)

    def pick(_, carry):
        x = x_buf[0:n_sel]
        top = jnp.max(x, axis=0, keepdims=True)
        first = jnp.min(jnp.where(x == top, blk_f, float(n_sel)), axis=0, keepdims=True)
        x_buf[0:n_sel] = jnp.where(blk_f == first, -jnp.inf, x)
        return carry

    lax.fori_loop(0, jnp.where(qi > 0, n_top - n_forced, n_top), pick, 0)
    mb_ref[0, 0:n_sel, :] = jnp.where(x_buf[0:n_sel] == -jnp.inf, 0.0, NEG_INF).astype(BF16)
    if n_sel < mb_ref.shape[1]:
        mb_ref[0, n_sel:, :] = jnp.full((mb_ref.shape[1] - n_sel, tq), NEG_INF, BF16)


def _cmp_topk_kernel(*refs, n_top, n_tiles):
    nc = refs[1].shape[1]
    n_sel = refs[3].shape[0]
    qi = pl.program_id(2)
    for v in range(max(n_tiles // 2, 1)):
        frac = lambda total: min(total, (2 * v + 2) * total // n_tiles)
        pl.when((qi >> 1) == v)(functools.partial(_cmp_topk_body, *refs, n_top=n_top, nc=frac(nc), n_sel=frac(n_sel)))


def _cmp_topk(qn_t, kc, vct, ov, gates, b, g_kv, s_len, n_top):
    tq = min(CMP_TILE, s_len)
    sub = min(CMP_SUBTILE, tq)
    nq = s_len // tq
    nc = kc.shape[1]
    n_sel = ov.shape[0]
    rep = NSA_REP
    return pl.pallas_call(
        functools.partial(_cmp_topk_kernel, n_top=n_top, n_tiles=nq),
        grid=(b, g_kv, nq),
        in_specs=[
            pl.BlockSpec((rep * HEAD_DIM, tq), lambda bi, gi, qi: (gi, bi * nq + qi)),
            pl.BlockSpec((1, nc, HEAD_DIM), lambda bi, gi, qi: (bi * g_kv + gi, 0, 0)),
            pl.BlockSpec((1, HEAD_DIM, nc), lambda bi, gi, qi: (bi * g_kv + gi, 0, 0)),
            _resident(ov.shape),
            pl.BlockSpec((1, 1, rep, tq), lambda bi, gi, qi: (0, gi, 0, bi * nq + qi)),
        ],
        out_specs=[
            pl.BlockSpec((1, n_sel, tq), lambda bi, gi, qi: (bi * g_kv + gi, 0, qi)),
            pl.BlockSpec((rep * HEAD_DIM, tq), lambda bi, gi, qi: (gi, bi * nq + qi)),
        ],
        out_shape=[jax.ShapeDtypeStruct((b * g_kv, n_sel, s_len), BF16),
                   jax.ShapeDtypeStruct((NSA_Q_W, b * s_len), F32)],
        scratch_shapes=[pltpu.VMEM((nc, sub), F32), pltpu.VMEM((nc, sub), BF16), pltpu.VMEM((nc, sub), F32),
                        pltpu.VMEM((nc, tq), F32), pltpu.VMEM((n_sel, tq), F32)],
        compiler_params=_params(3),
        name="cmp_topk",
    )(qn_t, kc, vct, ov, gates)


def _softmax_tile(m, s):
    m_new = jnp.maximum(m, jnp.max(s, axis=0, keepdims=True))
    return m_new, jnp.exp2(m - m_new), jnp.exp2(s - m_new).astype(BF16)


def _normalize(acc):
    return acc[:HEAD_DIM] / acc[HEAD_DIM:HEAD_DIM + 1]


def _rel_pos(tk, tq):
    return lax.broadcasted_iota(jnp.int32, (tk, tq), 1) - lax.broadcasted_iota(jnp.int32, (tk, tq), 0)


def _window_bias(rel, back, tk, kj, window):
    dist = rel + (back * tk + jnp.where(kj >= 0, 0, window))
    return jnp.where((dist >= 0) & (dist < window), 0.0, NEG_INF)


def _pipe_stage(s_buf, p_buf, heads, cur=None, nxt=None, prv=None):
    if nxt is not None:
        s_next = [_dot(nxt[0], nxt[1][r][...]) for r in heads]
    if prv is not None:
        pv = [_dot(prv[0], p_buf[r]) for r in heads]
    out = None
    if cur is not None:
        m, bias = cur
        out = []
        for r in heads:
            s = s_buf[r] if bias is None else s_buf[r] + bias
            m_r, a_r, p_buf[r] = _softmax_tile(m[r], s)
            out.append((m_r, a_r))
        out = tuple(zip(*out))
    for r in heads:
        if prv is not None:
            prv[1][r] = prv[2][r] * prv[1][r] + pv[r]
        if nxt is not None:
            s_buf[r] = s_next[r]
    return out


def _nsa_kernel(q_ref, mb_ref, oc_ref, gate_ref, ks_ref, ex_ref, vs_ref, kw_ref, vw_ref, o_ref, *scratch):
    n_sub = q_ref.shape[1] // NSA_TILE
    per_tile = len(scratch) // n_sub
    for sub in range(n_sub):
        at = pl.ds(sub * NSA_TILE, NSA_TILE)
        _nsa_tile(pl.program_id(2) * n_sub + sub, q_ref.at[:, at], mb_ref.at[:, :, at], oc_ref.at[:, at],
                  gate_ref.at[:, :, :, at], ks_ref, ex_ref, vs_ref, kw_ref, vw_ref, o_ref.at[at, :],
                  *scratch[sub * per_tile:(sub + 1) * per_tile])


def _nsa_tile(qi, q_ref, mb_ref, oc_ref, gate_ref, ks_ref, ex_ref, vs_ref, kw_ref, vw_ref, o_ref,
              qa_buf, s_buf, p_buf, sel_acc, win_acc):
    tq = q_ref.shape[1]
    tk = tq
    n_sel = mb_ref.shape[1]
    rep = NSA_REP
    heads = range(rep)
    stage = functools.partial(_pipe_stage, s_buf, p_buf, heads)
    key_rows = lambda j: pl.ds(pl.multiple_of(j * tk, tk), tk)

    def sel_keys(j):
        return jnp.concatenate([ex_ref[key_rows(j), :], ks_ref[0, key_rows(j), :]], axis=1)

    q_heads = [q_ref.at[pl.ds(r * HEAD_DIM, HEAD_DIM), :] for r in heads]
    q_aug = [qa_buf.at[r] for r in heads]
    for r in heads:
        qa_buf[r, 0:n_sel, :] = mb_ref[0]
        qa_buf[r, n_sel:n_sel + HEAD_DIM, :] = q_heads[r][...]
        s_buf[r] = _dot(sel_keys(0), qa_buf[r])
        p_buf[r] = jnp.zeros((tk, tq), BF16)
        sel_acc[r] = jnp.zeros((VAL_ROWS, tq), F32)
        win_acc[r] = jnp.zeros((VAL_ROWS, tq), F32)
    row = lambda v: tuple(jnp.full((1, tq), v, F32) for _ in heads)

    def sel_stage(i, carry):
        m, alpha = carry
        return stage(cur=(m, None), nxt=(sel_keys(i + 1), q_aug),
                     prv=(vs_ref[jnp.maximum(i - 1, 0)], sel_acc, alpha))

    carry = lax.fori_loop(0, qi >> 1, lambda j, c: sel_stage(2 * j + 1, sel_stage(2 * j, c)),
                          (row(NEG_INF), row(1.0)))
    m, alpha = lax.fori_loop(0, qi & 1, lambda _, c: sel_stage(qi - 1, c), carry)

    rel = _rel_pos(tk, tq)
    backs = list(range(NSA_WINDOW // tk, -1, -1))
    win_tile = [jnp.maximum(qi - back, 0) for back in backs]
    win_bias = [_window_bias(rel, back, tk, qi - back, NSA_WINDOW) for back in backs]
    _, alpha_d = stage(cur=(m, jnp.where(rel >= 0, 0.0, NEG_INF)),
                       nxt=(kw_ref[0, key_rows(win_tile[0]), :], q_heads),
                       prv=(vs_ref[jnp.maximum(qi - 1, 0)], sel_acc, alpha))
    prv = (vs_ref[qi], sel_acc, alpha_d)
    m_w = row(NEG_INF)
    for n in range(len(backs)):
        nxt = (kw_ref[0, key_rows(win_tile[n + 1]), :], q_heads) if n + 1 < len(backs) else None
        m_w, alpha_w = stage(cur=(m_w, win_bias[n]), nxt=nxt, prv=prv)
        prv = (vw_ref[win_tile[n]], win_acc, alpha_w)
    stage(prv=prv)

    outs = [oc_ref[r * HEAD_DIM:(r + 1) * HEAD_DIM, :]
            + gate_ref[1, 0, r:r + 1, :] * _normalize(sel_acc[r])
            + gate_ref[2, 0, r:r + 1, :] * _normalize(win_acc[r]) for r in heads]
    o_ref[...] = jnp.concatenate(outs, axis=0).T.astype(o_ref.dtype)


def _nsa(qn_t, mb, oc_t, gates, ksel, expand, vsel, kwin, vwin, b, g_kv, s_len):
    tk = NSA_TILE
    tq = min(NSA_STEP, s_len)
    nq = s_len // tq
    rep = NSA_REP
    n_sel = mb.shape[1]
    keys = pl.BlockSpec((1, s_len, HEAD_DIM), lambda bi, gi, qi: (gi * b + bi, 0, 0))
    vals = pl.BlockSpec((s_len // tk, VAL_ROWS, tk), lambda bi, gi, qi: (bi, gi, 0))
    return pl.pallas_call(
        _nsa_kernel,
        grid=(b, g_kv, nq),
        in_specs=[
            pl.BlockSpec((rep * HEAD_DIM, tq), lambda bi, gi, qi: (gi, bi * nq + qi)),
            pl.BlockSpec((1, n_sel, tq), lambda bi, gi, qi: (bi * g_kv + gi, 0, qi)),
            pl.BlockSpec((rep * HEAD_DIM, tq), lambda bi, gi, qi: (gi, bi * nq + qi)),
            pl.BlockSpec((3, 1, rep, tq), lambda bi, gi, qi: (0, gi, 0, bi * nq + qi)),
            keys, _resident(expand.shape), vals, keys, vals,
        ],
        out_specs=pl.BlockSpec((tq, rep * HEAD_DIM), lambda bi, gi, qi: (bi * nq + qi, gi)),
        out_shape=jax.ShapeDtypeStruct((b * s_len, NSA_Q_W), BF16),
        scratch_shapes=[pltpu.VMEM((rep, n_sel + HEAD_DIM, tk), BF16), pltpu.VMEM((rep, tk, tk), F32),
                        pltpu.VMEM((rep, tk, tk), BF16), pltpu.VMEM((rep, VAL_ROWS, tk), F32),
                        pltpu.VMEM((rep, VAL_ROWS, tk), F32)] * (tq // tk),
        compiler_params=_params(3),
        name="nsa",
    )(qn_t, mb, oc_t, gates, ksel.reshape(g_kv * b, s_len, HEAD_DIM), expand, vsel,
      kwin.reshape(g_kv * b, s_len, HEAD_DIM), vwin)


def _swa_kernel(q_ref, sink_ref, k_ref, v_ref, o_ref, *scratch):
    n_sub = q_ref.shape[1] // SWA_TILE
    per_tile = len(scratch) // n_sub
    for sub in range(n_sub):
        at = pl.ds(sub * SWA_TILE, SWA_TILE)
        _swa_tile(pl.program_id(1) * n_sub + sub, q_ref.at[:, at], sink_ref, k_ref, v_ref, o_ref.at[at, :],
                  *scratch[sub * per_tile:(sub + 1) * per_tile])


def _swa_tile(qi, q_ref, sink_ref, k_ref, v_ref, o_ref, s_buf, p_buf, bias_buf, o_buf):
    tq = q_ref.shape[1]
    tv = v_ref.shape[2]
    nk = s_buf.shape[0]
    heads = SWA_HEADS
    q0 = qi * tq
    k0 = jnp.maximum(q0 - SWA_WINDOW, 0)

    keys = k_ref[0, pl.ds(pl.multiple_of(k0, tv), nk), :]
    vals = jnp.concatenate([v_ref[k0 // tv + j] for j in range(nk // tv)], axis=1)
    dist = _rel_pos(nk, tq) + (q0 - k0)
    bias_buf[...] = jnp.where((dist >= 0) & (dist < SWA_WINDOW), 0.0, NEG_INF)
    q_head = lambda h: q_ref[h * HEAD_DIM:(h + 1) * HEAD_DIM, :]

    def finish(h, m_all, sink):
        acc = _dot(vals, p_buf[...])
        o_buf[h * HEAD_DIM:(h + 1) * HEAD_DIM, :] = (
            acc[:HEAD_DIM] / (acc[HEAD_DIM:HEAD_DIM + 1] + jnp.exp2(sink - m_all)))

    s_buf[...] = _dot(keys, q_head(0))
    prev = None
    for h in range(heads):
        if h + 1 < heads:
            s_next = _dot(keys, q_head(h + 1))
        if prev is not None:
            finish(*prev)
        sink = sink_ref[:, h * tq:(h + 1) * tq]
        s = s_buf[...] + bias_buf[...]
        m_all = jnp.maximum(jnp.max(s, axis=0, keepdims=True), sink)
        p_buf[...] = jnp.exp2(s - m_all).astype(BF16)
        prev = (h, m_all, sink)
        if h + 1 < heads:
            s_buf[...] = s_next
    finish(*prev)
    o_ref[...] = o_buf[...].T.astype(o_ref.dtype)


def _swa(sq_t, sink_row, k, v, b, s_len):
    tile = min(SWA_TILE, s_len)
    tq = min(SWA_STEP, s_len)
    nq = s_len // tq
    tv = v.shape[2]
    nk = tile + SWA_WINDOW
    return pl.pallas_call(
        _swa_kernel,
        grid=(b, nq),
        in_specs=[pl.BlockSpec((SWA_Q_W, tq), lambda bi, qi: (0, bi * nq + qi)),
                  _resident(sink_row.shape),
                  pl.BlockSpec((1, s_len, HEAD_DIM), lambda bi, qi: (bi, 0, 0)),
                  pl.BlockSpec((s_len // tv, VAL_ROWS, tv), lambda bi, qi: (bi, 0, 0))],
        out_specs=pl.BlockSpec((tq, SWA_Q_W), lambda bi, qi: (bi * nq + qi, 0)),
        out_shape=jax.ShapeDtypeStruct((b * s_len, SWA_Q_W), BF16),
        scratch_shapes=[pltpu.VMEM((nk, tile), F32), pltpu.VMEM((nk, tile), BF16), pltpu.VMEM((nk, tile), F32),
                        pltpu.VMEM((SWA_Q_W, tile), F32)] * (tq // tile),
        compiler_params=_params(2),
        name="swa",
    )(sq_t, sink_row, k.reshape(b, s_len, HEAD_DIM), v)


def _overlap_matrix(n_sel, n_cmp, n_cmp_pad):
    cs = np.arange(n_cmp) * CMP_STRIDE
    ss = np.arange(n_sel) * SEL_LEN
    ov = np.clip(np.minimum(cs[None, :] + CMP_LEN, ss[:, None] + SEL_LEN)
                 - np.maximum(cs[None, :], ss[:, None]), 0, None).astype(np.float32) / CMP_LEN
    return np.pad(ov, ((0, 0), (0, n_cmp_pad - n_cmp)))


def _layer(h, positions, w, b, s_len, norm_final, last):
    n, d = h.shape
    g_kv = NSA_KV
    n_sel = s_len // SEL_LEN
    n_top = min(SEL_TOPN, n_sel)
    nc = s_len // CMP_STRIDE
    bf = lambda a: a.astype(BF16)

    h = _ffn(h, w['norm_ffn1'][None], bf(w['ffn1_gate']), bf(w['ffn1_up']), bf(w['ffn1_down']),
             w['norm_ffn1'][None], False)

    pts = np.cumsum((NSA_Q_W,) + (NSA_KV_W,) * 6 + (NSA_GATE_W, SWA_Q_W, SWA_KV_W, SWA_KV_W, d, d))[:-1]
    (w_nq, w_kc, w_vc, w_ksl, w_vsl, w_kwn, w_vwn, w_ng, w_sq, w_sk, w_sv, w_ga, w_gb) = jnp.split(
        bf(w['w_in']), pts, axis=1)
    parts = dict(qn=w_nq, sq=w_sq, ksel=w_ksl, kwin=w_kwn, kswa=w_sk, vsel=w_vsl, vwin=w_vwn, vswa=w_sv,
                 ng=w_ng, kc=w_kc, vc=w_vc)
    w_t = jnp.concatenate([jnp.pad(parts[name], ((0, 0), (0, width - parts[name].shape[1])))
                           for name, width in PROJ_ROWS], axis=1).T
    w_n = jnp.concatenate([w_ga, w_gb], axis=1)

    freq_col = (ROPE_THETA ** (-jnp.arange(HALF_DIM, dtype=F32) / HALF_DIM))[:, None]
    (qn_t, sq_t, ksel, kwin, kswa, vsel, vwin, vswa, ng_t, kcv, gab) = _proj(
        h, w['norm_mix'][None], w_t, w_n, positions.reshape(1, n), freq_col)

    pos_c = jnp.pad(positions[:, CMP_LEN - 1::CMP_STRIDE], ((0, 0), (0, 1)))[:, None, :]
    w1 = lambda a: bf(a.reshape(CMP_LEN, HEAD_DIM, a.shape[-1]))
    kc, vc_t = _compress(kcv, w['cmp_pe_k'], w['cmp_pe_v'], w1(w['cmp_k_w1']), w1(w['cmp_v_w1']),
                         bf(w['cmp_k_w2'].T), bf(w['cmp_v_w2'].T), pos_c, freq_col, b, g_kv, s_len)

    gates = ng_t.reshape(3, g_kv, NSA_REP, n)
    ov = jnp.asarray(_overlap_matrix(n_sel, nc - 1, nc), BF16)
    mb, oc_t = _cmp_topk(qn_t, kc, vc_t, ov, gates, b, g_kv, s_len, n_top)

    expand = jnp.asarray(np.arange(s_len)[:, None] // SEL_LEN == np.arange(n_sel)[None, :], BF16)
    o_a = _nsa(qn_t, mb, oc_t, gates, ksel, expand, vsel, kwin, vwin, b, g_kv, s_len)

    sink_row = jnp.repeat(w['swa_sinks'].astype(F32) * LOG2E, min(SWA_TILE, s_len))[None]
    o_b = _swa(sq_t, sink_row, kswa, vswa, b, s_len)

    branches = (o_a, o_b, gab, bf(w['w_branch_a']), bf(w['w_branch_b']), bf(w['w_out']))
    return _ffn(h, w['norm_ffn2'][None], bf(w['ffn2_gate']), bf(w['ffn2_up']), bf(w['ffn2_down']),
                norm_final[None], last, branches)


def kernel(x, positions, norm_ffn1, ffn1_gate, ffn1_up, ffn1_down, norm_mix, w_in, cmp_pe_k, cmp_k_w1, cmp_k_w2, cmp_pe_v, cmp_v_w1, cmp_v_w2, swa_sinks, w_branch_a, w_branch_b, w_out, norm_ffn2, ffn2_gate, ffn2_up, ffn2_down, norm_final):
    b, s_len, d = x.shape
    stacked = dict(norm_ffn1=norm_ffn1, ffn1_gate=ffn1_gate, ffn1_up=ffn1_up, ffn1_down=ffn1_down,
                   norm_mix=norm_mix, w_in=w_in, cmp_pe_k=cmp_pe_k, cmp_k_w1=cmp_k_w1, cmp_k_w2=cmp_k_w2,
                   cmp_pe_v=cmp_pe_v, cmp_v_w1=cmp_v_w1, cmp_v_w2=cmp_v_w2, swa_sinks=swa_sinks,
                   w_branch_a=w_branch_a, w_branch_b=w_branch_b, w_out=w_out,
                   norm_ffn2=norm_ffn2, ffn2_gate=ffn2_gate, ffn2_up=ffn2_up, ffn2_down=ffn2_down)
    depth = norm_ffn1.shape[0]
    h = x.reshape(b * s_len, d)
    for i in range(depth):
        w = {k: v[i] for k, v in stacked.items()}
        h = _layer(h, positions, w, b, s_len, norm_final, i == depth - 1)
    return h.reshape(b, s_len, d)
```
